```python
import jax
import jax.numpy as jnp
from jax import lax
import numpy as np

D_MODEL = 2048
BATCH = 1
SEQ = 8192
DEPTH = 2

SB_HEADS = 8
SB_HEAD_DIM = 128
SB_WIDTH = SB_HEADS * SB_HEAD_DIM
SG_GROUPS = 8
SG_GROUP_DIM = 128
SG_WIDTH = SG_GROUPS * SG_GROUP_DIM
CHUNK = 128
Q_BLOCK = 128
N_BRANCH = 2
OFF_Q = 0
OFF_K = OFF_Q + SB_WIDTH
OFF_V = OFF_K + SB_WIDTH
OFF_U = OFF_V + SB_WIDTH
OFF_VG = OFF_U + SG_WIDTH
OFF_GATE = OFF_VG + SG_WIDTH
D_IN = OFF_GATE + N_BRANCH * D_MODEL
D_FF_DENSE = 5504
N_EXPERTS = 8
TOP_K = 2
D_FF_EXPERT = 7168
N_DENSE_LAYERS = (DEPTH + 1) // 2
N_MOE_LAYERS = DEPTH // 2
DEEPNORM_ALPHA = (2.0 * DEPTH) ** 0.25
DEEPNORM_BETA = (8.0 * DEPTH) ** -0.25
LN_EPS = 1e-5

kernel_name = 'hybrid_stickbreak_sgu_moe_deepnorm'


def layer_norm(x, gain, bias):
    xf = x.astype(jnp.float32)
    mu = jnp.mean(xf, axis=-1, keepdims=True)
    var = jnp.mean(jnp.square(xf - mu), axis=-1, keepdims=True)
    y = (xf - mu) * lax.rsqrt(var + LN_EPS) * gain.astype(jnp.float32) + bias.astype(jnp.float32)
    return y.astype(x.dtype)


def stick_breaking_attention(q, k, v):
    B, S, H, Dh = q.shape
    n_blocks = S // Q_BLOCK
    scale = Dh ** -0.5
    key_pos = jnp.arange(S)

    def block(i):
        start = i * Q_BLOCK
        qb = lax.dynamic_slice_in_dim(q, start, Q_BLOCK, axis=1)
        z = jnp.einsum('bqhd,bkhd->bhqk', qb, k,
                       preferred_element_type=jnp.float32) * scale
        q_pos = start + jnp.arange(Q_BLOCK)
        before = key_pos[None, :] < q_pos[:, None]
        log_beta = jax.nn.log_sigmoid(z)
        log_keep = jnp.where(before, jax.nn.log_sigmoid(-z), 0.0)
        tail = lax.cumsum(log_keep, axis=3, reverse=True) - log_keep
        a = jnp.where(before, jnp.exp(log_beta + tail), 0.0)
        return jnp.einsum('bhqk,bkhd->bqhd', a.astype(v.dtype), v)

    out = lax.map(block, jnp.arange(n_blocks))
    return jnp.moveaxis(out, 0, 1).reshape(B, S, H * Dh)


def chunked_spatial_gating(u, v, sg_w, sg_b, ln_g, ln_b):
    B, S, G, Dg = v.shape
    v = layer_norm(v, ln_g.reshape(G, Dg), ln_b.reshape(G, Dg))
    vc = v.reshape(B, S // CHUNK, CHUNK, G, Dg)
    causal = jnp.tril(jnp.ones((CHUNK, CHUNK), dtype=bool))
    w = jnp.where(causal, sg_w, 0.0).astype(v.dtype)
    mixed = jnp.einsum('gts,bcsgd->bctgd', w, vc) + sg_b.T[None, None, :, :, None].astype(v.dtype)
    return (u * mixed.reshape(B, S, G, Dg)).reshape(B, S, G * Dg)


def hybrid_mixer(x, w_in, b_gate, sg_w, sg_b, sg_ln_g, sg_ln_b, w_branch_a, w_branch_b, w_out):
    B, S, D = x.shape
    proj = jnp.einsum('bsd,de->bse', x, w_in)
    q = proj[..., OFF_Q:OFF_K].reshape(B, S, SB_HEADS, SB_HEAD_DIM)
    k = proj[..., OFF_K:OFF_V].reshape(B, S, SB_HEADS, SB_HEAD_DIM)
    v = proj[..., OFF_V:OFF_U].reshape(B, S, SB_HEADS, SB_HEAD_DIM)
    u = jax.nn.gelu(proj[..., OFF_U:OFF_VG]).reshape(B, S, SG_GROUPS, SG_GROUP_DIM)
    vg = jax.nn.gelu(proj[..., OFF_VG:OFF_GATE]).reshape(B, S, SG_GROUPS, SG_GROUP_DIM)
    gates = jax.nn.sigmoid(proj[..., OFF_GATE:].reshape(B, S, N_BRANCH, D)
                           + b_gate.reshape(N_BRANCH, D))
    y_a = stick_breaking_attention(q, k, v) @ w_branch_a
    y_b = chunked_spatial_gating(u, vg, sg_w, sg_b, sg_ln_g, sg_ln_b) @ w_branch_b
    merged = gates[:, :, 0, :] * y_a + gates[:, :, 1, :] * y_b
    return merged @ w_out


def swiglu(x, w1, w3, w2):
    return (jax.nn.silu(x @ w1) * (x @ w3)) @ w2


def moe_swiglu(x, w_router, w1, w3, w2):
    logits = jnp.einsum('bsd,de->bse', x, w_router, preferred_element_type=jnp.float32)
    top_logits, top_idx = lax.top_k(logits, TOP_K)
    top_w = jax.nn.softmax(top_logits, axis=-1)
    combine = jnp.sum(jax.nn.one_hot(top_idx, N_EXPERTS, dtype=jnp.float32)
                      * top_w[..., None], axis=-2)
    y = jnp.zeros_like(x)
    for e in range(N_EXPERTS):
        y = y + combine[..., e:e + 1].astype(x.dtype) * swiglu(x, w1[e], w3[e], w2[e])
    return y


def setup_inputs(seed: int = 0) -> dict:
    key = jax.random.key(seed)
    ks = jax.random.split(key, 32)
    L, D = DEPTH, D_MODEL
    beta = DEEPNORM_BETA

    def nrm(k, shape, scale):
        return jax.random.normal(k, shape, jnp.float32) * scale

    x = nrm(ks[0], (BATCH, SEQ, D), 1.0)
    w_qk = nrm(ks[1], (L, D, 2 * SB_WIDTH), D ** -0.5)
    w_v = nrm(ks[2], (L, D, SB_WIDTH), beta * D ** -0.5)
    w_u = nrm(ks[3], (L, D, SG_WIDTH), beta * D ** -0.5)
    w_vg = nrm(ks[4], (L, D, SG_WIDTH), D ** -0.5)
    w_gt = nrm(ks[5], (L, D, N_BRANCH * D), D ** -0.5)
    w_in = jnp.concatenate([w_qk, w_v, w_u, w_vg, w_gt], axis=-1)
    b_gate = nrm(ks[6], (L, N_BRANCH * D), 0.02)
    sg_w = nrm(ks[7], (L, SG_GROUPS, CHUNK, CHUNK), CHUNK ** -0.5)
    sg_b = 1.0 + nrm(ks[8], (L, SG_GROUPS, CHUNK), 0.02)
    sg_ln_g = 1.0 + nrm(ks[9], (L, SG_WIDTH), 0.02)
    sg_ln_b = nrm(ks[10], (L, SG_WIDTH), 0.02)
    w_branch_a = nrm(ks[11], (L, SB_WIDTH, D), beta * SB_WIDTH ** -0.5)
    w_branch_b = nrm(ks[12], (L, SG_WIDTH, D), beta * SG_WIDTH ** -0.5)
    w_out = nrm(ks[13], (L, D, D), beta * D ** -0.5)
    ln1_g = 1.0 + nrm(ks[14], (L, D), 0.02)
    ln1_b = nrm(ks[15], (L, D), 0.02)
    ffn_w1 = nrm(ks[16], (N_DENSE_LAYERS, D, D_FF_DENSE), beta * D ** -0.5)
    ffn_w3 = nrm(ks[17], (N_DENSE_LAYERS, D, D_FF_DENSE), beta * D ** -0.5)
    ffn_w2 = nrm(ks[18], (N_DENSE_LAYERS, D_FF_DENSE, D), beta * D_FF_DENSE ** -0.5)
    moe_router = nrm(ks[19], (N_MOE_LAYERS, D, N_EXPERTS), D ** -0.5)
    moe_w1 = nrm(ks[20], (N_MOE_LAYERS, N_EXPERTS, D, D_FF_EXPERT), beta * D ** -0.5)
    moe_w3 = nrm(ks[21], (N_MOE_LAYERS, N_EXPERTS, D, D_FF_EXPERT), beta * D ** -0.5)
    moe_w2 = nrm(ks[22], (N_MOE_LAYERS, N_EXPERTS, D_FF_EXPERT, D), beta * D_FF_EXPERT ** -0.5)
    ln2_g = 1.0 + nrm(ks[23], (L, D), 0.02)
    ln2_b = nrm(ks[24], (L, D), 0.02)
    return {'x': x, 'w_in': w_in, 'b_gate': b_gate, 'sg_w': sg_w, 'sg_b': sg_b,
            'sg_ln_g': sg_ln_g, 'sg_ln_b': sg_ln_b, 'w_branch_a': w_branch_a,
            'w_branch_b': w_branch_b, 'w_out': w_out, 'ln1_g': ln1_g, 'ln1_b': ln1_b,
            'ffn_w1': ffn_w1, 'ffn_w3': ffn_w3, 'ffn_w2': ffn_w2, 'moe_router': moe_router,
            'moe_w1': moe_w1, 'moe_w3': moe_w3, 'moe_w2': moe_w2, 'ln2_g': ln2_g, 'ln2_b': ln2_b}


def reference(x, w_in, b_gate, sg_w, sg_b, sg_ln_g, sg_ln_b, w_branch_a, w_branch_b, w_out,
              ln1_g, ln1_b, ffn_w1, ffn_w3, ffn_w2, moe_router, moe_w1, moe_w3, moe_w2,
              ln2_g, ln2_b):
    for layer in range(DEPTH):
        mix = hybrid_mixer(x, w_in[layer], b_gate[layer], sg_w[layer], sg_b[layer],
                           sg_ln_g[layer], sg_ln_b[layer], w_branch_a[layer],
                           w_branch_b[layer], w_out[layer])
        x = layer_norm(DEEPNORM_ALPHA * x + mix, ln1_g[layer], ln1_b[layer])
        i = layer // 2
        if layer % 2 == 0:
            f = swiglu(x, ffn_w1[i], ffn_w3[i], ffn_w2[i])
        else:
            f = moe_swiglu(x, moe_router[i], moe_w1[i], moe_w3[i], moe_w2[i])
        x = layer_norm(DEEPNORM_ALPHA * x + f, ln2_g[layer], ln2_b[layer])
    return x
```

```python
import functools

import jax
import jax.numpy as jnp
from jax import lax
from jax.experimental import pallas as pl
from jax.experimental.pallas import tpu as pltpu

F32 = jnp.float32
BF16 = jnp.bfloat16

D_MODEL = 2048
SEQ = 8192
DEPTH = 2
HEADS = 8
HEAD_DIM = 128
SB_WIDTH = HEADS * HEAD_DIM
GROUPS = 8
GROUP_DIM = 128
SG_WIDTH = GROUPS * GROUP_DIM
CHUNK = 128
OFF_U = 3 * SB_WIDTH
OFF_VG = OFF_U + SG_WIDTH
OFF_GATE = OFF_VG + SG_WIDTH
D_FF_DENSE = 5504
N_EXPERTS = 8
D_FF_EXPERT = 7168
ALPHA = (2.0 * DEPTH) ** 0.25
LN_EPS = 1e-5

LANES = 128
V7X_VMEM_BYTES = 64 * 1024 * 1024

PROJ_TM = 1024
PROJ_TN = 1024
ATT_T = 128
SGU_ROWS = 512
MERGE_TM = 256
FFN_TM = 512
FFN_TF = 256
ROUTER_TM = 512


def _params(semantics, vmem_bytes):
    assert vmem_bytes < V7X_VMEM_BYTES
    return pltpu.CompilerParams(dimension_semantics=semantics, vmem_limit_bytes=vmem_bytes)


def _dot(a, b):
    return jnp.dot(a, b, preferred_element_type=F32)


def _layer_norm(y, g, b):
    mu = jnp.mean(y, axis=-1, keepdims=True)
    d = y - mu
    var = jnp.mean(d * d, axis=-1, keepdims=True)
    return d * lax.rsqrt(var + LN_EPS) * g + b


def _gelu_tanh(x):
    return 0.5 * x * (1.0 + jnp.tanh(0.7978845608028654 * (x + 0.044715 * (x * x * x))))


def _inproj_kernel(x_ref, w_ref, *rest, mode):
    acc = _dot(x_ref[...], w_ref[...])
    if mode == "qkv":
        (o_ref,) = rest
        scale = jnp.where(pl.program_id(0) == 0, HEAD_DIM ** -0.5, 1.0).astype(F32)
        o_ref[...] = (acc * scale).astype(o_ref.dtype)
    elif mode == "gelu":
        (o_ref,) = rest
        o_ref[...] = _gelu_tanh(acc).astype(o_ref.dtype)
    elif mode == "gelu_ln":
        g_ref, b_ref, o_ref = rest
        act = _gelu_tanh(acc)
        for grp in range(acc.shape[1] // GROUP_DIM):
            cols = slice(grp * GROUP_DIM, (grp + 1) * GROUP_DIM)
            o_ref[:, cols] = _layer_norm(act[:, cols], g_ref[:, cols], b_ref[:, cols]).astype(o_ref.dtype)
    elif mode == "gate":
        b_ref, o_ref = rest
        o_ref[...] = jax.nn.sigmoid(acc + b_ref[...]).astype(o_ref.dtype)
    else:
        raise ValueError(mode)


def _inproj(x16, w16, col_off, width, mode, extra=()):
    s, d = x16.shape
    tm, tn = PROJ_TM, PROJ_TN
    n_blk = width // tn
    off_blk = col_off // tn
    in_specs = [
        pl.BlockSpec((tm, d), lambda n, m: (m, 0)),
        pl.BlockSpec((d, tn), lambda n, m: (0, off_blk + n)),
    ]
    for _ in extra:
        in_specs.append(pl.BlockSpec((1, tn), lambda n, m: (0, n)))
    vmem = 2 * (tm * d * 2 + d * tn * 2 + tm * tn * 2) + 6 * tm * tn * 4
    return pl.pallas_call(
        functools.partial(_inproj_kernel, mode=mode),
        grid=(n_blk, s // tm),
        in_specs=in_specs,
        out_specs=pl.BlockSpec((tm, tn), lambda n, m: (m, n)),
        out_shape=jax.ShapeDtypeStruct((s, width), BF16),
        compiler_params=_params(("arbitrary", "arbitrary"), vmem),
        name="inproj_" + mode,
    )(x16, w16, *extra)


def _sb_tile(q, k, v, c, later, diag_mask):
    z = lax.dot_general(q, k, (((1,), (1,)), ((), ())), preferred_element_type=F32)
    ls = jnp.minimum(z, 0.0) - jnp.log1p(jnp.exp(-jnp.abs(z)))
    lk = ls - z
    if diag_mask is not None:
        lk = jnp.where(diag_mask, lk, 0.0)
    hi = lk.astype(BF16)
    lo = (lk - hi.astype(F32)).astype(BF16)
    tail = _dot(hi, later) + _dot(lo, later)
    a = jnp.exp(ls + tail + c)
    if diag_mask is not None:
        a = jnp.where(diag_mask, a, 0.0)
    pv = _dot(a.astype(BF16), v)
    return pv, c + tail[:, :1] + lk[:, :1]


def _attn_kernel(q_ref, k_ref, v_ref, o_ref):
    t = ATT_T
    i = pl.program_id(1)
    q = q_ref[...]
    row = lax.broadcasted_iota(jnp.int32, (t, t), 0)
    col = lax.broadcasted_iota(jnp.int32, (t, t), 1)
    later = jnp.where(row > col, 1.0, 0.0).astype(BF16)
    before = col < row

    def kv(j):
        start = pl.multiple_of(j * t, t)
        return k_ref[pl.ds(start, t), :], v_ref[pl.ds(start, t), :]

    k0, v0 = kv(i)
    acc, c = _sb_tile(q, k0, v0, jnp.zeros((t, 1), F32), later, before)

    def body(jj, carry):
        acc, c = carry
        kj, vj = kv(i - 1 - jj)
        pv, c = _sb_tile(q, kj, vj, c, later, None)
        return acc + pv, c

    acc, c = lax.fori_loop(0, i, body, (acc, c))
    o_ref[...] = acc.astype(o_ref.dtype)


def _attention(qkv):
    s = qkv.shape[0]
    t = ATT_T
    vmem = 2 * (2 * s * HEAD_DIM * 2) + 32 * t * t * 4 + (4 << 20)
    return pl.pallas_call(
        _attn_kernel,
        grid=(HEADS, s // t),
        in_specs=[
            pl.BlockSpec((t, HEAD_DIM), lambda h, i: (i, h)),
            pl.BlockSpec((s, HEAD_DIM), lambda h, i: (0, HEADS + h)),
            pl.BlockSpec((s, HEAD_DIM), lambda h, i: (0, 2 * HEADS + h)),
        ],
        out_specs=pl.BlockSpec((t, HEAD_DIM), lambda h, i: (i, h)),
        out_shape=jax.ShapeDtypeStruct((s, SB_WIDTH), BF16),
        compiler_params=_params(("arbitrary", "arbitrary"), vmem),
        name="stickbreak_attn",
    )(qkv, qkv, qkv)


def _sgu_kernel(u_ref, v_ref, w_ref, b_ref, o_ref):
    c = CHUNK
    row = lax.broadcasted_iota(jnp.int32, (c, c), 0)
    col = lax.broadcasted_iota(jnp.int32, (c, c), 1)
    causal = col <= row
    for g in range(GROUPS):
        w = jnp.where(causal, w_ref[g], 0.0).astype(BF16)
        b = b_ref[:, g:g + 1]
        cols = slice(g * GROUP_DIM, (g + 1) * GROUP_DIM)
        for cc in range(SGU_ROWS // c):
            rows = slice(cc * c, (cc + 1) * c)
            mixed = _dot(w, v_ref[rows, cols]) + b
            o_ref[rows, cols] = (u_ref[rows, cols].astype(F32) * mixed).astype(o_ref.dtype)


def _sgu(u, vn, sg_w, sg_b_t):
    s = u.shape[0]
    r = SGU_ROWS
    vmem = 2 * 3 * r * SG_WIDTH * 2 + 2 * GROUPS * CHUNK * CHUNK * 4 + (4 << 20)
    return pl.pallas_call(
        _sgu_kernel,
        grid=(s // r,),
        in_specs=[
            pl.BlockSpec((r, SG_WIDTH), lambda i: (i, 0)),
            pl.BlockSpec((r, SG_WIDTH), lambda i: (i, 0)),
            pl.BlockSpec((GROUPS, CHUNK, CHUNK), lambda i: (0, 0, 0)),
            pl.BlockSpec((CHUNK, GROUPS), lambda i: (0, 0)),
        ],
        out_specs=pl.BlockSpec((r, SG_WIDTH), lambda i: (i, 0)),
        out_shape=jax.ShapeDtypeStruct((s, SG_WIDTH), BF16),
        compiler_params=_params(("arbitrary",), vmem),
        name="spatial_gating",
    )(u, vn, sg_w, sg_b_t)


def _merge_kernel(a_ref, b_ref, gate_ref, x_ref, wa_ref, wb_ref, wo_ref, g_ref, beta_ref,
                  o32_ref, o16_ref):
    d = D_MODEL
    ya = _dot(a_ref[...], wa_ref[...])
    yb = _dot(b_ref[...], wb_ref[...])
    merged = gate_ref[:, :d].astype(F32) * ya + gate_ref[:, d:].astype(F32) * yb
    mix = _dot(merged.astype(BF16), wo_ref[...])
    y = _layer_norm(ALPHA * x_ref[...] + mix, g_ref[...], beta_ref[...])
    o32_ref[...] = y
    o16_ref[...] = y.astype(BF16)


def _merge(att, sgu, gates, x32, wa, wb, wo, ln_g, ln_b):
    s, d = x32.shape
    tm = MERGE_TM
    const = lambda i: (0, 0)
    rowblk = lambda i: (i, 0)
    weights = (SB_WIDTH * d + SG_WIDTH * d + d * d) * 2
    vmem = 2 * weights + 2 * tm * (SB_WIDTH * 2 + SG_WIDTH * 2 + 2 * d * 2 + d * 4 + d * 4 + d * 2) \
        + 6 * tm * d * 4
    return pl.pallas_call(
        _merge_kernel,
        grid=(s // tm,),
        in_specs=[
            pl.BlockSpec((tm, SB_WIDTH), rowblk),
            pl.BlockSpec((tm, SG_WIDTH), rowblk),
            pl.BlockSpec((tm, 2 * d), rowblk),
            pl.BlockSpec((tm, d), rowblk),
            pl.BlockSpec((SB_WIDTH, d), const),
            pl.BlockSpec((SG_WIDTH, d), const),
            pl.BlockSpec((d, d), const),
            pl.BlockSpec((1, d), const),
            pl.BlockSpec((1, d), const),
        ],
        out_specs=[pl.BlockSpec((tm, d), rowblk), pl.BlockSpec((tm, d), rowblk)],
        out_shape=[jax.ShapeDtypeStruct((s, d), F32), jax.ShapeDtypeStruct((s, d), BF16)],
        compiler_params=_params(("arbitrary",), vmem),
        name="merge_outproj_ln",
    )(att, sgu, gates, x32, wa, wb, wo, ln_g, ln_b)


def _ffn_kernel(*refs, weighted):
    if weighted:
        x16_ref, x32_ref, comb_ref, w1_ref, w3_ref, w2_ref, g_ref, b_ref, o32_ref, o16_ref, acc_ref = refs
    else:
        x16_ref, x32_ref, w1_ref, w3_ref, w2_ref, g_ref, b_ref, o32_ref, o16_ref, acc_ref = refs
    e = pl.program_id(1)
    f = pl.program_id(2)

    @pl.when((e == 0) & (f == 0))
    def _():
        acc_ref[...] = jnp.zeros_like(acc_ref)

    x = x16_ref[...]
    h1 = _dot(x, w1_ref[...])
    h3 = _dot(x, w3_ref[...])
    h = h1 * jax.nn.sigmoid(h1) * h3
    if weighted:
        lane = lax.broadcasted_iota(jnp.int32, comb_ref.shape, 1)
        h = h * jnp.sum(jnp.where(lane == e, comb_ref[...], 0.0), axis=-1, keepdims=True)
    acc_ref[...] += _dot(h.astype(BF16), w2_ref[...])

    @pl.when((e == pl.num_programs(1) - 1) & (f == pl.num_programs(2) - 1))
    def _():
        y = _layer_norm(ALPHA * x32_ref[...] + acc_ref[...], g_ref[...], b_ref[...])
        o32_ref[...] = y
        o16_ref[...] = y.astype(BF16)


def _ffn(x16, x32, w1, w3, w2, ln_g, ln_b, comb=None):
    s, d = x32.shape
    n_e, _, ff = w1.shape
    tm, tf = FFN_TM, FFN_TF
    rowblk = lambda i, e, f: (i, 0)
    const = lambda i, e, f: (0, 0)
    in_specs = [pl.BlockSpec((tm, d), rowblk), pl.BlockSpec((tm, d), rowblk)]
    args = [x16, x32]
    if comb is not None:
        in_specs.append(pl.BlockSpec((tm, LANES), rowblk))
        args.append(comb)
    in_specs += [
        pl.BlockSpec((None, d, tf), lambda i, e, f: (e, 0, f)),
        pl.BlockSpec((None, d, tf), lambda i, e, f: (e, 0, f)),
        pl.BlockSpec((None, tf, d), lambda i, e, f: (e, f, 0)),
        pl.BlockSpec((1, d), const),
        pl.BlockSpec((1, d), const),
    ]
    args += [w1, w3, w2, ln_g, ln_b]
    vmem = 2 * (tm * d * (2 + 4 + 4 + 2) + 3 * d * tf * 2) + tm * d * 4 + 4 * tm * tf * 4 + 3 * tm * d * 4
    return pl.pallas_call(
        functools.partial(_ffn_kernel, weighted=comb is not None),
        grid=(s // tm, n_e, ff // tf),
        in_specs=in_specs,
        out_specs=[pl.BlockSpec((tm, d), rowblk), pl.BlockSpec((tm, d), rowblk)],
        out_shape=[jax.ShapeDtypeStruct((s, d), F32), jax.ShapeDtypeStruct((s, d), BF16)],
        scratch_shapes=[pltpu.VMEM((tm, d), F32)],
        compiler_params=_params(("arbitrary", "arbitrary", "arbitrary"), vmem),
        name="swiglu_ln",
    )(*args)


def _router_kernel(x_ref, w_ref, comb_ref):
    logits = jnp.dot(x_ref[...], w_ref[...], preferred_element_type=F32,
                     precision=lax.Precision.HIGHEST)
    lane = lax.broadcasted_iota(jnp.int32, logits.shape, 1).astype(F32)
    neg = jnp.float32(-jnp.inf)
    l1 = jnp.where(lane < N_EXPERTS, logits, neg)
    m1 = jnp.max(l1, axis=-1, keepdims=True)
    i1 = jnp.min(jnp.where(l1 == m1, lane, float(LANES)), axis=-1, keepdims=True)
    l2 = jnp.where(lane == i1, neg, l1)
    m2 = jnp.max(l2, axis=-1, keepdims=True)
    i2 = jnp.min(jnp.where(l2 == m2, lane, float(LANES)), axis=-1, keepdims=True)
    e2 = jnp.exp(m2 - m1)
    w_top = 1.0 / (1.0 + e2)
    comb_ref[...] = jnp.where(lane == i1, w_top, 0.0) + jnp.where(lane == i2, e2 * w_top, 0.0)


def _router(x32, w_router_padded):
    s, d = x32.shape
    tm = ROUTER_TM
    vmem = 2 * (tm * d * 4 + d * LANES * 4 + tm * LANES * 4) + 8 * tm * d * 4
    return pl.pallas_call(
        _router_kernel,
        grid=(s // tm,),
        in_specs=[pl.BlockSpec((tm, d), lambda i: (i, 0)), pl.BlockSpec((d, LANES), lambda i: (0, 0))],
        out_specs=pl.BlockSpec((tm, LANES), lambda i: (i, 0)),
        out_shape=jax.ShapeDtypeStruct((s, LANES), F32),
        compiler_params=_params(("arbitrary",), vmem),
        name="router_top2",
    )(x32, w_router_padded)


def _pad_ff(w, axis, mult):
    pad = (-w.shape[axis]) % mult
    if pad == 0:
        return w
    widths = [(0, 0)] * w.ndim
    widths[axis] = (0, pad)
    return jnp.pad(w, widths)


def kernel(x, w_in, b_gate, sg_w, sg_b, sg_ln_g, sg_ln_b, w_branch_a, w_branch_b, w_out,
           ln1_g, ln1_b, ffn_w1, ffn_w3, ffn_w2, moe_router, moe_w1, moe_w3, moe_w2,
           ln2_g, ln2_b):
    b, s, d = x.shape
    assert (b, s, d) == (1, SEQ, D_MODEL)
    x32 = x.reshape(s, d)
    x16 = x32.astype(BF16)
    for layer in range(DEPTH):
        w16 = w_in[layer].astype(BF16)
        qkv = _inproj(x16, w16, 0, 3 * SB_WIDTH, "qkv")
        u = _inproj(x16, w16, OFF_U, SG_WIDTH, "gelu")
        vn = _inproj(x16, w16, OFF_VG, SG_WIDTH, "gelu_ln",
                     (sg_ln_g[layer].reshape(1, -1), sg_ln_b[layer].reshape(1, -1)))
        gates = _inproj(x16, w16, OFF_GATE, 2 * d, "gate", (b_gate[layer].reshape(1, -1),))
        att = _attention(qkv)
        sgu = _sgu(u, vn, sg_w[layer], sg_b[layer].T)
        x32, x16 = _merge(att, sgu, gates, x32,
                          w_branch_a[layer].astype(BF16), w_branch_b[layer].astype(BF16),
                          w_out[layer].astype(BF16),
                          ln1_g[layer].reshape(1, d), ln1_b[layer].reshape(1, d))
        i = layer // 2
        g2, b2 = ln2_g[layer].reshape(1, d), ln2_b[layer].reshape(1, d)
        if layer % 2 == 0:
            w1 = _pad_ff(ffn_w1[i].astype(BF16), 1, FFN_TF)[None]
            w3 = _pad_ff(ffn_w3[i].astype(BF16), 1, FFN_TF)[None]
            w2 = _pad_ff(ffn_w2[i].astype(BF16), 0, FFN_TF)[None]
            x32, x16 = _ffn(x16, x32, w1, w3, w2, g2, b2)
        else:
            wr = jnp.pad(moe_router[i], ((0, 0), (0, LANES - N_EXPERTS)))
            comb = _router(x32, wr)
            x32, x16 = _ffn(x16, x32, moe_w1[i].astype(BF16), moe_w3[i].astype(BF16),
                            moe_w2[i].astype(BF16), g2, b2, comb)
    return x32.reshape(b, s, d)
```

```python
import functools

import jax
import jax.numpy as jnp
from jax import lax
from jax.experimental import pallas as pl
from jax.experimental.pallas import tpu as pltpu

F32 = jnp.float32
BF16 = jnp.bfloat16

D_MODEL = 2048
SEQ = 8192
DEPTH = 2
HEADS = 8
HEAD_DIM = 128
SB_WIDTH = HEADS * HEAD_DIM
GROUPS = 8
GROUP_DIM = 128
SG_WIDTH = GROUPS * GROUP_DIM
CHUNK = 128
OFF_U = 3 * SB_WIDTH
OFF_VG = OFF_U + SG_WIDTH
OFF_GATE = OFF_VG + SG_WIDTH
D_FF_DENSE = 5504
N_EXPERTS = 8
D_FF_EXPERT = 7168
ALPHA = (2.0 * DEPTH) ** 0.25
LN_EPS = 1e-5
SB_UNDERFLOW = 110.0

LANES = 128
V7X_VMEM_BYTES = 64 * 1024 * 1024

PROJ_TM = 1024
PROJ_TN = 1024
ATT_T = 128
ATT_HEADS = 4
SGU_ROWS = 512
MERGE_TM = 256
FFN_TM = 512
FFN_TF = 256
ROUTER_TM = 512


def _params(semantics, vmem_bytes):
    assert vmem_bytes < V7X_VMEM_BYTES
    return pltpu.CompilerParams(dimension_semantics=semantics, vmem_limit_bytes=vmem_bytes)


def _dot(a, b):
    return jnp.dot(a, b, preferred_element_type=F32)


def _layer_norm(y, g, b):
    mu = jnp.mean(y, axis=-1, keepdims=True)
    d = y - mu
    var = jnp.mean(d * d, axis=-1, keepdims=True)
    return d * lax.rsqrt(var + LN_EPS) * g + b


def _gelu_tanh(x):
    return 0.5 * x * (1.0 + jnp.tanh(0.7978845608028654 * (x + 0.044715 * (x * x * x))))


def _inproj_kernel(x_ref, w_ref, *rest, mode):
    acc = _dot(x_ref[...], w_ref[...])
    if mode == "qkv":
        (o_ref,) = rest
        scale = jnp.where(pl.program_id(0) == 0, HEAD_DIM ** -0.5, 1.0).astype(F32)
        o_ref[...] = (acc * scale).astype(o_ref.dtype)
    elif mode == "gelu":
        (o_ref,) = rest
        o_ref[...] = _gelu_tanh(acc).astype(o_ref.dtype)
    elif mode == "gelu_ln":
        g_ref, b_ref, o_ref = rest
        act = _gelu_tanh(acc)
        for grp in range(acc.shape[1] // GROUP_DIM):
            cols = slice(grp * GROUP_DIM, (grp + 1) * GROUP_DIM)
            o_ref[:, cols] = _layer_norm(act[:, cols], g_ref[:, cols], b_ref[:, cols]).astype(o_ref.dtype)
    elif mode == "gate":
        b_ref, o_ref = rest
        o_ref[...] = jax.nn.sigmoid(acc + b_ref[...]).astype(o_ref.dtype)
    else:
        raise ValueError(mode)


def _inproj(x16, w16, col_off, width, mode, extra=()):
    s, d = x16.shape
    tm, tn = PROJ_TM, PROJ_TN
    n_blk = width // tn
    off_blk = col_off // tn
    in_specs = [
        pl.BlockSpec((tm, d), lambda n, m: (m, 0)),
        pl.BlockSpec((d, tn), lambda n, m: (0, off_blk + n)),
    ]
    for _ in extra:
        in_specs.append(pl.BlockSpec((1, tn), lambda n, m: (0, n)))
    vmem = 2 * (tm * d * 2 + d * tn * 2 + tm * tn * 2) + 6 * tm * tn * 4
    return pl.pallas_call(
        functools.partial(_inproj_kernel, mode=mode),
        grid=(n_blk, s // tm),
        in_specs=in_specs,
        out_specs=pl.BlockSpec((tm, tn), lambda n, m: (m, n)),
        out_shape=jax.ShapeDtypeStruct((s, width), BF16),
        compiler_params=_params(("arbitrary", "arbitrary"), vmem),
        name="inproj_" + mode,
    )(x16, w16, *extra)


def _sb_tile(q, k, v, c, later, diag_mask):
    z = lax.dot_general(q, k, (((1,), (1,)), ((), ())), preferred_element_type=F32)
    ls = jnp.minimum(z, 0.0) - jnp.log1p(jnp.exp(-jnp.abs(z)))
    lk = ls - z
    if diag_mask is not None:
        lk = jnp.where(diag_mask, lk, 0.0)
    hi = lk.astype(BF16)
    lo = (lk - hi.astype(F32)).astype(BF16)
    tail = _dot(hi, later) + _dot(lo, later)
    a = jnp.exp(ls + tail + c)
    if diag_mask is not None:
        a = jnp.where(diag_mask, a, 0.0)
    pv = _dot(a.astype(BF16), v)
    return pv, c + tail[:, :1] + lk[:, :1]


def _attn_kernel(q_ref, k_ref, v_ref, o_ref):
    t = ATT_T
    i = pl.program_id(1)
    row = lax.broadcasted_iota(jnp.int32, (t, t), 0)
    col = lax.broadcasted_iota(jnp.int32, (t, t), 1)
    later = jnp.where(row > col, 1.0, 0.0).astype(BF16)
    before = col < row
    heads = [slice(h * HEAD_DIM, (h + 1) * HEAD_DIM) for h in range(ATT_HEADS)]
    qs = [q_ref[:, hs] for hs in heads]

    def key_tile(j, accs, cs, mask):
        start = pl.multiple_of(j * t, t)
        out = [_sb_tile(q, k_ref[pl.ds(start, t), hs], v_ref[pl.ds(start, t), hs], c, later, mask)
               for q, hs, c in zip(qs, heads, cs)]
        return tuple(a + pv for a, (pv, _) in zip(accs, out)), tuple(c for _, c in out)

    def any_weight_left(cs):
        return jnp.max(functools.reduce(jnp.maximum, cs)) > -SB_UNDERFLOW

    zeros = tuple(jnp.zeros((t, HEAD_DIM), F32) for _ in heads)
    accs, cs = key_tile(i, zeros, tuple(jnp.zeros((t, 1), F32) for _ in heads), before)

    def cond(carry):
        j, go, _, _ = carry
        return jnp.logical_and(j >= 0, go)

    def body(carry):
        j, _, accs, cs = carry
        accs, cs = key_tile(j, accs, cs, None)
        return j - 1, any_weight_left(cs), accs, cs

    _, _, accs, _ = lax.while_loop(cond, body, (i - 1, any_weight_left(cs), accs, cs))
    for hs, acc in zip(heads, accs):
        o_ref[:, hs] = acc.astype(o_ref.dtype)


def _attention(qkv):
    s = qkv.shape[0]
    t = ATT_T
    w = ATT_HEADS * HEAD_DIM
    groups = SB_WIDTH // w
    vmem = 2 * (2 * s * w * 2 + 2 * t * w * 2) + ATT_HEADS * 16 * t * t * 4 + (4 << 20)
    return pl.pallas_call(
        _attn_kernel,
        grid=(groups, s // t),
        in_specs=[
            pl.BlockSpec((t, w), lambda g, i: (i, g)),
            pl.BlockSpec((s, w), lambda g, i: (0, groups + g)),
            pl.BlockSpec((s, w), lambda g, i: (0, 2 * groups + g)),
        ],
        out_specs=pl.BlockSpec((t, w), lambda g, i: (i, g)),
        out_shape=jax.ShapeDtypeStruct((s, SB_WIDTH), BF16),
        compiler_params=_params(("arbitrary", "arbitrary"), vmem),
        name="stickbreak_attn",
    )(qkv, qkv, qkv)


def _sgu_kernel(u_ref, v_ref, w_ref, b_ref, o_ref):
    c = CHUNK
    row = lax.broadcasted_iota(jnp.int32, (c, c), 0)
    col = lax.broadcasted_iota(jnp.int32, (c, c), 1)
    causal = col <= row
    for g in range(GROUPS):
        w = jnp.where(causal, w_ref[g], 0.0).astype(BF16)
        b = b_ref[:, g:g + 1]
        cols = slice(g * GROUP_DIM, (g + 1) * GROUP_DIM)
        for cc in range(SGU_ROWS // c):
            rows = slice(cc * c, (cc + 1) * c)
            mixed = _dot(w, v_ref[rows, cols]) + b
            o_ref[rows, cols] = (u_ref[rows, cols].astype(F32) * mixed).astype(o_ref.dtype)


def _sgu(u, vn, sg_w, sg_b_t):
    s = u.shape[0]
    r = SGU_ROWS
    vmem = 2 * 3 * r * SG_WIDTH * 2 + 2 * GROUPS * CHUNK * CHUNK * 4 + (4 << 20)
    return pl.pallas_call(
        _sgu_kernel,
        grid=(s // r,),
        in_specs=[
            pl.BlockSpec((r, SG_WIDTH), lambda i: (i, 0)),
            pl.BlockSpec((r, SG_WIDTH), lambda i: (i, 0)),
            pl.BlockSpec((GROUPS, CHUNK, CHUNK), lambda i: (0, 0, 0)),
            pl.BlockSpec((CHUNK, GROUPS), lambda i: (0, 0)),
        ],
        out_specs=pl.BlockSpec((r, SG_WIDTH), lambda i: (i, 0)),
        out_shape=jax.ShapeDtypeStruct((s, SG_WIDTH), BF16),
        compiler_params=_params(("arbitrary",), vmem),
        name="spatial_gating",
    )(u, vn, sg_w, sg_b_t)


def _merge_kernel(a_ref, b_ref, gate_ref, x_ref, wa_ref, wb_ref, wo_ref, g_ref, beta_ref,
                  o32_ref, o16_ref):
    d = D_MODEL
    ya = _dot(a_ref[...], wa_ref[...])
    yb = _dot(b_ref[...], wb_ref[...])
    merged = gate_ref[:, :d].astype(F32) * ya + gate_ref[:, d:].astype(F32) * yb
    mix = _dot(merged.astype(BF16), wo_ref[...])
    y = _layer_norm(ALPHA * x_ref[...] + mix, g_ref[...], beta_ref[...])
    o32_ref[...] = y
    o16_ref[...] = y.astype(BF16)


def _merge(att, sgu, gates, x32, wa, wb, wo, ln_g, ln_b):
    s, d = x32.shape
    tm = MERGE_TM
    const = lambda i: (0, 0)
    rowblk = lambda i: (i, 0)
    weights = (SB_WIDTH * d + SG_WIDTH * d + d * d) * 2
    vmem = 2 * weights + 2 * tm * (SB_WIDTH * 2 + SG_WIDTH * 2 + 2 * d * 2 + d * 4 + d * 4 + d * 2) \
        + 6 * tm * d * 4
    return pl.pallas_call(
        _merge_kernel,
        grid=(s // tm,),
        in_specs=[
            pl.BlockSpec((tm, SB_WIDTH), rowblk),
            pl.BlockSpec((tm, SG_WIDTH), rowblk),
            pl.BlockSpec((tm, 2 * d), rowblk),
            pl.BlockSpec((tm, d), rowblk),
            pl.BlockSpec((SB_WIDTH, d), const),
            pl.BlockSpec((SG_WIDTH, d), const),
            pl.BlockSpec((d, d), const),
            pl.BlockSpec((1, d), const),
            pl.BlockSpec((1, d), const),
        ],
        out_specs=[pl.BlockSpec((tm, d), rowblk), pl.BlockSpec((tm, d), rowblk)],
        out_shape=[jax.ShapeDtypeStruct((s, d), F32), jax.ShapeDtypeStruct((s, d), BF16)],
        compiler_params=_params(("arbitrary",), vmem),
        name="merge_outproj_ln",
    )(att, sgu, gates, x32, wa, wb, wo, ln_g, ln_b)


def _ffn_kernel(*refs, weighted):
    if weighted:
        x16_ref, x32_ref, comb_ref, w1_ref, w3_ref, w2_ref, g_ref, b_ref, o32_ref, o16_ref, acc_ref = refs
    else:
        x16_ref, x32_ref, w1_ref, w3_ref, w2_ref, g_ref, b_ref, o32_ref, o16_ref, acc_ref = refs
    e = pl.program_id(1)
    f = pl.program_id(2)

    @pl.when((e == 0) & (f == 0))
    def _():
        acc_ref[...] = jnp.zeros_like(acc_ref)

    x = x16_ref[...]
    h1 = _dot(x, w1_ref[...])
    h3 = _dot(x, w3_ref[...])
    h = h1 * jax.nn.sigmoid(h1) * h3
    if weighted:
        lane = lax.broadcasted_iota(jnp.int32, comb_ref.shape, 1)
        h = h * jnp.sum(jnp.where(lane == e, comb_ref[...], 0.0), axis=-1, keepdims=True)
    acc_ref[...] += _dot(h.astype(BF16), w2_ref[...])

    @pl.when((e == pl.num_programs(1) - 1) & (f == pl.num_programs(2) - 1))
    def _():
        y = _layer_norm(ALPHA * x32_ref[...] + acc_ref[...], g_ref[...], b_ref[...])
        o32_ref[...] = y
        o16_ref[...] = y.astype(BF16)


def _ffn(x16, x32, w1, w3, w2, ln_g, ln_b, comb=None):
    s, d = x32.shape
    n_e, _, ff = w1.shape
    tm, tf = FFN_TM, FFN_TF
    rowblk = lambda i, e, f: (i, 0)
    const = lambda i, e, f: (0, 0)
    in_specs = [pl.BlockSpec((tm, d), rowblk), pl.BlockSpec((tm, d), rowblk)]
    args = [x16, x32]
    if comb is not None:
        in_specs.append(pl.BlockSpec((tm, LANES), rowblk))
        args.append(comb)
    in_specs += [
        pl.BlockSpec((None, d, tf), lambda i, e, f: (e, 0, f)),
        pl.BlockSpec((None, d, tf), lambda i, e, f: (e, 0, f)),
        pl.BlockSpec((None, tf, d), lambda i, e, f: (e, f, 0)),
        pl.BlockSpec((1, d), const),
        pl.BlockSpec((1, d), const),
    ]
    args += [w1, w3, w2, ln_g, ln_b]
    vmem = 2 * (tm * d * (2 + 4 + 4 + 2) + 3 * d * tf * 2) + tm * d * 4 + 4 * tm * tf * 4 + 3 * tm * d * 4
    return pl.pallas_call(
        functools.partial(_ffn_kernel, weighted=comb is not None),
        grid=(s // tm, n_e, ff // tf),
        in_specs=in_specs,
        out_specs=[pl.BlockSpec((tm, d), rowblk), pl.BlockSpec((tm, d), rowblk)],
        out_shape=[jax.ShapeDtypeStruct((s, d), F32), jax.ShapeDtypeStruct((s, d), BF16)],
        scratch_shapes=[pltpu.VMEM((tm, d), F32)],
        compiler_params=_params(("arbitrary", "arbitrary", "arbitrary"), vmem),
        name="swiglu_ln",
    )(*args)


def _router_kernel(x_ref, w_ref, comb_ref):
    logits = jnp.dot(x_ref[...], w_ref[...], preferred_element_type=F32,
                     precision=lax.Precision.HIGHEST)
    lane = lax.broadcasted_iota(jnp.int32, logits.shape, 1).astype(F32)
    neg = jnp.float32(-jnp.inf)
    l1 = jnp.where(lane < N_EXPERTS, logits, neg)
    m1 = jnp.max(l1, axis=-1, keepdims=True)
    i1 = jnp.min(jnp.where(l1 == m1, lane, float(LANES)), axis=-1, keepdims=True)
    l2 = jnp.where(lane == i1, neg, l1)
    m2 = jnp.max(l2, axis=-1, keepdims=True)
    i2 = jnp.min(jnp.where(l2 == m2, lane, float(LANES)), axis=-1, keepdims=True)
    e2 = jnp.exp(m2 - m1)
    w_top = 1.0 / (1.0 + e2)
    comb_ref[...] = jnp.where(lane == i1, w_top, 0.0) + jnp.where(lane == i2, e2 * w_top, 0.0)


def _router(x32, w_router_padded):
    s, d = x32.shape
    tm = ROUTER_TM
    vmem = 2 * (tm * d * 4 + d * LANES * 4 + tm * LANES * 4) + 8 * tm * d * 4
    return pl.pallas_call(
        _router_kernel,
        grid=(s // tm,),
        in_specs=[pl.BlockSpec((tm, d), lambda i: (i, 0)), pl.BlockSpec((d, LANES), lambda i: (0, 0))],
        out_specs=pl.BlockSpec((tm, LANES), lambda i: (i, 0)),
        out_shape=jax.ShapeDtypeStruct((s, LANES), F32),
        compiler_params=_params(("arbitrary",), vmem),
        name="router_top2",
    )(x32, w_router_padded)


def _pad_ff(w, axis, mult):
    pad = (-w.shape[axis]) % mult
    if pad == 0:
        return w
    widths = [(0, 0)] * w.ndim
    widths[axis] = (0, pad)
    return jnp.pad(w, widths)


def kernel(x, w_in, b_gate, sg_w, sg_b, sg_ln_g, sg_ln_b, w_branch_a, w_branch_b, w_out,
           ln1_g, ln1_b, ffn_w1, ffn_w3, ffn_w2, moe_router, moe_w1, moe_w3, moe_w2,
           ln2_g, ln2_b):
    b, s, d = x.shape
    assert (b, s, d) == (1, SEQ, D_MODEL)
    x32 = x.reshape(s, d)
    x16 = x32.astype(BF16)
    for layer in range(DEPTH):
        w16 = w_in[layer].astype(BF16)
        qkv = _inproj(x16, w16, 0, 3 * SB_WIDTH, "qkv")
        u = _inproj(x16, w16, OFF_U, SG_WIDTH, "gelu")
        vn = _inproj(x16, w16, OFF_VG, SG_WIDTH, "gelu_ln",
                     (sg_ln_g[layer].reshape(1, -1), sg_ln_b[layer].reshape(1, -1)))
        gates = _inproj(x16, w16, OFF_GATE, 2 * d, "gate", (b_gate[layer].reshape(1, -1),))
        att = _attention(qkv)
        sgu = _sgu(u, vn, sg_w[layer], sg_b[layer].T)
        x32, x16 = _merge(att, sgu, gates, x32,
                          w_branch_a[layer].astype(BF16), w_branch_b[layer].astype(BF16),
                          w_out[layer].astype(BF16),
                          ln1_g[layer].reshape(1, d), ln1_b[layer].reshape(1, d))
        i = layer // 2
        g2, b2 = ln2_g[layer].reshape(1, d), ln2_b[layer].reshape(1, d)
        if layer % 2 == 0:
            w1 = _pad_ff(ffn_w1[i].astype(BF16), 1, FFN_TF)[None]
            w3 = _pad_ff(ffn_w3[i].astype(BF16), 1, FFN_TF)[None]
            w2 = _pad_ff(ffn_w2[i].astype(BF16), 0, FFN_TF)[None]
            x32, x16 = _ffn(x16, x32, w1, w3, w2, g2, b2)
        else:
            wr = jnp.pad(moe_router[i], ((0, 0), (0, LANES - N_EXPERTS)))
            comb = _router(x32, wr)
            x32, x16 = _ffn(x16, x32, moe_w1[i].astype(BF16), moe_w3[i].astype(BF16),
                            moe_w2[i].astype(BF16), g2, b2, comb)
    return x32.reshape(b, s, d)
```

```python
import functools

import jax
import jax.numpy as jnp
from jax import lax
from jax.experimental import pallas as pl
from jax.experimental.pallas import tpu as pltpu

F32 = jnp.float32
BF16 = jnp.bfloat16

D_MODEL = 2048
SEQ = 8192
DEPTH = 2
HEADS = 8
HEAD_DIM = 128
SB_WIDTH = HEADS * HEAD_DIM
GROUPS = 8
GROUP_DIM = 128
SG_WIDTH = GROUPS * GROUP_DIM
CHUNK = 128
OFF_U = 3 * SB_WIDTH
OFF_VG = OFF_U + SG_WIDTH
OFF_GATE = OFF_VG + SG_WIDTH
D_FF_DENSE = 5504
N_EXPERTS = 8
D_FF_EXPERT = 7168
ALPHA = (2.0 * DEPTH) ** 0.25
LN_EPS = 1e-5
SB_UNDERFLOW = 110.0

LANES = 128
V7X_VMEM_BYTES = 64 * 1024 * 1024

PROJ_TM = 1024
PROJ_TN = 1024
ATT_T = 128
ATT_HEADS = 4
SGU_ROWS = 512
MERGE_TM = 256
FFN_TM = 512
FFN_TF = 256
ROUTER_TM = 512
DISPATCH_TM = 512
MOE_TM = 512
MOE_TF = 512
COMBINE_TM = 256


def _params(semantics, vmem_bytes):
    assert vmem_bytes < V7X_VMEM_BYTES
    return pltpu.CompilerParams(dimension_semantics=semantics, vmem_limit_bytes=vmem_bytes)


def _dot(a, b):
    return jnp.dot(a, b, preferred_element_type=F32)


def _layer_norm(y, g, b):
    mu = jnp.mean(y, axis=-1, keepdims=True)
    d = y - mu
    var = jnp.mean(d * d, axis=-1, keepdims=True)
    return d * lax.rsqrt(var + LN_EPS) * g + b


def _gelu_tanh(x):
    return 0.5 * x * (1.0 + jnp.tanh(0.7978845608028654 * (x + 0.044715 * (x * x * x))))


def _inproj_kernel(x_ref, w_ref, *rest, mode):
    acc = _dot(x_ref[...], w_ref[...])
    if mode == "qkv":
        (o_ref,) = rest
        scale = jnp.where(pl.program_id(0) == 0, HEAD_DIM ** -0.5, 1.0).astype(F32)
        o_ref[...] = (acc * scale).astype(o_ref.dtype)
    elif mode == "gelu":
        (o_ref,) = rest
        o_ref[...] = _gelu_tanh(acc).astype(o_ref.dtype)
    elif mode == "gelu_ln":
        g_ref, b_ref, o_ref = rest
        act = _gelu_tanh(acc)
        for grp in range(acc.shape[1] // GROUP_DIM):
            cols = slice(grp * GROUP_DIM, (grp + 1) * GROUP_DIM)
            o_ref[:, cols] = _layer_norm(act[:, cols], g_ref[:, cols], b_ref[:, cols]).astype(o_ref.dtype)
    elif mode == "gate":
        b_ref, o_ref = rest
        o_ref[...] = jax.nn.sigmoid(acc + b_ref[...]).astype(o_ref.dtype)
    else:
        raise ValueError(mode)


def _inproj(x16, w16, col_off, width, mode, extra=()):
    s, d = x16.shape
    tm, tn = PROJ_TM, PROJ_TN
    n_blk = width // tn
    off_blk = col_off // tn
    in_specs = [
        pl.BlockSpec((tm, d), lambda n, m: (m, 0)),
        pl.BlockSpec((d, tn), lambda n, m: (0, off_blk + n)),
    ]
    for _ in extra:
        in_specs.append(pl.BlockSpec((1, tn), lambda n, m: (0, n)))
    vmem = 2 * (tm * d * 2 + d * tn * 2 + tm * tn * 2) + 6 * tm * tn * 4
    return pl.pallas_call(
        functools.partial(_inproj_kernel, mode=mode),
        grid=(n_blk, s // tm),
        in_specs=in_specs,
        out_specs=pl.BlockSpec((tm, tn), lambda n, m: (m, n)),
        out_shape=jax.ShapeDtypeStruct((s, width), BF16),
        compiler_params=_params(("arbitrary", "arbitrary"), vmem),
        name="inproj_" + mode,
    )(x16, w16, *extra)


def _sb_tile(q, k, v, c, later, diag_mask):
    z = lax.dot_general(q, k, (((1,), (1,)), ((), ())), preferred_element_type=F32)
    ls = jnp.minimum(z, 0.0) - jnp.log1p(jnp.exp(-jnp.abs(z)))
    lk = ls - z
    if diag_mask is not None:
        lk = jnp.where(diag_mask, lk, 0.0)
    hi = lk.astype(BF16)
    lo = (lk - hi.astype(F32)).astype(BF16)
    tail = _dot(hi, later) + _dot(lo, later)
    a = jnp.exp(ls + tail + c)
    if diag_mask is not None:
        a = jnp.where(diag_mask, a, 0.0)
    pv = _dot(a.astype(BF16), v)
    return pv, c + tail[:, :1] + lk[:, :1]


def _attn_kernel(q_ref, k_ref, v_ref, o_ref):
    t = ATT_T
    i = pl.program_id(1)
    row = lax.broadcasted_iota(jnp.int32, (t, t), 0)
    col = lax.broadcasted_iota(jnp.int32, (t, t), 1)
    later = jnp.where(row > col, 1.0, 0.0).astype(BF16)
    before = col < row
    heads = [slice(h * HEAD_DIM, (h + 1) * HEAD_DIM) for h in range(ATT_HEADS)]
    qs = [q_ref[:, hs] for hs in heads]

    def key_tile(j, accs, cs, mask):
        start = pl.multiple_of(j * t, t)
        out = [_sb_tile(q, k_ref[pl.ds(start, t), hs], v_ref[pl.ds(start, t), hs], c, later, mask)
               for q, hs, c in zip(qs, heads, cs)]
        return tuple(a + pv for a, (pv, _) in zip(accs, out)), tuple(c for _, c in out)

    def any_weight_left(cs):
        return jnp.max(functools.reduce(jnp.maximum, cs)) > -SB_UNDERFLOW

    zeros = tuple(jnp.zeros((t, HEAD_DIM), F32) for _ in heads)
    accs, cs = key_tile(i, zeros, tuple(jnp.zeros((t, 1), F32) for _ in heads), before)

    def cond(carry):
        j, go, _, _ = carry
        return jnp.logical_and(j >= 0, go)

    def body(carry):
        j, _, accs, cs = carry
        accs, cs = key_tile(j, accs, cs, None)
        return j - 1, any_weight_left(cs), accs, cs

    _, _, accs, _ = lax.while_loop(cond, body, (i - 1, any_weight_left(cs), accs, cs))
    for hs, acc in zip(heads, accs):
        o_ref[:, hs] = acc.astype(o_ref.dtype)


def _attention(qkv):
    s = qkv.shape[0]
    t = ATT_T
    w = ATT_HEADS * HEAD_DIM
    groups = SB_WIDTH // w
    vmem = 2 * (2 * s * w * 2 + 2 * t * w * 2) + ATT_HEADS * 16 * t * t * 4 + (4 << 20)
    return pl.pallas_call(
        _attn_kernel,
        grid=(groups, s // t),
        in_specs=[
            pl.BlockSpec((t, w), lambda g, i: (i, g)),
            pl.BlockSpec((s, w), lambda g, i: (0, groups + g)),
            pl.BlockSpec((s, w), lambda g, i: (0, 2 * groups + g)),
        ],
        out_specs=pl.BlockSpec((t, w), lambda g, i: (i, g)),
        out_shape=jax.ShapeDtypeStruct((s, SB_WIDTH), BF16),
        compiler_params=_params(("arbitrary", "arbitrary"), vmem),
        name="stickbreak_attn",
    )(qkv, qkv, qkv)


def _sgu_kernel(u_ref, v_ref, w_ref, b_ref, o_ref):
    c = CHUNK
    row = lax.broadcasted_iota(jnp.int32, (c, c), 0)
    col = lax.broadcasted_iota(jnp.int32, (c, c), 1)
    causal = col <= row
    for g in range(GROUPS):
        w = jnp.where(causal, w_ref[g], 0.0).astype(BF16)
        b = b_ref[:, g:g + 1]
        cols = slice(g * GROUP_DIM, (g + 1) * GROUP_DIM)
        for cc in range(SGU_ROWS // c):
            rows = slice(cc * c, (cc + 1) * c)
            mixed = _dot(w, v_ref[rows, cols]) + b
            o_ref[rows, cols] = (u_ref[rows, cols].astype(F32) * mixed).astype(o_ref.dtype)


def _sgu(u, vn, sg_w, sg_b_t):
    s = u.shape[0]
    r = SGU_ROWS
    vmem = 2 * 3 * r * SG_WIDTH * 2 + 2 * GROUPS * CHUNK * CHUNK * 4 + (4 << 20)
    return pl.pallas_call(
        _sgu_kernel,
        grid=(s // r,),
        in_specs=[
            pl.BlockSpec((r, SG_WIDTH), lambda i: (i, 0)),
            pl.BlockSpec((r, SG_WIDTH), lambda i: (i, 0)),
            pl.BlockSpec((GROUPS, CHUNK, CHUNK), lambda i: (0, 0, 0)),
            pl.BlockSpec((CHUNK, GROUPS), lambda i: (0, 0)),
        ],
        out_specs=pl.BlockSpec((r, SG_WIDTH), lambda i: (i, 0)),
        out_shape=jax.ShapeDtypeStruct((s, SG_WIDTH), BF16),
        compiler_params=_params(("arbitrary",), vmem),
        name="spatial_gating",
    )(u, vn, sg_w, sg_b_t)


def _merge_kernel(a_ref, b_ref, gate_ref, x_ref, wa_ref, wb_ref, wo_ref, g_ref, beta_ref,
                  o32_ref, o16_ref):
    d = D_MODEL
    ya = _dot(a_ref[...], wa_ref[...])
    yb = _dot(b_ref[...], wb_ref[...])
    merged = gate_ref[:, :d].astype(F32) * ya + gate_ref[:, d:].astype(F32) * yb
    mix = _dot(merged.astype(BF16), wo_ref[...])
    y = _layer_norm(ALPHA * x_ref[...] + mix, g_ref[...], beta_ref[...])
    o32_ref[...] = y
    o16_ref[...] = y.astype(BF16)


def _merge(att, sgu, gates, x32, wa, wb, wo, ln_g, ln_b):
    s, d = x32.shape
    tm = MERGE_TM
    const = lambda i: (0, 0)
    rowblk = lambda i: (i, 0)
    weights = (SB_WIDTH * d + SG_WIDTH * d + d * d) * 2
    vmem = 2 * weights + 2 * tm * (SB_WIDTH * 2 + SG_WIDTH * 2 + 2 * d * 2 + d * 4 + d * 4 + d * 2) \
        + 6 * tm * d * 4
    return pl.pallas_call(
        _merge_kernel,
        grid=(s // tm,),
        in_specs=[
            pl.BlockSpec((tm, SB_WIDTH), rowblk),
            pl.BlockSpec((tm, SG_WIDTH), rowblk),
            pl.BlockSpec((tm, 2 * d), rowblk),
            pl.BlockSpec((tm, d), rowblk),
            pl.BlockSpec((SB_WIDTH, d), const),
            pl.BlockSpec((SG_WIDTH, d), const),
            pl.BlockSpec((d, d), const),
            pl.BlockSpec((1, d), const),
            pl.BlockSpec((1, d), const),
        ],
        out_specs=[pl.BlockSpec((tm, d), rowblk), pl.BlockSpec((tm, d), rowblk)],
        out_shape=[jax.ShapeDtypeStruct((s, d), F32), jax.ShapeDtypeStruct((s, d), BF16)],
        compiler_params=_params(("arbitrary",), vmem),
        name="merge_outproj_ln",
    )(att, sgu, gates, x32, wa, wb, wo, ln_g, ln_b)


def _swiglu_hidden(x, w1_ref, w3_ref):
    h1 = _dot(x, w1_ref[...])
    return (h1 * jax.nn.sigmoid(h1) * _dot(x, w3_ref[...])).astype(BF16)


def _ffn_kernel(x16_ref, x32_ref, w1_ref, w3_ref, w2_ref, g_ref, b_ref, o32_ref, o16_ref, acc_ref):
    f = pl.program_id(1)

    @pl.when(f == 0)
    def _():
        acc_ref[...] = jnp.zeros_like(acc_ref)

    acc_ref[...] += _dot(_swiglu_hidden(x16_ref[...], w1_ref, w3_ref), w2_ref[...])

    @pl.when(f == pl.num_programs(1) - 1)
    def _():
        y = _layer_norm(ALPHA * x32_ref[...] + acc_ref[...], g_ref[...], b_ref[...])
        o32_ref[...] = y
        o16_ref[...] = y.astype(BF16)


def _ffn(x16, x32, w1, w3, w2, ln_g, ln_b):
    s, d = x32.shape
    ff = w1.shape[1]
    tm, tf = FFN_TM, FFN_TF
    rowblk = lambda i, f: (i, 0)
    const = lambda i, f: (0, 0)
    vmem = 2 * (tm * d * (2 + 4 + 4 + 2) + 3 * d * tf * 2) + tm * d * 4 + 4 * tm * tf * 4 + 3 * tm * d * 4
    return pl.pallas_call(
        _ffn_kernel,
        grid=(s // tm, ff // tf),
        in_specs=[
            pl.BlockSpec((tm, d), rowblk),
            pl.BlockSpec((tm, d), rowblk),
            pl.BlockSpec((d, tf), lambda i, f: (0, f)),
            pl.BlockSpec((d, tf), lambda i, f: (0, f)),
            pl.BlockSpec((tf, d), lambda i, f: (f, 0)),
            pl.BlockSpec((1, d), const),
            pl.BlockSpec((1, d), const),
        ],
        out_specs=[pl.BlockSpec((tm, d), rowblk), pl.BlockSpec((tm, d), rowblk)],
        out_shape=[jax.ShapeDtypeStruct((s, d), F32), jax.ShapeDtypeStruct((s, d), BF16)],
        scratch_shapes=[pltpu.VMEM((tm, d), F32)],
        compiler_params=_params(("arbitrary", "arbitrary"), vmem),
        name="swiglu_ln",
    )(x16, x32, w1, w3, w2, ln_g, ln_b)


ROUTE_E, ROUTE_W, ROUTE_RANK = 0, 2, 4


def _router_kernel(x_ref, w_ref, route_ref, counts_ref, seen_ref):
    @pl.when(pl.program_id(0) == 0)
    def _():
        seen_ref[...] = jnp.zeros_like(seen_ref)

    logits = jnp.dot(x_ref[...], w_ref[...], preferred_element_type=F32,
                     precision=lax.Precision.HIGHEST)
    tm = logits.shape[0]
    lane = lax.broadcasted_iota(jnp.int32, logits.shape, 1).astype(F32)
    neg = jnp.float32(-jnp.inf)
    l1 = jnp.where(lane < N_EXPERTS, logits, neg)
    m1 = jnp.max(l1, axis=-1, keepdims=True)
    i1 = jnp.min(jnp.where(l1 == m1, lane, float(LANES)), axis=-1, keepdims=True)
    l2 = jnp.where(lane == i1, neg, l1)
    m2 = jnp.max(l2, axis=-1, keepdims=True)
    i2 = jnp.min(jnp.where(l2 == m2, lane, float(LANES)), axis=-1, keepdims=True)
    e2 = jnp.exp(m2 - m1)
    w_top = 1.0 / (1.0 + e2)

    chosen = jnp.where(lane == i1, 1.0, 0.0) + jnp.where(lane == i2, 1.0, 0.0)
    row = lax.broadcasted_iota(jnp.int32, (tm, tm), 0)
    col = lax.broadcasted_iota(jnp.int32, (tm, tm), 1)
    earlier = jnp.where(col < row, 1.0, 0.0).astype(BF16)
    prefix = _dot(earlier, chosen.astype(BF16)) + seen_ref[...]
    r1 = jnp.sum(jnp.where(lane == i1, prefix, 0.0), axis=-1, keepdims=True)
    r2 = jnp.sum(jnp.where(lane == i2, prefix, 0.0), axis=-1, keepdims=True)
    seen_ref[...] += jnp.sum(chosen, axis=0, keepdims=True)
    counts_ref[...] = seen_ref[...]

    fields = ((ROUTE_E, i1), (ROUTE_E + 1, i2), (ROUTE_W, w_top), (ROUTE_W + 1, e2 * w_top),
              (ROUTE_RANK, r1), (ROUTE_RANK + 1, r2))
    route = jnp.zeros_like(logits)
    for at, val in fields:
        route = jnp.where(lane == at, val, route)
    route_ref[...] = route


def _router(x32, w_router_padded):
    s, d = x32.shape
    tm = ROUTER_TM
    vmem = 2 * (tm * d * 4 + d * LANES * 4 + tm * LANES * 4) + 8 * tm * d * 4
    return pl.pallas_call(
        _router_kernel,
        grid=(s // tm,),
        in_specs=[pl.BlockSpec((tm, d), lambda i: (i, 0)), pl.BlockSpec((d, LANES), lambda i: (0, 0))],
        out_specs=[pl.BlockSpec((tm, LANES), lambda i: (i, 0)), pl.BlockSpec((1, LANES), lambda i: (0, 0))],
        out_shape=[jax.ShapeDtypeStruct((s, LANES), F32), jax.ShapeDtypeStruct((1, LANES), F32)],
        scratch_shapes=[pltpu.VMEM((1, LANES), F32)],
        compiler_params=_params(("arbitrary",), vmem),
        name="router_top2",
    )(x32, w_router_padded)


def _row_copy(src, src_row, dst, dst_row, sem):
    return pltpu.make_async_copy(src.at[pl.ds(src_row, 1)], dst.at[pl.ds(dst_row, 1)], sem)


def _dispatch_kernel(pos_ref, x_hbm, xs_zero_hbm, xs_hbm, sem):
    del xs_zero_hbm
    tm = DISPATCH_TM
    base = pl.program_id(0) * tm

    def start(r, carry):
        for k in range(2):
            _row_copy(x_hbm, base + r, xs_hbm, pos_ref[0, 0, k * tm + r], sem).start()
        return carry

    def wait(r, carry):
        for k in range(2):
            _row_copy(x_hbm, 0, xs_hbm, 0, sem).wait()
        return carry

    lax.fori_loop(0, tm, start, 0)
    lax.fori_loop(0, tm, wait, 0)


def _dispatch(pos_tiles, x32, n_rows):
    s, d = x32.shape
    tm = DISPATCH_TM
    return pl.pallas_call(
        _dispatch_kernel,
        grid=(s // tm,),
        in_specs=[
            pl.BlockSpec((1, 1, 2 * tm), lambda i: (i, 0, 0), memory_space=pltpu.SMEM),
            pl.BlockSpec(memory_space=pl.ANY),
            pl.BlockSpec(memory_space=pl.ANY),
        ],
        out_specs=pl.BlockSpec(memory_space=pl.ANY),
        out_shape=jax.ShapeDtypeStruct((n_rows, d), F32),
        scratch_shapes=[pltpu.SemaphoreType.DMA(())],
        input_output_aliases={2: 0},
        compiler_params=_params(("arbitrary",), 4 << 20),
        name="moe_dispatch",
    )(pos_tiles, x32, jnp.zeros((n_rows, d), F32))


def _experts_kernel(tile_expert_ref, n_used_ref, xs_ref, w1_ref, w3_ref, w2_ref, ys_ref, x16_ref):
    del tile_expert_ref
    f = pl.program_id(1)

    @pl.when(f == 0)
    def _():
        ys_ref[...] = jnp.zeros_like(ys_ref)

    @pl.when(pl.program_id(0) < n_used_ref[0])
    def _():
        @pl.when(f == 0)
        def _():
            x16_ref[...] = xs_ref[...].astype(BF16)

        ys_ref[...] += _dot(_swiglu_hidden(x16_ref[...], w1_ref, w3_ref), w2_ref[...])


def _experts(tile_expert, n_used, xs, w1, w3, w2):
    n_rows, d = xs.shape
    ff = w1.shape[2]
    tm, tf = MOE_TM, MOE_TF
    n_f = ff // tf

    def rows(i, f, te, nu):
        return (jnp.minimum(i, nu[0] - 1), 0)

    def chunk(i, f, nu):
        return jnp.where(i < nu[0], f, n_f - 1)

    vmem = 2 * (2 * tm * d * 4 + 3 * d * tf * 2) + tm * d * 2 + 6 * tm * tf * 4
    return pl.pallas_call(
        _experts_kernel,
        grid_spec=pltpu.PrefetchScalarGridSpec(
            num_scalar_prefetch=2,
            grid=(n_rows // tm, n_f),
            in_specs=[
                pl.BlockSpec((tm, d), rows),
                pl.BlockSpec((None, d, tf), lambda i, f, te, nu: (te[i], 0, chunk(i, f, nu))),
                pl.BlockSpec((None, d, tf), lambda i, f, te, nu: (te[i], 0, chunk(i, f, nu))),
                pl.BlockSpec((None, tf, d), lambda i, f, te, nu: (te[i], chunk(i, f, nu), 0)),
            ],
            out_specs=pl.BlockSpec((tm, d), lambda i, f, te, nu: (i, 0)),
            scratch_shapes=[pltpu.VMEM((tm, d), BF16)],
        ),
        out_shape=jax.ShapeDtypeStruct((n_rows, d), F32),
        compiler_params=_params(("arbitrary", "arbitrary"), vmem),
        name="moe_experts",
    )(tile_expert, n_used, xs, w1, w3, w2)


def _combine_kernel(pos_ref, pos_next_ref, route_ref, x_ref, ys_hbm, g_ref, b_ref, o_ref, buf, sems):
    tm = COMBINE_TM
    i = pl.program_id(0)
    slot = i % 2

    def start_tile(p_ref, slot):
        def start(r, carry):
            for k in range(2):
                _row_copy(ys_hbm, p_ref[0, 0, k * tm + r], buf.at[slot, k], r, sems.at[slot]).start()
            return carry
        lax.fori_loop(0, tm, start, 0)

    @pl.when(i == 0)
    def _():
        start_tile(pos_ref, 0)

    @pl.when(i + 1 < pl.num_programs(0))
    def _():
        start_tile(pos_next_ref, 1 - slot)

    def wait(r, carry):
        for k in range(2):
            _row_copy(ys_hbm, 0, buf.at[slot, k], 0, sems.at[slot]).wait()
        return carry

    lax.fori_loop(0, tm, wait, 0)
    w0 = route_ref[:, ROUTE_W:ROUTE_W + 1]
    w1 = route_ref[:, ROUTE_W + 1:ROUTE_W + 2]
    y = w0 * buf[slot, 0] + w1 * buf[slot, 1]
    o_ref[...] = _layer_norm(ALPHA * x_ref[...] + y, g_ref[...], b_ref[...])


def _combine(pos_tiles, route, x32, ys, ln_g, ln_b):
    s, d = x32.shape
    tm = COMBINE_TM
    n = s // tm
    rowblk = lambda i: (i, 0)
    const = lambda i: (0, 0)
    vmem = 2 * 2 * tm * d * 4 + 2 * (2 * tm * d * 4 + tm * LANES * 4) + 6 * tm * d * 4
    return pl.pallas_call(
        _combine_kernel,
        grid=(n,),
        in_specs=[
            pl.BlockSpec((1, 1, 2 * tm), lambda i: (i, 0, 0), memory_space=pltpu.SMEM),
            pl.BlockSpec((1, 1, 2 * tm), lambda i: (jnp.minimum(i + 1, n - 1), 0, 0),
                         memory_space=pltpu.SMEM),
            pl.BlockSpec((tm, LANES), rowblk),
            pl.BlockSpec((tm, d), rowblk),
            pl.BlockSpec(memory_space=pl.ANY),
            pl.BlockSpec((1, d), const),
            pl.BlockSpec((1, d), const),
        ],
        out_specs=pl.BlockSpec((tm, d), rowblk),
        out_shape=jax.ShapeDtypeStruct((s, d), F32),
        scratch_shapes=[pltpu.VMEM((2, 2, tm, d), F32), pltpu.SemaphoreType.DMA((2,))],
        compiler_params=_params(("arbitrary",), vmem),
        name="moe_combine_ln",
    )(pos_tiles, pos_tiles, route, x32, ys, ln_g, ln_b)


def _pos_tiles(pos, tm):
    s = pos.shape[0]
    return pos.reshape(s // tm, tm, 2).transpose(0, 2, 1).reshape(s // tm, 1, 2 * tm)


def _moe(x32, w_router, w1, w3, w2, ln_g, ln_b):
    s, d = x32.shape
    tm = MOE_TM
    n_tiles = (2 * s) // tm + N_EXPERTS
    route, counts = _router(x32, jnp.pad(w_router, ((0, 0), (0, LANES - N_EXPERTS))))

    counts = counts[0, :N_EXPERTS].astype(jnp.int32)
    tiles_per_expert = (counts + tm - 1) // tm
    tile_end = jnp.cumsum(tiles_per_expert)
    row_start = (tile_end - tiles_per_expert) * tm
    n_used = tile_end[-1:]
    tile_ids = jnp.minimum(jnp.arange(n_tiles, dtype=jnp.int32), n_used - 1)
    tile_expert = jnp.sum(tile_ids[:, None] >= tile_end[None, :], axis=1).astype(jnp.int32)
    experts = route[:, ROUTE_E:ROUTE_E + 2].astype(jnp.int32)
    pos = row_start[experts] + route[:, ROUTE_RANK:ROUTE_RANK + 2].astype(jnp.int32)

    xs = _dispatch(_pos_tiles(pos, DISPATCH_TM), x32, n_tiles * tm)
    ys = _experts(tile_expert, n_used, xs, w1, w3, w2)
    return _combine(_pos_tiles(pos, COMBINE_TM), route, x32, ys, ln_g, ln_b)


def _pad_ff(w, axis, mult):
    pad = (-w.shape[axis]) % mult
    if pad == 0:
        return w
    widths = [(0, 0)] * w.ndim
    widths[axis] = (0, pad)
    return jnp.pad(w, widths)


def kernel(x, w_in, b_gate, sg_w, sg_b, sg_ln_g, sg_ln_b, w_branch_a, w_branch_b, w_out,
           ln1_g, ln1_b, ffn_w1, ffn_w3, ffn_w2, moe_router, moe_w1, moe_w3, moe_w2,
           ln2_g, ln2_b):
    b, s, d = x.shape
    assert (b, s, d) == (1, SEQ, D_MODEL)
    x32 = x.reshape(s, d)
    x16 = x32.astype(BF16)
    for layer in range(DEPTH):
        w16 = w_in[layer].astype(BF16)
        qkv = _inproj(x16, w16, 0, 3 * SB_WIDTH, "qkv")
        u = _inproj(x16, w16, OFF_U, SG_WIDTH, "gelu")
        vn = _inproj(x16, w16, OFF_VG, SG_WIDTH, "gelu_ln",
                     (sg_ln_g[layer].reshape(1, -1), sg_ln_b[layer].reshape(1, -1)))
        gates = _inproj(x16, w16, OFF_GATE, 2 * d, "gate", (b_gate[layer].reshape(1, -1),))
        att = _attention(qkv)
        sgu = _sgu(u, vn, sg_w[layer], sg_b[layer].T)
        x32, x16 = _merge(att, sgu, gates, x32,
                          w_branch_a[layer].astype(BF16), w_branch_b[layer].astype(BF16),
                          w_out[layer].astype(BF16),
                          ln1_g[layer].reshape(1, d), ln1_b[layer].reshape(1, d))
        i = layer // 2
        g2, b2 = ln2_g[layer].reshape(1, d), ln2_b[layer].reshape(1, d)
        if layer % 2 == 0:
            w1 = _pad_ff(ffn_w1[i].astype(BF16), 1, FFN_TF)
            w3 = _pad_ff(ffn_w3[i].astype(BF16), 1, FFN_TF)
            w2 = _pad_ff(ffn_w2[i].astype(BF16), 0, FFN_TF)
            x32, x16 = _ffn(x16, x32, w1, w3, w2, g2, b2)
        else:
            x32 = _moe(x32, moe_router[i], moe_w1[i].astype(BF16), moe_w3[i].astype(BF16),
                       moe_w2[i].astype(BF16), g2, b2)
            x16 = x32.astype(BF16) if layer + 1 < DEPTH else None
    return x32.reshape(b, s, d)
```

```python
import functools

import jax
import jax.numpy as jnp
from jax import lax
from jax.experimental import pallas as pl
from jax.experimental.pallas import tpu as pltpu

F32 = jnp.float32
BF16 = jnp.bfloat16

D_MODEL = 2048
SEQ = 8192
DEPTH = 2
HEADS = 8
HEAD_DIM = 128
SB_WIDTH = HEADS * HEAD_DIM
GROUPS = 8
GROUP_DIM = 128
SG_WIDTH = GROUPS * GROUP_DIM
CHUNK = 128
OFF_U = 3 * SB_WIDTH
OFF_VG = OFF_U + SG_WIDTH
OFF_GATE = OFF_VG + SG_WIDTH
D_FF_DENSE = 5504
N_EXPERTS = 8
D_FF_EXPERT = 7168
ALPHA = (2.0 * DEPTH) ** 0.25
LN_EPS = 1e-5
SB_UNDERFLOW = 110.0

LANES = 128
V7X_VMEM_BYTES = 64 * 1024 * 1024

PROJ_TM = 1024
PROJ_TN = 1024
ATT_T = 128
ATT_HEADS = 8
SGU_ROWS = 512
MERGE_TM = 256
FFN_TM = 512
FFN_TF = 256
ROUTER_TM = 512
DISPATCH_TM = 512
MOE_TM = 512
MOE_TF = 512
COMBINE_TM = 256


def _params(semantics, vmem_bytes):
    assert vmem_bytes < V7X_VMEM_BYTES
    return pltpu.CompilerParams(dimension_semantics=semantics, vmem_limit_bytes=vmem_bytes)


def _dot(a, b):
    return jnp.dot(a, b, preferred_element_type=F32)


def _layer_norm(y, g, b):
    mu = jnp.mean(y, axis=-1, keepdims=True)
    d = y - mu
    var = jnp.mean(d * d, axis=-1, keepdims=True)
    return d * lax.rsqrt(var + LN_EPS) * g + b


def _gelu_tanh(x):
    return 0.5 * x * (1.0 + jnp.tanh(0.7978845608028654 * (x + 0.044715 * (x * x * x))))


def _inproj_kernel(x_ref, w_ref, *rest, mode):
    acc = _dot(x_ref[...], w_ref[...])
    if mode == "qkv":
        (o_ref,) = rest
        scale = jnp.where(pl.program_id(0) == 0, HEAD_DIM ** -0.5, 1.0).astype(F32)
        o_ref[...] = (acc * scale).astype(o_ref.dtype)
    elif mode == "gelu":
        (o_ref,) = rest
        o_ref[...] = _gelu_tanh(acc).astype(o_ref.dtype)
    elif mode == "gelu_ln":
        g_ref, b_ref, o_ref = rest
        act = _gelu_tanh(acc)
        for grp in range(acc.shape[1] // GROUP_DIM):
            cols = slice(grp * GROUP_DIM, (grp + 1) * GROUP_DIM)
            o_ref[:, cols] = _layer_norm(act[:, cols], g_ref[:, cols], b_ref[:, cols]).astype(o_ref.dtype)
    elif mode == "gate":
        b_ref, o_ref = rest
        o_ref[...] = jax.nn.sigmoid(acc + b_ref[...]).astype(o_ref.dtype)
    else:
        raise ValueError(mode)


def _inproj(x16, w16, col_off, width, mode, extra=()):
    s, d = x16.shape
    tm, tn = PROJ_TM, PROJ_TN
    n_blk = width // tn
    off_blk = col_off // tn
    in_specs = [
        pl.BlockSpec((tm, d), lambda n, m: (m, 0)),
        pl.BlockSpec((d, tn), lambda n, m: (0, off_blk + n)),
    ]
    for _ in extra:
        in_specs.append(pl.BlockSpec((1, tn), lambda n, m: (0, n)))
    vmem = 2 * (tm * d * 2 + d * tn * 2 + tm * tn * 2) + 6 * tm * tn * 4
    return pl.pallas_call(
        functools.partial(_inproj_kernel, mode=mode),
        grid=(n_blk, s // tm),
        in_specs=in_specs,
        out_specs=pl.BlockSpec((tm, tn), lambda n, m: (m, n)),
        out_shape=jax.ShapeDtypeStruct((s, width), BF16),
        compiler_params=_params(("arbitrary", "arbitrary"), vmem),
        name="inproj_" + mode,
    )(x16, w16, *extra)


def _attn_kernel(q_ref, k_ref, v_ref, o_ref, z_ref, ls_ref, hl_ref, tail_ref, a_ref, acc_ref, c_ref):
    t = ATT_T
    i = pl.program_id(1)
    row = lax.broadcasted_iota(jnp.int32, (t, t), 0)
    col = lax.broadcasted_iota(jnp.int32, (t, t), 1)
    later = jnp.where(row > col, 1.0, 0.0).astype(BF16)
    later2 = jnp.concatenate([later, later], axis=0)
    before = col < row
    heads = [slice(h * HEAD_DIM, (h + 1) * HEAD_DIM) for h in range(ATT_HEADS)]

    def key_tile(j, mask):
        keys = pl.ds(pl.multiple_of(j * t, t), t)
        for h, hs in enumerate(heads):
            z_ref[h] = lax.dot_general(q_ref[:, hs], k_ref[keys, hs], (((1,), (1,)), ((), ())),
                                       preferred_element_type=F32)
        for h in range(ATT_HEADS):
            z = z_ref[h]
            ls = jnp.minimum(z, 0.0) - jnp.log1p(jnp.exp(-jnp.abs(z)))
            lk = ls - z
            if mask is not None:
                lk = jnp.where(mask, lk, 0.0)
            hi = lk.astype(BF16)
            ls_ref[h] = ls
            hl_ref[h, :, :t] = hi
            hl_ref[h, :, t:] = (lk - hi.astype(F32)).astype(BF16)
        for h in range(ATT_HEADS):
            tail_ref[h] = _dot(hl_ref[h], later2)
        for h, hs in enumerate(heads):
            tail = tail_ref[h]
            a = jnp.exp(ls_ref[h] + tail + c_ref[:, hs])
            lk0 = ls_ref[h][:, :1] - z_ref[h][:, :1]
            if mask is not None:
                a = jnp.where(mask, a, 0.0)
                lk0 = jnp.where(mask[:, :1], lk0, 0.0)
            a_ref[h] = a.astype(BF16)
            c_ref[:, hs] += jnp.broadcast_to(tail[:, :1] + lk0, (t, HEAD_DIM))
        for h, hs in enumerate(heads):
            acc_ref[:, hs] += _dot(a_ref[h], v_ref[keys, hs])

    def any_weight_left():
        return jnp.max(c_ref[...]) > -SB_UNDERFLOW

    acc_ref[...] = jnp.zeros_like(acc_ref)
    c_ref[...] = jnp.zeros_like(c_ref)
    key_tile(i, before)

    def cond(carry):
        j, go = carry
        return jnp.logical_and(j >= 0, go)

    def body(carry):
        j, _ = carry
        key_tile(j, None)
        return j - 1, any_weight_left()

    lax.while_loop(cond, body, (i - 1, any_weight_left()))
    o_ref[...] = acc_ref[...].astype(o_ref.dtype)


def _attention(qkv):
    s = qkv.shape[0]
    t = ATT_T
    nh = ATT_HEADS
    w = nh * HEAD_DIM
    groups = SB_WIDTH // w
    resident = pl.Buffered(1)
    vmem = 2 * s * w * 2 + 2 * (2 * t * w * 2) + nh * 16 * t * t + 2 * t * w * 4 + (8 << 20)
    return pl.pallas_call(
        _attn_kernel,
        grid=(groups, s // t),
        in_specs=[
            pl.BlockSpec((t, w), lambda g, i: (i, g)),
            pl.BlockSpec((s, w), lambda g, i: (0, groups + g), pipeline_mode=resident),
            pl.BlockSpec((s, w), lambda g, i: (0, 2 * groups + g), pipeline_mode=resident),
        ],
        out_specs=pl.BlockSpec((t, w), lambda g, i: (i, g)),
        out_shape=jax.ShapeDtypeStruct((s, SB_WIDTH), BF16),
        scratch_shapes=[
            pltpu.VMEM((nh, t, t), F32),
            pltpu.VMEM((nh, t, t), F32),
            pltpu.VMEM((nh, t, 2 * t), BF16),
            pltpu.VMEM((nh, t, t), F32),
            pltpu.VMEM((nh, t, t), BF16),
            pltpu.VMEM((t, w), F32),
            pltpu.VMEM((t, w), F32),
        ],
        compiler_params=_params(("arbitrary", "arbitrary"), vmem),
        name="stickbreak_attn",
    )(qkv, qkv, qkv)


def _sgu_kernel(u_ref, v_ref, w_ref, b_ref, o_ref):
    c = CHUNK
    row = lax.broadcasted_iota(jnp.int32, (c, c), 0)
    col = lax.broadcasted_iota(jnp.int32, (c, c), 1)
    causal = col <= row
    for g in range(GROUPS):
        w = jnp.where(causal, w_ref[g], 0.0).astype(BF16)
        b = b_ref[:, g:g + 1]
        cols = slice(g * GROUP_DIM, (g + 1) * GROUP_DIM)
        for cc in range(SGU_ROWS // c):
            rows = slice(cc * c, (cc + 1) * c)
            mixed = _dot(w, v_ref[rows, cols]) + b
            o_ref[rows, cols] = (u_ref[rows, cols].astype(F32) * mixed).astype(o_ref.dtype)


def _sgu(u, vn, sg_w, sg_b_t):
    s = u.shape[0]
    r = SGU_ROWS
    vmem = 2 * 3 * r * SG_WIDTH * 2 + 2 * GROUPS * CHUNK * CHUNK * 4 + (4 << 20)
    return pl.pallas_call(
        _sgu_kernel,
        grid=(s // r,),
        in_specs=[
            pl.BlockSpec((r, SG_WIDTH), lambda i: (i, 0)),
            pl.BlockSpec((r, SG_WIDTH), lambda i: (i, 0)),
            pl.BlockSpec((GROUPS, CHUNK, CHUNK), lambda i: (0, 0, 0)),
            pl.BlockSpec((CHUNK, GROUPS), lambda i: (0, 0)),
        ],
        out_specs=pl.BlockSpec((r, SG_WIDTH), lambda i: (i, 0)),
        out_shape=jax.ShapeDtypeStruct((s, SG_WIDTH), BF16),
        compiler_params=_params(("arbitrary",), vmem),
        name="spatial_gating",
    )(u, vn, sg_w, sg_b_t)


def _merge_kernel(a_ref, b_ref, gate_ref, x_ref, wa_ref, wb_ref, wo_ref, g_ref, beta_ref,
                  o32_ref, o16_ref):
    d = D_MODEL
    ya = _dot(a_ref[...], wa_ref[...])
    yb = _dot(b_ref[...], wb_ref[...])
    merged = gate_ref[:, :d].astype(F32) * ya + gate_ref[:, d:].astype(F32) * yb
    mix = _dot(merged.astype(BF16), wo_ref[...])
    y = _layer_norm(ALPHA * x_ref[...] + mix, g_ref[...], beta_ref[...])
    o32_ref[...] = y
    o16_ref[...] = y.astype(BF16)


def _merge(att, sgu, gates, x32, wa, wb, wo, ln_g, ln_b):
    s, d = x32.shape
    tm = MERGE_TM
    const = lambda i: (0, 0)
    rowblk = lambda i: (i, 0)
    weights = (SB_WIDTH * d + SG_WIDTH * d + d * d) * 2
    vmem = 2 * weights + 2 * tm * (SB_WIDTH * 2 + SG_WIDTH * 2 + 2 * d * 2 + d * 4 + d * 4 + d * 2) \
        + 6 * tm * d * 4
    return pl.pallas_call(
        _merge_kernel,
        grid=(s // tm,),
        in_specs=[
            pl.BlockSpec((tm, SB_WIDTH), rowblk),
            pl.BlockSpec((tm, SG_WIDTH), rowblk),
            pl.BlockSpec((tm, 2 * d), rowblk),
            pl.BlockSpec((tm, d), rowblk),
            pl.BlockSpec((SB_WIDTH, d), const),
            pl.BlockSpec((SG_WIDTH, d), const),
            pl.BlockSpec((d, d), const),
            pl.BlockSpec((1, d), const),
            pl.BlockSpec((1, d), const),
        ],
        out_specs=[pl.BlockSpec((tm, d), rowblk), pl.BlockSpec((tm, d), rowblk)],
        out_shape=[jax.ShapeDtypeStruct((s, d), F32), jax.ShapeDtypeStruct((s, d), BF16)],
        compiler_params=_params(("arbitrary",), vmem),
        name="merge_outproj_ln",
    )(att, sgu, gates, x32, wa, wb, wo, ln_g, ln_b)


def _swiglu_hidden(x, w1_ref, w3_ref):
    h1 = _dot(x, w1_ref[...])
    return (h1 * jax.nn.sigmoid(h1) * _dot(x, w3_ref[...])).astype(BF16)


def _ffn_kernel(x16_ref, x32_ref, w1_ref, w3_ref, w2_ref, g_ref, b_ref, o32_ref, o16_ref, acc_ref):
    f = pl.program_id(1)

    @pl.when(f == 0)
    def _():
        acc_ref[...] = jnp.zeros_like(acc_ref)

    acc_ref[...] += _dot(_swiglu_hidden(x16_ref[...], w1_ref, w3_ref), w2_ref[...])

    @pl.when(f == pl.num_programs(1) - 1)
    def _():
        y = _layer_norm(ALPHA * x32_ref[...] + acc_ref[...], g_ref[...], b_ref[...])
        o32_ref[...] = y
        o16_ref[...] = y.astype(BF16)


def _ffn(x16, x32, w1, w3, w2, ln_g, ln_b):
    s, d = x32.shape
    ff = w1.shape[1]
    tm, tf = FFN_TM, FFN_TF
    rowblk = lambda i, f: (i, 0)
    const = lambda i, f: (0, 0)
    vmem = 2 * (tm * d * (2 + 4 + 4 + 2) + 3 * d * tf * 2) + tm * d * 4 + 4 * tm * tf * 4 + 3 * tm * d * 4
    return pl.pallas_call(
        _ffn_kernel,
        grid=(s // tm, ff // tf),
        in_specs=[
            pl.BlockSpec((tm, d), rowblk),
            pl.BlockSpec((tm, d), rowblk),
            pl.BlockSpec((d, tf), lambda i, f: (0, f)),
            pl.BlockSpec((d, tf), lambda i, f: (0, f)),
            pl.BlockSpec((tf, d), lambda i, f: (f, 0)),
            pl.BlockSpec((1, d), const),
            pl.BlockSpec((1, d), const),
        ],
        out_specs=[pl.BlockSpec((tm, d), rowblk), pl.BlockSpec((tm, d), rowblk)],
        out_shape=[jax.ShapeDtypeStruct((s, d), F32), jax.ShapeDtypeStruct((s, d), BF16)],
        scratch_shapes=[pltpu.VMEM((tm, d), F32)],
        compiler_params=_params(("arbitrary", "arbitrary"), vmem),
        name="swiglu_ln",
    )(x16, x32, w1, w3, w2, ln_g, ln_b)


ROUTE_E, ROUTE_W, ROUTE_RANK = 0, 2, 4


def _router_kernel(x_ref, w_ref, route_ref, counts_ref, seen_ref):
    @pl.when(pl.program_id(0) == 0)
    def _():
        seen_ref[...] = jnp.zeros_like(seen_ref)

    logits = jnp.dot(x_ref[...], w_ref[...], preferred_element_type=F32,
                     precision=lax.Precision.HIGHEST)
    tm = logits.shape[0]
    lane = lax.broadcasted_iota(jnp.int32, logits.shape, 1).astype(F32)
    neg = jnp.float32(-jnp.inf)
    l1 = jnp.where(lane < N_EXPERTS, logits, neg)
    m1 = jnp.max(l1, axis=-1, keepdims=True)
    i1 = jnp.min(jnp.where(l1 == m1, lane, float(LANES)), axis=-1, keepdims=True)
    l2 = jnp.where(lane == i1, neg, l1)
    m2 = jnp.max(l2, axis=-1, keepdims=True)
    i2 = jnp.min(jnp.where(l2 == m2, lane, float(LANES)), axis=-1, keepdims=True)
    e2 = jnp.exp(m2 - m1)
    w_top = 1.0 / (1.0 + e2)

    chosen = jnp.where(lane == i1, 1.0, 0.0) + jnp.where(lane == i2, 1.0, 0.0)
    row = lax.broadcasted_iota(jnp.int32, (tm, tm), 0)
    col = lax.broadcasted_iota(jnp.int32, (tm, tm), 1)
    earlier = jnp.where(col < row, 1.0, 0.0).astype(BF16)
    prefix = _dot(earlier, chosen.astype(BF16)) + seen_ref[...]
    r1 = jnp.sum(jnp.where(lane == i1, prefix, 0.0), axis=-1, keepdims=True)
    r2 = jnp.sum(jnp.where(lane == i2, prefix, 0.0), axis=-1, keepdims=True)
    seen_ref[...] += jnp.sum(chosen, axis=0, keepdims=True)
    counts_ref[...] = seen_ref[...]

    fields = ((ROUTE_E, i1), (ROUTE_E + 1, i2), (ROUTE_W, w_top), (ROUTE_W + 1, e2 * w_top),
              (ROUTE_RANK, r1), (ROUTE_RANK + 1, r2))
    route = jnp.zeros_like(logits)
    for at, val in fields:
        route = jnp.where(lane == at, val, route)
    route_ref[...] = route


def _router(x32, w_router_padded):
    s, d = x32.shape
    tm = ROUTER_TM
    vmem = 2 * (tm * d * 4 + d * LANES * 4 + tm * LANES * 4) + 8 * tm * d * 4
    return pl.pallas_call(
        _router_kernel,
        grid=(s // tm,),
        in_specs=[pl.BlockSpec((tm, d), lambda i: (i, 0)), pl.BlockSpec((d, LANES), lambda i: (0, 0))],
        out_specs=[pl.BlockSpec((tm, LANES), lambda i: (i, 0)), pl.BlockSpec((1, LANES), lambda i: (0, 0))],
        out_shape=[jax.ShapeDtypeStruct((s, LANES), F32), jax.ShapeDtypeStruct((1, LANES), F32)],
        scratch_shapes=[pltpu.VMEM((1, LANES), F32)],
        compiler_params=_params(("arbitrary",), vmem),
        name="router_top2",
    )(x32, w_router_padded)


def _row_copy(src, src_row, dst, dst_row, sem):
    return pltpu.make_async_copy(src.at[pl.ds(src_row, 1)], dst.at[pl.ds(dst_row, 1)], sem)


def _dispatch_kernel(pos_ref, x_ref, xs_zero_hbm, xs_hbm, sem):
    del xs_zero_hbm
    tm = DISPATCH_TM

    def start(r, carry):
        for k in range(2):
            _row_copy(x_ref, r, xs_hbm, pos_ref[0, 0, k * tm + r], sem).start()
        return carry

    def wait(r, carry):
        for k in range(2):
            _row_copy(x_ref, 0, xs_hbm, 0, sem).wait()
        return carry

    lax.fori_loop(0, tm, start, 0)
    lax.fori_loop(0, tm, wait, 0)


def _dispatch(pos_tiles, x32, n_rows):
    s, d = x32.shape
    tm = DISPATCH_TM
    return pl.pallas_call(
        _dispatch_kernel,
        grid=(s // tm,),
        in_specs=[
            pl.BlockSpec((1, 1, 2 * tm), lambda i: (i, 0, 0), memory_space=pltpu.SMEM),
            pl.BlockSpec((tm, d), lambda i: (i, 0)),
            pl.BlockSpec(memory_space=pl.ANY),
        ],
        out_specs=pl.BlockSpec(memory_space=pl.ANY),
        out_shape=jax.ShapeDtypeStruct((n_rows, d), F32),
        scratch_shapes=[pltpu.SemaphoreType.DMA(())],
        input_output_aliases={2: 0},
        compiler_params=_params(("arbitrary",), 2 * tm * d * 4 + (4 << 20)),
        name="moe_dispatch",
    )(pos_tiles, x32, jnp.zeros((n_rows, d), F32))


def _experts_kernel(tile_expert_ref, n_used_ref, xs_ref, w1_ref, w3_ref, w2_ref, ys_ref, x16_ref):
    del tile_expert_ref
    f = pl.program_id(1)

    @pl.when(f == 0)
    def _():
        ys_ref[...] = jnp.zeros_like(ys_ref)

    @pl.when(pl.program_id(0) < n_used_ref[0])
    def _():
        @pl.when(f == 0)
        def _():
            x16_ref[...] = xs_ref[...].astype(BF16)

        ys_ref[...] += _dot(_swiglu_hidden(x16_ref[...], w1_ref, w3_ref), w2_ref[...])


def _experts(tile_expert, n_used, xs, w1, w3, w2):
    n_rows, d = xs.shape
    ff = w1.shape[2]
    tm, tf = MOE_TM, MOE_TF
    n_f = ff // tf

    def rows(i, f, te, nu):
        return (jnp.minimum(i, nu[0] - 1), 0)

    def chunk(i, f, nu):
        return jnp.where(i < nu[0], f, n_f - 1)

    vmem = 2 * (2 * tm * d * 4 + 3 * d * tf * 2) + tm * d * 2 + 6 * tm * tf * 4
    return pl.pallas_call(
        _experts_kernel,
        grid_spec=pltpu.PrefetchScalarGridSpec(
            num_scalar_prefetch=2,
            grid=(n_rows // tm, n_f),
            in_specs=[
                pl.BlockSpec((tm, d), rows),
                pl.BlockSpec((None, d, tf), lambda i, f, te, nu: (te[i], 0, chunk(i, f, nu))),
                pl.BlockSpec((None, d, tf), lambda i, f, te, nu: (te[i], 0, chunk(i, f, nu))),
                pl.BlockSpec((None, tf, d), lambda i, f, te, nu: (te[i], chunk(i, f, nu), 0)),
            ],
            out_specs=pl.BlockSpec((tm, d), lambda i, f, te, nu: (i, 0)),
            scratch_shapes=[pltpu.VMEM((tm, d), BF16)],
        ),
        out_shape=jax.ShapeDtypeStruct((n_rows, d), F32),
        compiler_params=_params(("arbitrary", "arbitrary"), vmem),
        name="moe_experts",
    )(tile_expert, n_used, xs, w1, w3, w2)


def _combine_kernel(pos_ref, pos_next_ref, route_ref, x_ref, ys_hbm, g_ref, b_ref, o_ref, buf, sems):
    tm = COMBINE_TM
    i = pl.program_id(0)
    slot = i % 2

    def start_tile(p_ref, slot):
        def start(r, carry):
            for k in range(2):
                _row_copy(ys_hbm, p_ref[0, 0, k * tm + r], buf.at[slot, k], r, sems.at[slot]).start()
            return carry
        lax.fori_loop(0, tm, start, 0)

    @pl.when(i == 0)
    def _():
        start_tile(pos_ref, 0)

    @pl.when(i + 1 < pl.num_programs(0))
    def _():
        start_tile(pos_next_ref, 1 - slot)

    def wait(r, carry):
        for k in range(2):
            _row_copy(ys_hbm, 0, buf.at[slot, k], 0, sems.at[slot]).wait()
        return carry

    lax.fori_loop(0, tm, wait, 0)
    w0 = route_ref[:, ROUTE_W:ROUTE_W + 1]
    w1 = route_ref[:, ROUTE_W + 1:ROUTE_W + 2]
    y = w0 * buf[slot, 0] + w1 * buf[slot, 1]
    o_ref[...] = _layer_norm(ALPHA * x_ref[...] + y, g_ref[...], b_ref[...])


def _combine(pos_tiles, route, x32, ys, ln_g, ln_b):
    s, d = x32.shape
    tm = COMBINE_TM
    n = s // tm
    rowblk = lambda i: (i, 0)
    const = lambda i: (0, 0)
    vmem = 2 * 2 * tm * d * 4 + 2 * (2 * tm * d * 4 + tm * LANES * 4) + 6 * tm * d * 4
    return pl.pallas_call(
        _combine_kernel,
        grid=(n,),
        in_specs=[
            pl.BlockSpec((1, 1, 2 * tm), lambda i: (i, 0, 0), memory_space=pltpu.SMEM),
            pl.BlockSpec((1, 1, 2 * tm), lambda i: (jnp.minimum(i + 1, n - 1), 0, 0),
                         memory_space=pltpu.SMEM),
            pl.BlockSpec((tm, LANES), rowblk),
            pl.BlockSpec((tm, d), rowblk),
            pl.BlockSpec(memory_space=pl.ANY),
            pl.BlockSpec((1, d), const),
            pl.BlockSpec((1, d), const),
        ],
        out_specs=pl.BlockSpec((tm, d), rowblk),
        out_shape=jax.ShapeDtypeStruct((s, d), F32),
        scratch_shapes=[pltpu.VMEM((2, 2, tm, d), F32), pltpu.SemaphoreType.DMA((2,))],
        compiler_params=_params(("arbitrary",), vmem),
        name="moe_combine_ln",
    )(pos_tiles, pos_tiles, route, x32, ys, ln_g, ln_b)


def _pos_tiles(pos, tm):
    s = pos.shape[0]
    return pos.reshape(s // tm, tm, 2).transpose(0, 2, 1).reshape(s // tm, 1, 2 * tm)


def _moe(x32, w_router, w1, w3, w2, ln_g, ln_b):
    s, d = x32.shape
    tm = MOE_TM
    n_tiles = (2 * s) // tm + N_EXPERTS
    route, counts = _router(x32, jnp.pad(w_router, ((0, 0), (0, LANES - N_EXPERTS))))

    counts = counts[0, :N_EXPERTS].astype(jnp.int32)
    tiles_per_expert = (counts + tm - 1) // tm
    tile_end = jnp.cumsum(tiles_per_expert)
    row_start = (tile_end - tiles_per_expert) * tm
    n_used = tile_end[-1:]
    tile_ids = jnp.minimum(jnp.arange(n_tiles, dtype=jnp.int32), n_used - 1)
    tile_expert = jnp.sum(tile_ids[:, None] >= tile_end[None, :], axis=1).astype(jnp.int32)
    experts = route[:, ROUTE_E:ROUTE_E + 2].astype(jnp.int32)
    pos = row_start[experts] + route[:, ROUTE_RANK:ROUTE_RANK + 2].astype(jnp.int32)

    xs = _dispatch(_pos_tiles(pos, DISPATCH_TM), x32, n_tiles * tm)
    ys = _experts(tile_expert, n_used, xs, w1, w3, w2)
    return _combine(_pos_tiles(pos, COMBINE_TM), route, x32, ys, ln_g, ln_b)


def _pad_ff(w, axis, mult):
    pad = (-w.shape[axis]) % mult
    if pad == 0:
        return w
    widths = [(0, 0)] * w.ndim
    widths[axis] = (0, pad)
    return jnp.pad(w, widths)


def kernel(x, w_in, b_gate, sg_w, sg_b, sg_ln_g, sg_ln_b, w_branch_a, w_branch_b, w_out,
           ln1_g, ln1_b, ffn_w1, ffn_w3, ffn_w2, moe_router, moe_w1, moe_w3, moe_w2,
           ln2_g, ln2_b):
    b, s, d = x.shape
    assert (b, s, d) == (1, SEQ, D_MODEL)
    x32 = x.reshape(s, d)
    x16 = x32.astype(BF16)
    for layer in range(DEPTH):
        w16 = w_in[layer].astype(BF16)
        qkv = _inproj(x16, w16, 0, 3 * SB_WIDTH, "qkv")
        u = _inproj(x16, w16, OFF_U, SG_WIDTH, "gelu")
        vn = _inproj(x16, w16, OFF_VG, SG_WIDTH, "gelu_ln",
                     (sg_ln_g[layer].reshape(1, -1), sg_ln_b[layer].reshape(1, -1)))
        gates = _inproj(x16, w16, OFF_GATE, 2 * d, "gate", (b_gate[layer].reshape(1, -1),))
        att = _attention(qkv)
        sgu = _sgu(u, vn, sg_w[layer], sg_b[layer].T)
        x32, x16 = _merge(att, sgu, gates, x32,
                          w_branch_a[layer].astype(BF16), w_branch_b[layer].astype(BF16),
                          w_out[layer].astype(BF16),
                          ln1_g[layer].reshape(1, d), ln1_b[layer].reshape(1, d))
        i = layer // 2
        g2, b2 = ln2_g[layer].reshape(1, d), ln2_b[layer].reshape(1, d)
        if layer % 2 == 0:
            w1 = _pad_ff(ffn_w1[i].astype(BF16), 1, FFN_TF)
            w3 = _pad_ff(ffn_w3[i].astype(BF16), 1, FFN_TF)
            w2 = _pad_ff(ffn_w2[i].astype(BF16), 0, FFN_TF)
            x32, x16 = _ffn(x16, x32, w1, w3, w2, g2, b2)
        else:
            x32 = _moe(x32, moe_router[i], moe_w1[i].astype(BF16), moe_w3[i].astype(BF16),
                       moe_w2[i].astype(BF16), g2, b2)
            x16 = x32.astype(BF16) if layer + 1 < DEPTH else None
    return x32.reshape(b, s, d)
```

```python
import functools

import jax
import jax.numpy as jnp
from jax import lax
from jax.experimental import pallas as pl
from jax.experimental.pallas import tpu as pltpu

F32 = jnp.float32
BF16 = jnp.bfloat16

D_MODEL = 2048
SEQ = 8192
DEPTH = 2
HEADS = 8
HEAD_DIM = 128
SB_WIDTH = HEADS * HEAD_DIM
GROUPS = 8
GROUP_DIM = 128
SG_WIDTH = GROUPS * GROUP_DIM
CHUNK = 128
OFF_U = 3 * SB_WIDTH
OFF_VG = OFF_U + SG_WIDTH
OFF_GATE = OFF_VG + SG_WIDTH
D_FF_DENSE = 5504
N_EXPERTS = 8
D_FF_EXPERT = 7168
ALPHA = (2.0 * DEPTH) ** 0.25
LN_EPS = 1e-5
SB_UNDERFLOW = 110.0

LANES = 128
V7X_VMEM_BYTES = 64 * 1024 * 1024

PROJ_TM = 1024
PROJ_TN = 1024
ATT_T = 128
ATT_HEADS = 8
SGU_ROWS = 512
MERGE_TM = 256
FFN_TM = 512
FFN_TF = 256
ROUTER_TM = 512
DISPATCH_TM = 512
MOE_TM = 512
MOE_GRANULES = 2
MOE_TF = 256
COMBINE_TM = 256


def _params(semantics, vmem_bytes):
    assert vmem_bytes < V7X_VMEM_BYTES
    return pltpu.CompilerParams(dimension_semantics=semantics, vmem_limit_bytes=vmem_bytes)


def _dot(a, b):
    return jnp.dot(a, b, preferred_element_type=F32)


def _layer_norm(y, g, b):
    mu = jnp.mean(y, axis=-1, keepdims=True)
    d = y - mu
    var = jnp.mean(d * d, axis=-1, keepdims=True)
    return d * lax.rsqrt(var + LN_EPS) * g + b


def _gelu_tanh(x):
    return 0.5 * x * (1.0 + jnp.tanh(0.7978845608028654 * (x + 0.044715 * (x * x * x))))


def _inproj_kernel(x_ref, w_ref, *rest, mode):
    *rest, w16_ref = rest

    @pl.when(pl.program_id(1) == 0)
    def _():
        w16_ref[...] = w_ref[...].astype(BF16)

    acc = _dot(x_ref[...], w16_ref[...])
    if mode == "qkv":
        (o_ref,) = rest
        scale = jnp.where(pl.program_id(0) == 0, HEAD_DIM ** -0.5, 1.0).astype(F32)
        o_ref[...] = (acc * scale).astype(o_ref.dtype)
    elif mode == "gelu":
        (o_ref,) = rest
        o_ref[...] = _gelu_tanh(acc).astype(o_ref.dtype)
    elif mode == "gelu_ln":
        g_ref, b_ref, o_ref = rest
        act = _gelu_tanh(acc)
        for grp in range(acc.shape[1] // GROUP_DIM):
            cols = slice(grp * GROUP_DIM, (grp + 1) * GROUP_DIM)
            o_ref[:, cols] = _layer_norm(act[:, cols], g_ref[:, cols], b_ref[:, cols]).astype(o_ref.dtype)
    elif mode == "gate":
        b_ref, o_ref = rest
        o_ref[...] = jax.nn.sigmoid(acc + b_ref[...]).astype(o_ref.dtype)
    else:
        raise ValueError(mode)


def _inproj(x16, w_in, layer, col_off, width, mode, extra=()):
    s, d = x16.shape
    tm, tn = PROJ_TM, PROJ_TN
    n_blk = width // tn
    off_blk = col_off // tn
    in_specs = [
        pl.BlockSpec((tm, d), lambda n, m: (m, 0)),
        pl.BlockSpec((None, d, tn), lambda n, m: (layer, 0, off_blk + n)),
    ]
    for _ in extra:
        in_specs.append(pl.BlockSpec((1, tn), lambda n, m: (0, n)))
    vmem = 2 * (tm * d * 2 + d * tn * 4 + tm * tn * 2) + d * tn * 2 + 6 * tm * tn * 4
    return pl.pallas_call(
        functools.partial(_inproj_kernel, mode=mode),
        grid=(n_blk, s // tm),
        in_specs=in_specs,
        out_specs=pl.BlockSpec((tm, tn), lambda n, m: (m, n)),
        out_shape=jax.ShapeDtypeStruct((s, width), BF16),
        scratch_shapes=[pltpu.VMEM((d, tn), BF16)],
        compiler_params=_params(("arbitrary", "arbitrary"), vmem),
        name="inproj_" + mode,
    )(x16, w_in, *extra)


def _attn_kernel(q_ref, k_ref, v_ref, o_ref, z_ref, ls_ref, hl_ref, tail_ref, a_ref, acc_ref, c_ref):
    t = ATT_T
    i = pl.program_id(1)
    row = lax.broadcasted_iota(jnp.int32, (t, t), 0)
    col = lax.broadcasted_iota(jnp.int32, (t, t), 1)
    later = jnp.where(row > col, 1.0, 0.0).astype(BF16)
    later2 = jnp.concatenate([later, later], axis=0)
    before = col < row
    heads = [slice(h * HEAD_DIM, (h + 1) * HEAD_DIM) for h in range(ATT_HEADS)]

    def key_tile(j, mask):
        keys = pl.ds(pl.multiple_of(j * t, t), t)
        for h, hs in enumerate(heads):
            z_ref[h] = lax.dot_general(q_ref[:, hs], k_ref[keys, hs], (((1,), (1,)), ((), ())),
                                       preferred_element_type=F32)
        for h in range(ATT_HEADS):
            z = z_ref[h]
            ls = jnp.minimum(z, 0.0) - jnp.log1p(jnp.exp(-jnp.abs(z)))
            lk = ls - z
            if mask is not None:
                lk = jnp.where(mask, lk, 0.0)
            hi = lk.astype(BF16)
            ls_ref[h] = ls
            hl_ref[h, :, :t] = hi
            hl_ref[h, :, t:] = (lk - hi.astype(F32)).astype(BF16)
        for h in range(ATT_HEADS):
            tail_ref[h] = _dot(hl_ref[h], later2)
        for h, hs in enumerate(heads):
            tail = tail_ref[h]
            a = jnp.exp(ls_ref[h] + tail + c_ref[:, hs])
            lk0 = ls_ref[h][:, :1] - z_ref[h][:, :1]
            if mask is not None:
                a = jnp.where(mask, a, 0.0)
                lk0 = jnp.where(mask[:, :1], lk0, 0.0)
            a_ref[h] = a.astype(BF16)
            c_ref[:, hs] += jnp.broadcast_to(tail[:, :1] + lk0, (t, HEAD_DIM))
        for h, hs in enumerate(heads):
            acc_ref[:, hs] += _dot(a_ref[h], v_ref[keys, hs])

    def any_weight_left():
        return jnp.max(c_ref[...]) > -SB_UNDERFLOW

    acc_ref[...] = jnp.zeros_like(acc_ref)
    c_ref[...] = jnp.zeros_like(c_ref)
    key_tile(i, before)

    def cond(carry):
        j, go = carry
        return jnp.logical_and(j >= 0, go)

    def body(carry):
        j, _ = carry
        key_tile(j, None)
        return j - 1, any_weight_left()

    lax.while_loop(cond, body, (i - 1, any_weight_left()))
    o_ref[...] = acc_ref[...].astype(o_ref.dtype)


def _attention(qkv):
    s = qkv.shape[0]
    t = ATT_T
    nh = ATT_HEADS
    w = nh * HEAD_DIM
    groups = SB_WIDTH // w
    resident = pl.Buffered(1)
    vmem = 2 * s * w * 2 + 2 * (2 * t * w * 2) + nh * 16 * t * t + 2 * t * w * 4 + (8 << 20)
    return pl.pallas_call(
        _attn_kernel,
        grid=(groups, s // t),
        in_specs=[
            pl.BlockSpec((t, w), lambda g, i: (i, g)),
            pl.BlockSpec((s, w), lambda g, i: (0, groups + g), pipeline_mode=resident),
            pl.BlockSpec((s, w), lambda g, i: (0, 2 * groups + g), pipeline_mode=resident),
        ],
        out_specs=pl.BlockSpec((t, w), lambda g, i: (i, g)),
        out_shape=jax.ShapeDtypeStruct((s, SB_WIDTH), BF16),
        scratch_shapes=[
            pltpu.VMEM((nh, t, t), F32),
            pltpu.VMEM((nh, t, t), F32),
            pltpu.VMEM((nh, t, 2 * t), BF16),
            pltpu.VMEM((nh, t, t), F32),
            pltpu.VMEM((nh, t, t), BF16),
            pltpu.VMEM((t, w), F32),
            pltpu.VMEM((t, w), F32),
        ],
        compiler_params=_params(("arbitrary", "arbitrary"), vmem),
        name="stickbreak_attn",
    )(qkv, qkv, qkv)


def _sgu_kernel(u_ref, v_ref, w_ref, b_ref, o_ref):
    c = CHUNK
    row = lax.broadcasted_iota(jnp.int32, (c, c), 0)
    col = lax.broadcasted_iota(jnp.int32, (c, c), 1)
    causal = col <= row
    for g in range(GROUPS):
        w = jnp.where(causal, w_ref[g], 0.0).astype(BF16)
        b = b_ref[:, g:g + 1]
        cols = slice(g * GROUP_DIM, (g + 1) * GROUP_DIM)
        for cc in range(SGU_ROWS // c):
            rows = slice(cc * c, (cc + 1) * c)
            mixed = _dot(w, v_ref[rows, cols]) + b
            o_ref[rows, cols] = (u_ref[rows, cols].astype(F32) * mixed).astype(o_ref.dtype)


def _sgu(u, vn, sg_w, sg_b_t):
    s = u.shape[0]
    r = SGU_ROWS
    vmem = 2 * 3 * r * SG_WIDTH * 2 + 2 * GROUPS * CHUNK * CHUNK * 4 + (4 << 20)
    return pl.pallas_call(
        _sgu_kernel,
        grid=(s // r,),
        in_specs=[
            pl.BlockSpec((r, SG_WIDTH), lambda i: (i, 0)),
            pl.BlockSpec((r, SG_WIDTH), lambda i: (i, 0)),
            pl.BlockSpec((GROUPS, CHUNK, CHUNK), lambda i: (0, 0, 0)),
            pl.BlockSpec((CHUNK, GROUPS), lambda i: (0, 0)),
        ],
        out_specs=pl.BlockSpec((r, SG_WIDTH), lambda i: (i, 0)),
        out_shape=jax.ShapeDtypeStruct((s, SG_WIDTH), BF16),
        compiler_params=_params(("arbitrary",), vmem),
        name="spatial_gating",
    )(u, vn, sg_w, sg_b_t)


def _merge_kernel(a_ref, b_ref, gate_ref, x_ref, wa_ref, wb_ref, wo_ref, g_ref, beta_ref,
                  o32_ref, o16_ref):
    d = D_MODEL
    ya = _dot(a_ref[...], wa_ref[...])
    yb = _dot(b_ref[...], wb_ref[...])
    merged = gate_ref[:, :d].astype(F32) * ya + gate_ref[:, d:].astype(F32) * yb
    mix = _dot(merged.astype(BF16), wo_ref[...])
    y = _layer_norm(ALPHA * x_ref[...] + mix, g_ref[...], beta_ref[...])
    o32_ref[...] = y
    o16_ref[...] = y.astype(BF16)


def _merge(att, sgu, gates, x32, wa, wb, wo, ln_g, ln_b):
    s, d = x32.shape
    tm = MERGE_TM
    const = lambda i: (0, 0)
    rowblk = lambda i: (i, 0)
    weights = (SB_WIDTH * d + SG_WIDTH * d + d * d) * 2
    vmem = 2 * weights + 2 * tm * (SB_WIDTH * 2 + SG_WIDTH * 2 + 2 * d * 2 + d * 4 + d * 4 + d * 2) \
        + 6 * tm * d * 4
    return pl.pallas_call(
        _merge_kernel,
        grid=(s // tm,),
        in_specs=[
            pl.BlockSpec((tm, SB_WIDTH), rowblk),
            pl.BlockSpec((tm, SG_WIDTH), rowblk),
            pl.BlockSpec((tm, 2 * d), rowblk),
            pl.BlockSpec((tm, d), rowblk),
            pl.BlockSpec((SB_WIDTH, d), const),
            pl.BlockSpec((SG_WIDTH, d), const),
            pl.BlockSpec((d, d), const),
            pl.BlockSpec((1, d), const),
            pl.BlockSpec((1, d), const),
        ],
        out_specs=[pl.BlockSpec((tm, d), rowblk), pl.BlockSpec((tm, d), rowblk)],
        out_shape=[jax.ShapeDtypeStruct((s, d), F32), jax.ShapeDtypeStruct((s, d), BF16)],
        compiler_params=_params(("arbitrary",), vmem),
        name="merge_outproj_ln",
    )(att, sgu, gates, x32, wa, wb, wo, ln_g, ln_b)


def _swiglu_hidden(x, w1_ref, w3_ref):
    h1 = _dot(x, w1_ref[...])
    return (h1 * jax.nn.sigmoid(h1) * _dot(x, w3_ref[...])).astype(BF16)


def _ffn_kernel(x16_ref, x32_ref, w1_ref, w3_ref, w2_ref, g_ref, b_ref, o32_ref, o16_ref, acc_ref):
    f = pl.program_id(1)

    @pl.when(f == 0)
    def _():
        acc_ref[...] = jnp.zeros_like(acc_ref)

    acc_ref[...] += _dot(_swiglu_hidden(x16_ref[...], w1_ref, w3_ref), w2_ref[...])

    @pl.when(f == pl.num_programs(1) - 1)
    def _():
        y = _layer_norm(ALPHA * x32_ref[...] + acc_ref[...], g_ref[...], b_ref[...])
        o32_ref[...] = y
        o16_ref[...] = y.astype(BF16)


def _ffn(x16, x32, w1, w3, w2, ln_g, ln_b):
    s, d = x32.shape
    ff = w1.shape[1]
    tm, tf = FFN_TM, FFN_TF
    rowblk = lambda i, f: (i, 0)
    const = lambda i, f: (0, 0)
    vmem = 2 * (tm * d * (2 + 4 + 4 + 2) + 3 * d * tf * 2) + tm * d * 4 + 4 * tm * tf * 4 + 3 * tm * d * 4
    return pl.pallas_call(
        _ffn_kernel,
        grid=(s // tm, ff // tf),
        in_specs=[
            pl.BlockSpec((tm, d), rowblk),
            pl.BlockSpec((tm, d), rowblk),
            pl.BlockSpec((d, tf), lambda i, f: (0, f)),
            pl.BlockSpec((d, tf), lambda i, f: (0, f)),
            pl.BlockSpec((tf, d), lambda i, f: (f, 0)),
            pl.BlockSpec((1, d), const),
            pl.BlockSpec((1, d), const),
        ],
        out_specs=[pl.BlockSpec((tm, d), rowblk), pl.BlockSpec((tm, d), rowblk)],
        out_shape=[jax.ShapeDtypeStruct((s, d), F32), jax.ShapeDtypeStruct((s, d), BF16)],
        scratch_shapes=[pltpu.VMEM((tm, d), F32)],
        compiler_params=_params(("arbitrary", "arbitrary"), vmem),
        name="swiglu_ln",
    )(x16, x32, w1, w3, w2, ln_g, ln_b)


ROUTE_E, ROUTE_W, ROUTE_RANK = 0, 2, 4


def _router_kernel(x_ref, w_ref, route_ref, counts_ref, seen_ref):
    @pl.when(pl.program_id(0) == 0)
    def _():
        seen_ref[...] = jnp.zeros_like(seen_ref)

    logits = jnp.dot(x_ref[...], w_ref[...], preferred_element_type=F32,
                     precision=lax.Precision.HIGHEST)
    tm = logits.shape[0]
    lane = lax.broadcasted_iota(jnp.int32, logits.shape, 1).astype(F32)
    neg = jnp.float32(-jnp.inf)
    l1 = jnp.where(lane < N_EXPERTS, logits, neg)
    m1 = jnp.max(l1, axis=-1, keepdims=True)
    i1 = jnp.min(jnp.where(l1 == m1, lane, float(LANES)), axis=-1, keepdims=True)
    l2 = jnp.where(lane == i1, neg, l1)
    m2 = jnp.max(l2, axis=-1, keepdims=True)
    i2 = jnp.min(jnp.where(l2 == m2, lane, float(LANES)), axis=-1, keepdims=True)
    e2 = jnp.exp(m2 - m1)
    w_top = 1.0 / (1.0 + e2)

    chosen = jnp.where(lane == i1, 1.0, 0.0) + jnp.where(lane == i2, 1.0, 0.0)
    row = lax.broadcasted_iota(jnp.int32, (tm, tm), 0)
    col = lax.broadcasted_iota(jnp.int32, (tm, tm), 1)
    earlier = jnp.where(col < row, 1.0, 0.0).astype(BF16)
    prefix = _dot(earlier, chosen.astype(BF16)) + seen_ref[...]
    r1 = jnp.sum(jnp.where(lane == i1, prefix, 0.0), axis=-1, keepdims=True)
    r2 = jnp.sum(jnp.where(lane == i2, prefix, 0.0), axis=-1, keepdims=True)
    seen_ref[...] += jnp.sum(chosen, axis=0, keepdims=True)
    counts_ref[...] = seen_ref[...]

    fields = ((ROUTE_E, i1), (ROUTE_E + 1, i2), (ROUTE_W, w_top), (ROUTE_W + 1, e2 * w_top),
              (ROUTE_RANK, r1), (ROUTE_RANK + 1, r2))
    route = jnp.zeros_like(logits)
    for at, val in fields:
        route = jnp.where(lane == at, val, route)
    route_ref[...] = route


def _router(x32, w_router_padded):
    s, d = x32.shape
    tm = ROUTER_TM
    vmem = 2 * (tm * d * 4 + d * LANES * 4 + tm * LANES * 4) + 8 * tm * d * 4
    return pl.pallas_call(
        _router_kernel,
        grid=(s // tm,),
        in_specs=[pl.BlockSpec((tm, d), lambda i: (i, 0)), pl.BlockSpec((d, LANES), lambda i: (0, 0))],
        out_specs=[pl.BlockSpec((tm, LANES), lambda i: (i, 0)), pl.BlockSpec((1, LANES), lambda i: (0, 0))],
        out_shape=[jax.ShapeDtypeStruct((s, LANES), F32), jax.ShapeDtypeStruct((1, LANES), F32)],
        scratch_shapes=[pltpu.VMEM((1, LANES), F32)],
        compiler_params=_params(("arbitrary",), vmem),
        name="router_top2",
    )(x32, w_router_padded)


def _row_copy(src, src_row, dst, dst_row, sem):
    return pltpu.make_async_copy(src.at[pl.ds(src_row, 1)], dst.at[pl.ds(dst_row, 1)], sem)


def _dispatch_kernel(pos_ref, x_ref, xs_zero_hbm, xs_hbm, sem):
    del xs_zero_hbm
    tm = DISPATCH_TM

    def start(r, carry):
        for k in range(2):
            _row_copy(x_ref, r, xs_hbm, pos_ref[0, 0, k * tm + r], sem).start()
        return carry

    def wait(r, carry):
        for k in range(2):
            _row_copy(x_ref, 0, xs_hbm, 0, sem).wait()
        return carry

    lax.fori_loop(0, tm, start, 0)
    lax.fori_loop(0, tm, wait, 0)


def _dispatch(pos_tiles, x32, n_rows):
    s, d = x32.shape
    tm = DISPATCH_TM
    return pl.pallas_call(
        _dispatch_kernel,
        grid=(s // tm,),
        in_specs=[
            pl.BlockSpec((1, 1, 2 * tm), lambda i: (i, 0, 0), memory_space=pltpu.SMEM),
            pl.BlockSpec((tm, d), lambda i: (i, 0)),
            pl.BlockSpec(memory_space=pl.ANY),
        ],
        out_specs=pl.BlockSpec(memory_space=pl.ANY),
        out_shape=jax.ShapeDtypeStruct((n_rows, d), F32),
        scratch_shapes=[pltpu.SemaphoreType.DMA(())],
        input_output_aliases={2: 0},
        compiler_params=_params(("arbitrary",), 2 * tm * d * 4 + (4 << 20)),
        name="moe_dispatch",
    )(pos_tiles, x32, jnp.zeros((n_rows, d), F32))


def _experts_kernel(tile_expert_ref, tile_rows_ref, n_used_ref, xs_ref, w1_ref, w3_ref, w2_ref, ys_ref,
                    x16_ref):
    del tile_expert_ref, n_used_ref
    f = pl.program_id(1)
    n_rows = tile_rows_ref[pl.program_id(0)]

    @pl.when(f == 0)
    def _():
        ys_ref[...] = jnp.zeros_like(ys_ref)

    def swiglu(rows):
        @pl.when(f == 0)
        def _():
            x16_ref[rows] = xs_ref[rows].astype(BF16)

        x = x16_ref[rows]
        h1 = _dot(x, w1_ref[...].astype(BF16))
        h = (h1 * jax.nn.sigmoid(h1) * _dot(x, w3_ref[...].astype(BF16))).astype(BF16)
        ys_ref[rows] += _dot(h, w2_ref[...].astype(BF16))

    for g in range(1, MOE_GRANULES + 1):
        @pl.when((n_rows > (g - 1) * MOE_TM) & (n_rows <= g * MOE_TM))
        def _():
            swiglu(slice(0, g * MOE_TM))


def _experts(tile_expert, tile_rows, n_used, xs, w1, w3, w2):
    n_rows, d = xs.shape
    ff = w1.shape[2]
    tm, tf = MOE_TM * MOE_GRANULES, MOE_TF
    n_f = ff // tf

    def rows(i, f, te, tr, nu):
        return (jnp.minimum(i, nu[0] - 1), 0)

    def chunk(i, f, nu):
        return jnp.where(i < nu[0], f, n_f - 1)

    vmem = tm * d * 4 + 2 * (tm * d * 4 + 3 * d * tf * 4) + tm * d * 2 + 3 * d * tf * 2 + 6 * tm * tf * 4
    return pl.pallas_call(
        _experts_kernel,
        grid_spec=pltpu.PrefetchScalarGridSpec(
            num_scalar_prefetch=3,
            grid=(n_rows // tm, n_f),
            in_specs=[
                pl.BlockSpec((tm, d), rows, pipeline_mode=pl.Buffered(1)),
                pl.BlockSpec((None, d, tf), lambda i, f, te, tr, nu: (te[i], 0, chunk(i, f, nu))),
                pl.BlockSpec((None, d, tf), lambda i, f, te, tr, nu: (te[i], 0, chunk(i, f, nu))),
                pl.BlockSpec((None, tf, d), lambda i, f, te, tr, nu: (te[i], chunk(i, f, nu), 0)),
            ],
            out_specs=pl.BlockSpec((tm, d), lambda i, f, te, tr, nu: (i, 0)),
            scratch_shapes=[pltpu.VMEM((tm, d), BF16)],
        ),
        out_shape=jax.ShapeDtypeStruct((n_rows, d), F32),
        compiler_params=_params(("arbitrary", "arbitrary"), vmem),
        name="moe_experts",
    )(tile_expert, tile_rows, n_used, xs, w1, w3, w2)


def _combine_kernel(pos_ref, pos_next_ref, route_ref, x_ref, ys_hbm, g_ref, b_ref, o_ref, buf, sems):
    tm = COMBINE_TM
    i = pl.program_id(0)
    slot = i % 2

    def start_tile(p_ref, slot):
        def start(r, carry):
            for k in range(2):
                _row_copy(ys_hbm, p_ref[0, 0, k * tm + r], buf.at[slot, k], r, sems.at[slot]).start()
            return carry
        lax.fori_loop(0, tm, start, 0)

    @pl.when(i == 0)
    def _():
        start_tile(pos_ref, 0)

    @pl.when(i + 1 < pl.num_programs(0))
    def _():
        start_tile(pos_next_ref, 1 - slot)

    def wait(r, carry):
        for k in range(2):
            _row_copy(ys_hbm, 0, buf.at[slot, k], 0, sems.at[slot]).wait()
        return carry

    lax.fori_loop(0, tm, wait, 0)
    w0 = route_ref[:, ROUTE_W:ROUTE_W + 1]
    w1 = route_ref[:, ROUTE_W + 1:ROUTE_W + 2]
    y = w0 * buf[slot, 0] + w1 * buf[slot, 1]
    o_ref[...] = _layer_norm(ALPHA * x_ref[...] + y, g_ref[...], b_ref[...])


def _combine(pos_tiles, route, x32, ys, ln_g, ln_b):
    s, d = x32.shape
    tm = COMBINE_TM
    n = s // tm
    rowblk = lambda i: (i, 0)
    const = lambda i: (0, 0)
    vmem = 2 * 2 * tm * d * 4 + 2 * (2 * tm * d * 4 + tm * LANES * 4) + 6 * tm * d * 4
    return pl.pallas_call(
        _combine_kernel,
        grid=(n,),
        in_specs=[
            pl.BlockSpec((1, 1, 2 * tm), lambda i: (i, 0, 0), memory_space=pltpu.SMEM),
            pl.BlockSpec((1, 1, 2 * tm), lambda i: (jnp.minimum(i + 1, n - 1), 0, 0),
                         memory_space=pltpu.SMEM),
            pl.BlockSpec((tm, LANES), rowblk),
            pl.BlockSpec((tm, d), rowblk),
            pl.BlockSpec(memory_space=pl.ANY),
            pl.BlockSpec((1, d), const),
            pl.BlockSpec((1, d), const),
        ],
        out_specs=pl.BlockSpec((tm, d), rowblk),
        out_shape=jax.ShapeDtypeStruct((s, d), F32),
        scratch_shapes=[pltpu.VMEM((2, 2, tm, d), F32), pltpu.SemaphoreType.DMA((2,))],
        compiler_params=_params(("arbitrary",), vmem),
        name="moe_combine_ln",
    )(pos_tiles, pos_tiles, route, x32, ys, ln_g, ln_b)


def _pos_tiles(pos, tm):
    s = pos.shape[0]
    return pos.reshape(s // tm, tm, 2).transpose(0, 2, 1).reshape(s // tm, 1, 2 * tm)


def _moe(x32, w_router, w1, w3, w2, ln_g, ln_b):
    s, d = x32.shape
    tm = MOE_TM * MOE_GRANULES
    n_tiles = (2 * s) // tm + N_EXPERTS
    route, counts = _router(x32, jnp.pad(w_router, ((0, 0), (0, LANES - N_EXPERTS))))

    counts = counts[0, :N_EXPERTS].astype(jnp.int32)
    tiles_per_expert = (counts + tm - 1) // tm
    tile_end = jnp.cumsum(tiles_per_expert)
    tile_start = tile_end - tiles_per_expert
    n_used = tile_end[-1:]
    tile_ids = jnp.arange(n_tiles, dtype=jnp.int32)
    tile_expert = jnp.sum(jnp.minimum(tile_ids, n_used - 1)[:, None] >= tile_end[None, :],
                          axis=1).astype(jnp.int32)
    tile_rows = jnp.clip(counts[tile_expert] - (tile_ids - tile_start[tile_expert]) * tm, 0, tm)
    tile_rows = jnp.where(tile_ids < n_used, tile_rows, 0).astype(jnp.int32)
    experts = route[:, ROUTE_E:ROUTE_E + 2].astype(jnp.int32)
    pos = (tile_start * tm)[experts] + route[:, ROUTE_RANK:ROUTE_RANK + 2].astype(jnp.int32)

    xs = _dispatch(_pos_tiles(pos, DISPATCH_TM), x32, n_tiles * tm)
    ys = _experts(tile_expert, tile_rows, n_used, xs, w1, w3, w2)
    return _combine(_pos_tiles(pos, COMBINE_TM), route, x32, ys, ln_g, ln_b)


def _pad_ff(w, axis, mult):
    pad = (-w.shape[axis]) % mult
    if pad == 0:
        return w
    widths = [(0, 0)] * w.ndim
    widths[axis] = (0, pad)
    return jnp.pad(w, widths)


def kernel(x, w_in, b_gate, sg_w, sg_b, sg_ln_g, sg_ln_b, w_branch_a, w_branch_b, w_out,
           ln1_g, ln1_b, ffn_w1, ffn_w3, ffn_w2, moe_router, moe_w1, moe_w3, moe_w2,
           ln2_g, ln2_b):
    b, s, d = x.shape
    assert (b, s, d) == (1, SEQ, D_MODEL)
    x32 = x.reshape(s, d)
    x16 = x32.astype(BF16)
    for layer in range(DEPTH):
        qkv = _inproj(x16, w_in, layer, 0, 3 * SB_WIDTH, "qkv")
        u = _inproj(x16, w_in, layer, OFF_U, SG_WIDTH, "gelu")
        vn = _inproj(x16, w_in, layer, OFF_VG, SG_WIDTH, "gelu_ln",
                     (sg_ln_g[layer].reshape(1, -1), sg_ln_b[layer].reshape(1, -1)))
        gates = _inproj(x16, w_in, layer, OFF_GATE, 2 * d, "gate", (b_gate[layer].reshape(1, -1),))
        att = _attention(qkv)
        sgu = _sgu(u, vn, sg_w[layer], sg_b[layer].T)
        x32, x16 = _merge(att, sgu, gates, x32,
                          w_branch_a[layer].astype(BF16), w_branch_b[layer].astype(BF16),
                          w_out[layer].astype(BF16),
                          ln1_g[layer].reshape(1, d), ln1_b[layer].reshape(1, d))
        i = layer // 2
        g2, b2 = ln2_g[layer].reshape(1, d), ln2_b[layer].reshape(1, d)
        if layer % 2 == 0:
            w1 = _pad_ff(ffn_w1[i].astype(BF16), 1, FFN_TF)
            w3 = _pad_ff(ffn_w3[i].astype(BF16), 1, FFN_TF)
            w2 = _pad_ff(ffn_w2[i].astype(BF16), 0, FFN_TF)
            x32, x16 = _ffn(x16, x32, w1, w3, w2, g2, b2)
        else:
            x32 = _moe(x32, moe_router[i], moe_w1[i], moe_w3[i], moe_w2[i], g2, b2)
            x16 = x32.astype(BF16) if layer + 1 < DEPTH else None
    return x32.reshape(b, s, d)
```

```python
import functools

import jax
import jax.numpy as jnp
from jax import lax
from jax.experimental import pallas as pl
from jax.experimental.pallas import tpu as pltpu

F32 = jnp.float32
BF16 = jnp.bfloat16

D_MODEL = 2048
SEQ = 8192
DEPTH = 2
HEADS = 8
HEAD_DIM = 128
SB_WIDTH = HEADS * HEAD_DIM
GROUPS = 8
GROUP_DIM = 128
SG_WIDTH = GROUPS * GROUP_DIM
CHUNK = 128
OFF_U = 3 * SB_WIDTH
OFF_VG = OFF_U + SG_WIDTH
OFF_GATE = OFF_VG + SG_WIDTH
D_FF_DENSE = 5504
N_EXPERTS = 8
D_FF_EXPERT = 7168
ALPHA = (2.0 * DEPTH) ** 0.25
LN_EPS = 1e-5
SB_UNDERFLOW = 110.0

LANES = 128
V7X_VMEM_BYTES = 64 * 1024 * 1024

PROJ_TM = 1024
PROJ_TN = 1024
ATT_T = 128
ATT_HEADS = 8
SGU_ROWS = 512
MERGE_TM = 256
FFN_TM = 1024
FFN_TF = 256
LN_ROWS = 256
ROUTER_TM = 512
DISPATCH_TM = 512
MOE_TM = 512
MOE_GRANULES = 2
MOE_TF = 256
COMBINE_TM = 256


def _params(semantics, vmem_bytes):
    assert vmem_bytes < V7X_VMEM_BYTES
    return pltpu.CompilerParams(dimension_semantics=semantics, vmem_limit_bytes=vmem_bytes)


def _dot(a, b):
    return jnp.dot(a, b, preferred_element_type=F32)


def _layer_norm(y, g, b):
    mu = jnp.mean(y, axis=-1, keepdims=True)
    d = y - mu
    var = jnp.mean(d * d, axis=-1, keepdims=True)
    return d * lax.rsqrt(var + LN_EPS) * g + b


def _gelu_tanh(x):
    return 0.5 * x * (1.0 + jnp.tanh(0.7978845608028654 * (x + 0.044715 * (x * x * x))))


def _inproj_kernel(x_ref, w_ref, *rest, mode):
    *rest, w16_ref = rest

    @pl.when(pl.program_id(1) == 0)
    def _():
        w16_ref[...] = w_ref[...].astype(BF16)

    acc = _dot(x_ref[...], w16_ref[...])
    if mode == "qkv":
        (o_ref,) = rest
        scale = jnp.where(pl.program_id(0) == 0, HEAD_DIM ** -0.5, 1.0).astype(F32)
        o_ref[...] = (acc * scale).astype(o_ref.dtype)
    elif mode == "gelu":
        (o_ref,) = rest
        o_ref[...] = _gelu_tanh(acc).astype(o_ref.dtype)
    elif mode == "gelu_ln":
        g_ref, b_ref, o_ref = rest
        act = _gelu_tanh(acc)
        for grp in range(acc.shape[1] // GROUP_DIM):
            cols = slice(grp * GROUP_DIM, (grp + 1) * GROUP_DIM)
            o_ref[:, cols] = _layer_norm(act[:, cols], g_ref[:, cols], b_ref[:, cols]).astype(o_ref.dtype)
    elif mode == "gate":
        b_ref, o_ref = rest
        o_ref[...] = jax.nn.sigmoid(acc + b_ref[...]).astype(o_ref.dtype)
    else:
        raise ValueError(mode)


def _inproj(x16, w_in, layer, col_off, width, mode, extra=()):
    s, d = x16.shape
    tm, tn = PROJ_TM, PROJ_TN
    n_blk = width // tn
    off_blk = col_off // tn
    in_specs = [
        pl.BlockSpec((tm, d), lambda n, m: (m, 0)),
        pl.BlockSpec((None, d, tn), lambda n, m: (layer, 0, off_blk + n)),
    ]
    for _ in extra:
        in_specs.append(pl.BlockSpec((1, tn), lambda n, m: (0, n)))
    vmem = 2 * (tm * d * 2 + d * tn * 4 + tm * tn * 2) + d * tn * 2 + 6 * tm * tn * 4
    return pl.pallas_call(
        functools.partial(_inproj_kernel, mode=mode),
        grid=(n_blk, s // tm),
        in_specs=in_specs,
        out_specs=pl.BlockSpec((tm, tn), lambda n, m: (m, n)),
        out_shape=jax.ShapeDtypeStruct((s, width), BF16),
        scratch_shapes=[pltpu.VMEM((d, tn), BF16)],
        compiler_params=_params(("arbitrary", "arbitrary"), vmem),
        name="inproj_" + mode,
    )(x16, w_in, *extra)


def _attn_kernel(q_ref, k_ref, v_ref, o_ref, z_ref, ls_ref, hl_ref, tail_ref, a_ref, acc_ref, c_ref):
    t = ATT_T
    i = pl.program_id(1)
    row = lax.broadcasted_iota(jnp.int32, (t, t), 0)
    col = lax.broadcasted_iota(jnp.int32, (t, t), 1)
    later = jnp.where(row > col, 1.0, 0.0).astype(BF16)
    later2 = jnp.concatenate([later, later], axis=0)
    before = col < row
    heads = [slice(h * HEAD_DIM, (h + 1) * HEAD_DIM) for h in range(ATT_HEADS)]

    def key_tile(j, mask):
        keys = pl.ds(pl.multiple_of(j * t, t), t)
        for h, hs in enumerate(heads):
            z_ref[h] = lax.dot_general(q_ref[:, hs], k_ref[keys, hs], (((1,), (1,)), ((), ())),
                                       preferred_element_type=F32)
        for h in range(ATT_HEADS):
            z = z_ref[h]
            ls = jnp.minimum(z, 0.0) - jnp.log1p(jnp.exp(-jnp.abs(z)))
            lk = ls - z
            if mask is not None:
                lk = jnp.where(mask, lk, 0.0)
            hi = lk.astype(BF16)
            ls_ref[h] = ls
            hl_ref[h, :, :t] = hi
            hl_ref[h, :, t:] = (lk - hi.astype(F32)).astype(BF16)
        for h in range(ATT_HEADS):
            tail_ref[h] = _dot(hl_ref[h], later2)
        for h, hs in enumerate(heads):
            tail = tail_ref[h]
            a = jnp.exp(ls_ref[h] + tail + c_ref[:, hs])
            lk0 = ls_ref[h][:, :1] - z_ref[h][:, :1]
            if mask is not None:
                a = jnp.where(mask, a, 0.0)
                lk0 = jnp.where(mask[:, :1], lk0, 0.0)
            a_ref[h] = a.astype(BF16)
            c_ref[:, hs] += jnp.broadcast_to(tail[:, :1] + lk0, (t, HEAD_DIM))
        for h, hs in enumerate(heads):
            acc_ref[:, hs] += _dot(a_ref[h], v_ref[keys, hs])

    def any_weight_left():
        return jnp.max(c_ref[...]) > -SB_UNDERFLOW

    acc_ref[...] = jnp.zeros_like(acc_ref)
    c_ref[...] = jnp.zeros_like(c_ref)
    key_tile(i, before)

    def cond(carry):
        j, go = carry
        return jnp.logical_and(j >= 0, go)

    def body(carry):
        j, _ = carry
        key_tile(j, None)
        return j - 1, any_weight_left()

    lax.while_loop(cond, body, (i - 1, any_weight_left()))
    o_ref[...] = acc_ref[...].astype(o_ref.dtype)


def _attention(qkv):
    s = qkv.shape[0]
    t = ATT_T
    nh = ATT_HEADS
    w = nh * HEAD_DIM
    groups = SB_WIDTH // w
    resident = pl.Buffered(1)
    vmem = 2 * s * w * 2 + 2 * (2 * t * w * 2) + nh * 16 * t * t + 2 * t * w * 4 + (8 << 20)
    return pl.pallas_call(
        _attn_kernel,
        grid=(groups, s // t),
        in_specs=[
            pl.BlockSpec((t, w), lambda g, i: (i, g)),
            pl.BlockSpec((s, w), lambda g, i: (0, groups + g), pipeline_mode=resident),
            pl.BlockSpec((s, w), lambda g, i: (0, 2 * groups + g), pipeline_mode=resident),
        ],
        out_specs=pl.BlockSpec((t, w), lambda g, i: (i, g)),
        out_shape=jax.ShapeDtypeStruct((s, SB_WIDTH), BF16),
        scratch_shapes=[
            pltpu.VMEM((nh, t, t), F32),
            pltpu.VMEM((nh, t, t), F32),
            pltpu.VMEM((nh, t, 2 * t), BF16),
            pltpu.VMEM((nh, t, t), F32),
            pltpu.VMEM((nh, t, t), BF16),
            pltpu.VMEM((t, w), F32),
            pltpu.VMEM((t, w), F32),
        ],
        compiler_params=_params(("arbitrary", "arbitrary"), vmem),
        name="stickbreak_attn",
    )(qkv, qkv, qkv)


def _sgu_kernel(u_ref, v_ref, w_ref, b_ref, o_ref):
    c = CHUNK
    row = lax.broadcasted_iota(jnp.int32, (c, c), 0)
    col = lax.broadcasted_iota(jnp.int32, (c, c), 1)
    causal = col <= row
    for g in range(GROUPS):
        w = jnp.where(causal, w_ref[g], 0.0).astype(BF16)
        b = b_ref[:, g:g + 1]
        cols = slice(g * GROUP_DIM, (g + 1) * GROUP_DIM)
        for cc in range(SGU_ROWS // c):
            rows = slice(cc * c, (cc + 1) * c)
            mixed = _dot(w, v_ref[rows, cols]) + b
            o_ref[rows, cols] = (u_ref[rows, cols].astype(F32) * mixed).astype(o_ref.dtype)


def _sgu(u, vn, sg_w, sg_b_t):
    s = u.shape[0]
    r = SGU_ROWS
    vmem = 2 * 3 * r * SG_WIDTH * 2 + 2 * GROUPS * CHUNK * CHUNK * 4 + (4 << 20)
    return pl.pallas_call(
        _sgu_kernel,
        grid=(s // r,),
        in_specs=[
            pl.BlockSpec((r, SG_WIDTH), lambda i: (i, 0)),
            pl.BlockSpec((r, SG_WIDTH), lambda i: (i, 0)),
            pl.BlockSpec((GROUPS, CHUNK, CHUNK), lambda i: (0, 0, 0)),
            pl.BlockSpec((CHUNK, GROUPS), lambda i: (0, 0)),
        ],
        out_specs=pl.BlockSpec((r, SG_WIDTH), lambda i: (i, 0)),
        out_shape=jax.ShapeDtypeStruct((s, SG_WIDTH), BF16),
        compiler_params=_params(("arbitrary",), vmem),
        name="spatial_gating",
    )(u, vn, sg_w, sg_b_t)


def _merge_kernel(a_ref, b_ref, gate_ref, x_ref, wa_ref, wb_ref, wo_ref, g_ref, beta_ref,
                  o32_ref, o16_ref):
    d = D_MODEL
    ya = _dot(a_ref[...], wa_ref[...])
    yb = _dot(b_ref[...], wb_ref[...])
    merged = gate_ref[:, :d].astype(F32) * ya + gate_ref[:, d:].astype(F32) * yb
    mix = _dot(merged.astype(BF16), wo_ref[...])
    y = _layer_norm(ALPHA * x_ref[...] + mix, g_ref[...], beta_ref[...])
    o32_ref[...] = y
    o16_ref[...] = y.astype(BF16)


def _merge(att, sgu, gates, x32, wa, wb, wo, ln_g, ln_b):
    s, d = x32.shape
    tm = MERGE_TM
    const = lambda i: (0, 0)
    rowblk = lambda i: (i, 0)
    weights = (SB_WIDTH * d + SG_WIDTH * d + d * d) * 2
    vmem = 2 * weights + 2 * tm * (SB_WIDTH * 2 + SG_WIDTH * 2 + 2 * d * 2 + d * 4 + d * 4 + d * 2) \
        + 6 * tm * d * 4
    return pl.pallas_call(
        _merge_kernel,
        grid=(s // tm,),
        in_specs=[
            pl.BlockSpec((tm, SB_WIDTH), rowblk),
            pl.BlockSpec((tm, SG_WIDTH), rowblk),
            pl.BlockSpec((tm, 2 * d), rowblk),
            pl.BlockSpec((tm, d), rowblk),
            pl.BlockSpec((SB_WIDTH, d), const),
            pl.BlockSpec((SG_WIDTH, d), const),
            pl.BlockSpec((d, d), const),
            pl.BlockSpec((1, d), const),
            pl.BlockSpec((1, d), const),
        ],
        out_specs=[pl.BlockSpec((tm, d), rowblk), pl.BlockSpec((tm, d), rowblk)],
        out_shape=[jax.ShapeDtypeStruct((s, d), F32), jax.ShapeDtypeStruct((s, d), BF16)],
        compiler_params=_params(("arbitrary",), vmem),
        name="merge_outproj_ln",
    )(att, sgu, gates, x32, wa, wb, wo, ln_g, ln_b)


def _swiglu_hidden(x, w1_ref, w3_ref):
    h1 = _dot(x, w1_ref[...])
    return (h1 * jax.nn.sigmoid(h1) * _dot(x, w3_ref[...])).astype(BF16)


def _ffn_kernel(x16_ref, x32_ref, w1_ref, w3_ref, w2_ref, g_ref, b_ref, o32_ref, o16_ref):
    f = pl.program_id(1)

    @pl.when(f == 0)
    def _():
        o32_ref[...] = jnp.zeros_like(o32_ref)

    o32_ref[...] += _dot(_swiglu_hidden(x16_ref[...], w1_ref, w3_ref), w2_ref[...])

    @pl.when(f == pl.num_programs(1) - 1)
    def _():
        def ln_rows(c, carry):
            rows = pl.ds(pl.multiple_of(c * LN_ROWS, LN_ROWS), LN_ROWS)
            y = _layer_norm(ALPHA * x32_ref[rows, :] + o32_ref[rows, :], g_ref[...], b_ref[...])
            o32_ref[rows, :] = y
            o16_ref[rows, :] = y.astype(BF16)
            return carry
        lax.fori_loop(0, o32_ref.shape[0] // LN_ROWS, ln_rows, 0)


def _ffn(x16, x32, w1, w3, w2, ln_g, ln_b):
    s, d = x32.shape
    ff = w1.shape[1]
    tm, tf = FFN_TM, FFN_TF
    rowblk = lambda i, f: (i, 0)
    const = lambda i, f: (0, 0)
    vmem = tm * d * 4 + 2 * (tm * d * (2 + 4 + 2) + 3 * d * tf * 2) + 5 * tm * tf * 4 + 4 * LN_ROWS * d * 4
    return pl.pallas_call(
        _ffn_kernel,
        grid=(s // tm, ff // tf),
        in_specs=[
            pl.BlockSpec((tm, d), rowblk),
            pl.BlockSpec((tm, d), rowblk, pipeline_mode=pl.Buffered(1)),
            pl.BlockSpec((d, tf), lambda i, f: (0, f)),
            pl.BlockSpec((d, tf), lambda i, f: (0, f)),
            pl.BlockSpec((tf, d), lambda i, f: (f, 0)),
            pl.BlockSpec((1, d), const),
            pl.BlockSpec((1, d), const),
        ],
        out_specs=[pl.BlockSpec((tm, d), rowblk), pl.BlockSpec((tm, d), rowblk)],
        out_shape=[jax.ShapeDtypeStruct((s, d), F32), jax.ShapeDtypeStruct((s, d), BF16)],
        compiler_params=_params(("arbitrary", "arbitrary"), vmem),
        name="swiglu_ln",
    )(x16, x32, w1, w3, w2, ln_g, ln_b)


ROUTE_E, ROUTE_W, ROUTE_RANK = 0, 2, 4


def _router_kernel(x_ref, w_ref, route_ref, counts_ref, seen_ref):
    @pl.when(pl.program_id(0) == 0)
    def _():
        seen_ref[...] = jnp.zeros_like(seen_ref)

    logits = jnp.dot(x_ref[...], w_ref[...], preferred_element_type=F32,
                     precision=lax.Precision.HIGHEST)
    tm = logits.shape[0]
    lane = lax.broadcasted_iota(jnp.int32, logits.shape, 1).astype(F32)
    neg = jnp.float32(-jnp.inf)
    l1 = jnp.where(lane < N_EXPERTS, logits, neg)
    m1 = jnp.max(l1, axis=-1, keepdims=True)
    i1 = jnp.min(jnp.where(l1 == m1, lane, float(LANES)), axis=-1, keepdims=True)
    l2 = jnp.where(lane == i1, neg, l1)
    m2 = jnp.max(l2, axis=-1, keepdims=True)
    i2 = jnp.min(jnp.where(l2 == m2, lane, float(LANES)), axis=-1, keepdims=True)
    e2 = jnp.exp(m2 - m1)
    w_top = 1.0 / (1.0 + e2)

    chosen = jnp.where(lane == i1, 1.0, 0.0) + jnp.where(lane == i2, 1.0, 0.0)
    row = lax.broadcasted_iota(jnp.int32, (tm, tm), 0)
    col = lax.broadcasted_iota(jnp.int32, (tm, tm), 1)
    earlier = jnp.where(col < row, 1.0, 0.0).astype(BF16)
    prefix = _dot(earlier, chosen.astype(BF16)) + seen_ref[...]
    r1 = jnp.sum(jnp.where(lane == i1, prefix, 0.0), axis=-1, keepdims=True)
    r2 = jnp.sum(jnp.where(lane == i2, prefix, 0.0), axis=-1, keepdims=True)
    seen_ref[...] += jnp.sum(chosen, axis=0, keepdims=True)
    counts_ref[...] = seen_ref[...]

    fields = ((ROUTE_E, i1), (ROUTE_E + 1, i2), (ROUTE_W, w_top), (ROUTE_W + 1, e2 * w_top),
              (ROUTE_RANK, r1), (ROUTE_RANK + 1, r2))
    route = jnp.zeros_like(logits)
    for at, val in fields:
        route = jnp.where(lane == at, val, route)
    route_ref[...] = route


def _router(x32, w_router_padded):
    s, d = x32.shape
    tm = ROUTER_TM
    vmem = 2 * (tm * d * 4 + d * LANES * 4 + tm * LANES * 4) + 8 * tm * d * 4
    return pl.pallas_call(
        _router_kernel,
        grid=(s // tm,),
        in_specs=[pl.BlockSpec((tm, d), lambda i: (i, 0)), pl.BlockSpec((d, LANES), lambda i: (0, 0))],
        out_specs=[pl.BlockSpec((tm, LANES), lambda i: (i, 0)), pl.BlockSpec((1, LANES), lambda i: (0, 0))],
        out_shape=[jax.ShapeDtypeStruct((s, LANES), F32), jax.ShapeDtypeStruct((1, LANES), F32)],
        scratch_shapes=[pltpu.VMEM((1, LANES), F32)],
        compiler_params=_params(("arbitrary",), vmem),
        name="router_top2",
    )(x32, w_router_padded)


def _row_copy(src, src_row, dst, dst_row, sem):
    return pltpu.make_async_copy(src.at[pl.ds(src_row, 1)], dst.at[pl.ds(dst_row, 1)], sem)


def _dispatch_kernel(pos_ref, x_ref, xs_zero_hbm, xs_hbm, sem):
    del xs_zero_hbm
    tm = DISPATCH_TM

    def start(r, carry):
        for k in range(2):
            _row_copy(x_ref, r, xs_hbm, pos_ref[0, 0, k * tm + r], sem).start(priority=k)
        return carry

    def wait(r, carry):
        for k in range(2):
            _row_copy(x_ref, 0, xs_hbm, 0, sem).wait()
        return carry

    lax.fori_loop(0, tm, start, 0)
    lax.fori_loop(0, tm, wait, 0)


def _dispatch(pos_tiles, x32, n_rows):
    s, d = x32.shape
    tm = DISPATCH_TM
    return pl.pallas_call(
        _dispatch_kernel,
        grid=(s // tm,),
        in_specs=[
            pl.BlockSpec((1, 1, 2 * tm), lambda i: (i, 0, 0), memory_space=pltpu.SMEM),
            pl.BlockSpec((tm, d), lambda i: (i, 0)),
            pl.BlockSpec(memory_space=pl.ANY),
        ],
        out_specs=pl.BlockSpec(memory_space=pl.ANY),
        out_shape=jax.ShapeDtypeStruct((n_rows, d), F32),
        scratch_shapes=[pltpu.SemaphoreType.DMA(())],
        input_output_aliases={2: 0},
        compiler_params=_params(("arbitrary",), 2 * tm * d * 4 + (4 << 20)),
        name="moe_dispatch",
    )(pos_tiles, x32, jnp.zeros((n_rows, d), F32))


def _experts_kernel(tile_expert_ref, tile_rows_ref, n_used_ref, xs_ref, w1_ref, w3_ref, w2_ref, ys_ref,
                    x16_ref):
    del tile_expert_ref, n_used_ref
    f = pl.program_id(1)
    n_rows = tile_rows_ref[pl.program_id(0)]

    @pl.when(f == 0)
    def _():
        ys_ref[...] = jnp.zeros_like(ys_ref)

    def swiglu(rows):
        @pl.when(f == 0)
        def _():
            x16_ref[rows] = xs_ref[rows].astype(BF16)

        x = x16_ref[rows]
        h1 = _dot(x, w1_ref[...].astype(BF16))
        h = (h1 * jax.nn.sigmoid(h1) * _dot(x, w3_ref[...].astype(BF16))).astype(BF16)
        ys_ref[rows] += _dot(h, w2_ref[...].astype(BF16))

    for g in range(1, MOE_GRANULES + 1):
        @pl.when((n_rows > (g - 1) * MOE_TM) & (n_rows <= g * MOE_TM))
        def _():
            swiglu(slice(0, g * MOE_TM))


def _experts(tile_expert, tile_rows, n_used, xs, w1, w3, w2):
    n_rows, d = xs.shape
    ff = w1.shape[2]
    tm, tf = MOE_TM * MOE_GRANULES, MOE_TF
    n_f = ff // tf

    def rows(i, f, te, tr, nu):
        return (jnp.minimum(i, nu[0] - 1), 0)

    def chunk(i, f, nu):
        return jnp.where(i < nu[0], f, n_f - 1)

    vmem = tm * d * 4 + 2 * (tm * d * 4 + 3 * d * tf * 4) + tm * d * 2 + 3 * d * tf * 2 + 6 * tm * tf * 4
    return pl.pallas_call(
        _experts_kernel,
        grid_spec=pltpu.PrefetchScalarGridSpec(
            num_scalar_prefetch=3,
            grid=(n_rows // tm, n_f),
            in_specs=[
                pl.BlockSpec((tm, d), rows, pipeline_mode=pl.Buffered(1)),
                pl.BlockSpec((None, d, tf), lambda i, f, te, tr, nu: (te[i], 0, chunk(i, f, nu))),
                pl.BlockSpec((None, d, tf), lambda i, f, te, tr, nu: (te[i], 0, chunk(i, f, nu))),
                pl.BlockSpec((None, tf, d), lambda i, f, te, tr, nu: (te[i], chunk(i, f, nu), 0)),
            ],
            out_specs=pl.BlockSpec((tm, d), lambda i, f, te, tr, nu: (i, 0)),
            scratch_shapes=[pltpu.VMEM((tm, d), BF16)],
        ),
        out_shape=jax.ShapeDtypeStruct((n_rows, d), F32),
        compiler_params=_params(("arbitrary", "arbitrary"), vmem),
        name="moe_experts",
    )(tile_expert, tile_rows, n_used, xs, w1, w3, w2)


def _combine_kernel(pos_ref, pos_next_ref, route_ref, x_ref, ys_hbm, g_ref, b_ref, o_ref, buf, sems):
    tm = COMBINE_TM
    i = pl.program_id(0)
    slot = i % 2

    def start_tile(p_ref, slot):
        def start(r, carry):
            for k in range(2):
                _row_copy(ys_hbm, p_ref[0, 0, k * tm + r], buf.at[slot, k], r, sems.at[slot]).start(priority=k)
            return carry
        lax.fori_loop(0, tm, start, 0)

    @pl.when(i == 0)
    def _():
        start_tile(pos_ref, 0)

    @pl.when(i + 1 < pl.num_programs(0))
    def _():
        start_tile(pos_next_ref, 1 - slot)

    def wait(r, carry):
        for k in range(2):
            _row_copy(ys_hbm, 0, buf.at[slot, k], 0, sems.at[slot]).wait()
        return carry

    lax.fori_loop(0, tm, wait, 0)
    w0 = route_ref[:, ROUTE_W:ROUTE_W + 1]
    w1 = route_ref[:, ROUTE_W + 1:ROUTE_W + 2]
    y = w0 * buf[slot, 0] + w1 * buf[slot, 1]
    o_ref[...] = _layer_norm(ALPHA * x_ref[...] + y, g_ref[...], b_ref[...])


def _combine(pos_tiles, route, x32, ys, ln_g, ln_b):
    s, d = x32.shape
    tm = COMBINE_TM
    n = s // tm
    rowblk = lambda i: (i, 0)
    const = lambda i: (0, 0)
    vmem = 2 * 2 * tm * d * 4 + 2 * (2 * tm * d * 4 + tm * LANES * 4) + 6 * tm * d * 4
    return pl.pallas_call(
        _combine_kernel,
        grid=(n,),
        in_specs=[
            pl.BlockSpec((1, 1, 2 * tm), lambda i: (i, 0, 0), memory_space=pltpu.SMEM),
            pl.BlockSpec((1, 1, 2 * tm), lambda i: (jnp.minimum(i + 1, n - 1), 0, 0),
                         memory_space=pltpu.SMEM),
            pl.BlockSpec((tm, LANES), rowblk),
            pl.BlockSpec((tm, d), rowblk),
            pl.BlockSpec(memory_space=pl.ANY),
            pl.BlockSpec((1, d), const),
            pl.BlockSpec((1, d), const),
        ],
        out_specs=pl.BlockSpec((tm, d), rowblk),
        out_shape=jax.ShapeDtypeStruct((s, d), F32),
        scratch_shapes=[pltpu.VMEM((2, 2, tm, d), F32), pltpu.SemaphoreType.DMA((2,))],
        compiler_params=_params(("arbitrary",), vmem),
        name="moe_combine_ln",
    )(pos_tiles, pos_tiles, route, x32, ys, ln_g, ln_b)


def _pos_tiles(pos, tm):
    s = pos.shape[0]
    return pos.reshape(s // tm, tm, 2).transpose(0, 2, 1).reshape(s // tm, 1, 2 * tm)


def _moe(x32, w_router, w1, w3, w2, ln_g, ln_b):
    s, d = x32.shape
    tm = MOE_TM * MOE_GRANULES
    n_tiles = (2 * s) // tm + N_EXPERTS
    route, counts = _router(x32, jnp.pad(w_router, ((0, 0), (0, LANES - N_EXPERTS))))

    counts = counts[0, :N_EXPERTS].astype(jnp.int32)
    tiles_per_expert = (counts + tm - 1) // tm
    tile_end = jnp.cumsum(tiles_per_expert)
    tile_start = tile_end - tiles_per_expert
    n_used = tile_end[-1:]
    tile_ids = jnp.arange(n_tiles, dtype=jnp.int32)
    tile_expert = jnp.sum(jnp.minimum(tile_ids, n_used - 1)[:, None] >= tile_end[None, :],
                          axis=1).astype(jnp.int32)
    tile_rows = jnp.clip(counts[tile_expert] - (tile_ids - tile_start[tile_expert]) * tm, 0, tm)
    tile_rows = jnp.where(tile_ids < n_used, tile_rows, 0).astype(jnp.int32)
    experts = route[:, ROUTE_E:ROUTE_E + 2].astype(jnp.int32)
    pos = (tile_start * tm)[experts] + route[:, ROUTE_RANK:ROUTE_RANK + 2].astype(jnp.int32)

    xs = _dispatch(_pos_tiles(pos, DISPATCH_TM), x32, n_tiles * tm)
    ys = _experts(tile_expert, tile_rows, n_used, xs, w1, w3, w2)
    return _combine(_pos_tiles(pos, COMBINE_TM), route, x32, ys, ln_g, ln_b)


def _pad_ff(w, axis, mult):
    pad = (-w.shape[axis]) % mult
    if pad == 0:
        return w
    widths = [(0, 0)] * w.ndim
    widths[axis] = (0, pad)
    return jnp.pad(w, widths)


def kernel(x, w_in, b_gate, sg_w, sg_b, sg_ln_g, sg_ln_b, w_branch_a, w_branch_b, w_out,
           ln1_g, ln1_b, ffn_w1, ffn_w3, ffn_w2, moe_router, moe_w1, moe_w3, moe_w2,
           ln2_g, ln2_b):
    b, s, d = x.shape
    assert (b, s, d) == (1, SEQ, D_MODEL)
    x32 = x.reshape(s, d)
    x16 = x32.astype(BF16)
    for layer in range(DEPTH):
        qkv = _inproj(x16, w_in, layer, 0, 3 * SB_WIDTH, "qkv")
        u = _inproj(x16, w_in, layer, OFF_U, SG_WIDTH, "gelu")
        vn = _inproj(x16, w_in, layer, OFF_VG, SG_WIDTH, "gelu_ln",
                     (sg_ln_g[layer].reshape(1, -1), sg_ln_b[layer].reshape(1, -1)))
        gates = _inproj(x16, w_in, layer, OFF_GATE, 2 * d, "gate", (b_gate[layer].reshape(1, -1),))
        att = _attention(qkv)
        sgu = _sgu(u, vn, sg_w[layer], sg_b[layer].T)
        x32, x16 = _merge(att, sgu, gates, x32,
                          w_branch_a[layer].astype(BF16), w_branch_b[layer].astype(BF16),
                          w_out[layer].astype(BF16),
                          ln1_g[layer].reshape(1, d), ln1_b[layer].reshape(1, d))
        i = layer // 2
        g2, b2 = ln2_g[layer].reshape(1, d), ln2_b[layer].reshape(1, d)
        if layer % 2 == 0:
            w1 = _pad_ff(ffn_w1[i].astype(BF16), 1, FFN_TF)
            w3 = _pad_ff(ffn_w3[i].astype(BF16), 1, FFN_TF)
            w2 = _pad_ff(ffn_w2[i].astype(BF16), 0, FFN_TF)
            x32, x16 = _ffn(x16, x32, w1, w3, w2, g2, b2)
        else:
            x32 = _moe(x32, moe_router[i], moe_w1[i], moe_w3[i], moe_w2[i], g2, b2)
            x16 = x32.astype(BF16) if layer + 1 < DEPTH else None
    return x32.reshape(b, s, d)
```

```python
import functools

import jax
import jax.numpy as jnp
from jax import lax
from jax.experimental import pallas as pl
from jax.experimental.pallas import tpu as pltpu

F32 = jnp.float32
BF16 = jnp.bfloat16

D_MODEL = 2048
SEQ = 8192
DEPTH = 2
HEADS = 8
HEAD_DIM = 128
SB_WIDTH = HEADS * HEAD_DIM
GROUPS = 8
GROUP_DIM = 128
SG_WIDTH = GROUPS * GROUP_DIM
CHUNK = 128
OFF_U = 3 * SB_WIDTH
OFF_VG = OFF_U + SG_WIDTH
OFF_GATE = OFF_VG + SG_WIDTH
D_FF_DENSE = 5504
N_EXPERTS = 8
D_FF_EXPERT = 7168
ALPHA = (2.0 * DEPTH) ** 0.25
LN_EPS = 1e-5
SB_UNDERFLOW = 110.0

LANES = 128
V7X_VMEM_BYTES = 64 * 1024 * 1024

PROJ_TM = 1024
PROJ_TN = 1024
ATT_TQ = 256
ATT_TK = 128
ATT_HEADS = 8
SGU_ROWS = 512
MERGE_TM = 256
FFN_TM = 1024
FFN_TF = 256
LN_ROWS = 256
ROUTER_TM = 512
DISPATCH_TM = 512
MOE_TM = 256
MOE_GRANULES = 4
MOE_TF = 256
COMBINE_TM = 256


def _params(semantics, vmem_bytes):
    assert vmem_bytes < V7X_VMEM_BYTES
    return pltpu.CompilerParams(dimension_semantics=semantics, vmem_limit_bytes=vmem_bytes)


def _dot(a, b):
    return jnp.dot(a, b, preferred_element_type=F32)


def _layer_norm(y, g, b):
    mu = jnp.mean(y, axis=-1, keepdims=True)
    d = y - mu
    var = jnp.mean(d * d, axis=-1, keepdims=True)
    return d * lax.rsqrt(var + LN_EPS) * g + b


def _gelu_tanh(x):
    return 0.5 * x * (1.0 + jnp.tanh(0.7978845608028654 * (x + 0.044715 * (x * x * x))))


def _inproj_kernel(x_ref, w_ref, *rest, mode):
    *rest, w16_ref = rest

    @pl.when(pl.program_id(1) == 0)
    def _():
        w16_ref[...] = w_ref[...].astype(BF16)

    acc = _dot(x_ref[...], w16_ref[...])
    if mode == "qkv":
        (o_ref,) = rest
        scale = jnp.where(pl.program_id(0) == 0, HEAD_DIM ** -0.5, 1.0).astype(F32)
        o_ref[...] = (acc * scale).astype(o_ref.dtype)
    elif mode == "gelu":
        (o_ref,) = rest
        o_ref[...] = _gelu_tanh(acc).astype(o_ref.dtype)
    elif mode == "gelu_ln":
        g_ref, b_ref, o_ref = rest
        act = _gelu_tanh(acc)
        for grp in range(acc.shape[1] // GROUP_DIM):
            cols = slice(grp * GROUP_DIM, (grp + 1) * GROUP_DIM)
            o_ref[:, cols] = _layer_norm(act[:, cols], g_ref[:, cols], b_ref[:, cols]).astype(o_ref.dtype)
    elif mode == "gate":
        b_ref, o_ref = rest
        o_ref[...] = jax.nn.sigmoid(acc + b_ref[...]).astype(o_ref.dtype)
    else:
        raise ValueError(mode)


def _inproj(x16, w_in, layer, col_off, width, mode, extra=()):
    s, d = x16.shape
    tm, tn = PROJ_TM, PROJ_TN
    n_blk = width // tn
    off_blk = col_off // tn
    in_specs = [
        pl.BlockSpec((tm, d), lambda n, m: (m, 0)),
        pl.BlockSpec((None, d, tn), lambda n, m: (layer, 0, off_blk + n)),
    ]
    for _ in extra:
        in_specs.append(pl.BlockSpec((1, tn), lambda n, m: (0, n)))
    vmem = 2 * (tm * d * 2 + d * tn * 4 + tm * tn * 2) + d * tn * 2 + 6 * tm * tn * 4
    return pl.pallas_call(
        functools.partial(_inproj_kernel, mode=mode),
        grid=(n_blk, s // tm),
        in_specs=in_specs,
        out_specs=pl.BlockSpec((tm, tn), lambda n, m: (m, n)),
        out_shape=jax.ShapeDtypeStruct((s, width), BF16),
        scratch_shapes=[pltpu.VMEM((d, tn), BF16)],
        compiler_params=_params(("arbitrary", "arbitrary"), vmem),
        name="inproj_" + mode,
    )(x16, w_in, *extra)


def _attn_kernel(q_ref, k_ref, v_ref, o_ref, z_ref, ls_ref, hl_ref, tail_ref, a_ref, acc_ref, c_ref):
    tq, t = ATT_TQ, ATT_TK
    tiles_per_q = tq // t
    i = pl.program_id(1)
    krow = lax.broadcasted_iota(jnp.int32, (t, t), 0)
    kcol = lax.broadcasted_iota(jnp.int32, (t, t), 1)
    later = jnp.where(krow > kcol, 1.0, 0.0).astype(BF16)
    later2 = jnp.concatenate([later, later], axis=0)
    row = lax.broadcasted_iota(jnp.int32, (tq, t), 0)
    col = lax.broadcasted_iota(jnp.int32, (tq, t), 1)
    heads = [slice(h * HEAD_DIM, (h + 1) * HEAD_DIM) for h in range(ATT_HEADS)]

    def key_tile(j, mask):
        keys = pl.ds(pl.multiple_of(j * t, t), t)
        for h, hs in enumerate(heads):
            z_ref[h] = lax.dot_general(q_ref[:, hs], k_ref[keys, hs], (((1,), (1,)), ((), ())),
                                       preferred_element_type=F32)
        for h in range(ATT_HEADS):
            z = z_ref[h]
            ls = jnp.minimum(z, 0.0) - jnp.log(1.0 + jnp.exp(-jnp.abs(z)))
            lk = ls - z
            if mask is not None:
                lk = jnp.where(mask, lk, 0.0)
            hi = lk.astype(BF16)
            ls_ref[h] = ls
            hl_ref[h, :, :t] = hi
            hl_ref[h, :, t:] = (lk - hi.astype(F32)).astype(BF16)
        for h in range(ATT_HEADS):
            tail_ref[h] = _dot(hl_ref[h], later2)
        for h, hs in enumerate(heads):
            tail = tail_ref[h]
            a = jnp.exp(ls_ref[h] + tail + c_ref[:, hs])
            lk0 = ls_ref[h][:, :1] - z_ref[h][:, :1]
            if mask is not None:
                a = jnp.where(mask, a, 0.0)
                lk0 = jnp.where(mask[:, :1], lk0, 0.0)
            a_ref[h] = a.astype(BF16)
            c_ref[:, hs] += jnp.broadcast_to(tail[:, :1] + lk0, (tq, HEAD_DIM))
        for h, hs in enumerate(heads):
            acc_ref[:, hs] += _dot(a_ref[h], v_ref[keys, hs])

    def any_weight_left():
        return jnp.max(c_ref[...]) > -SB_UNDERFLOW

    acc_ref[...] = jnp.zeros_like(acc_ref)
    c_ref[...] = jnp.zeros_like(c_ref)
    for back in range(tiles_per_q):
        offset = (tiles_per_q - 1 - back) * t
        key_tile(i * tiles_per_q + (tiles_per_q - 1 - back), col + offset < row)

    def cond(carry):
        j, go = carry
        return jnp.logical_and(j >= 0, go)

    def body(carry):
        j, _ = carry
        key_tile(j, None)
        return j - 1, any_weight_left()

    lax.while_loop(cond, body, (i * tiles_per_q - 1, any_weight_left()))
    o_ref[...] = acc_ref[...].astype(o_ref.dtype)


def _attention(qkv):
    s = qkv.shape[0]
    tq, t = ATT_TQ, ATT_TK
    nh = ATT_HEADS
    w = nh * HEAD_DIM
    groups = SB_WIDTH // w
    resident = pl.Buffered(1)
    vmem = 2 * s * w * 2 + 2 * (2 * tq * w * 2) + nh * 16 * tq * t + 2 * tq * w * 4 + (8 << 20)
    return pl.pallas_call(
        _attn_kernel,
        grid=(groups, s // tq),
        in_specs=[
            pl.BlockSpec((tq, w), lambda g, i: (i, g)),
            pl.BlockSpec((s, w), lambda g, i: (0, groups + g), pipeline_mode=resident),
            pl.BlockSpec((s, w), lambda g, i: (0, 2 * groups + g), pipeline_mode=resident),
        ],
        out_specs=pl.BlockSpec((tq, w), lambda g, i: (i, g)),
        out_shape=jax.ShapeDtypeStruct((s, SB_WIDTH), BF16),
        scratch_shapes=[
            pltpu.VMEM((nh, tq, t), F32),
            pltpu.VMEM((nh, tq, t), F32),
            pltpu.VMEM((nh, tq, 2 * t), BF16),
            pltpu.VMEM((nh, tq, t), F32),
            pltpu.VMEM((nh, tq, t), BF16),
            pltpu.VMEM((tq, w), F32),
            pltpu.VMEM((tq, w), F32),
        ],
        compiler_params=_params(("arbitrary", "arbitrary"), vmem),
        name="stickbreak_attn",
    )(qkv, qkv, qkv)


def _sgu_kernel(u_ref, v_ref, w_ref, b_ref, o_ref):
    c = CHUNK
    row = lax.broadcasted_iota(jnp.int32, (c, c), 0)
    col = lax.broadcasted_iota(jnp.int32, (c, c), 1)
    causal = col <= row
    for g in range(GROUPS):
        w = jnp.where(causal, w_ref[g], 0.0).astype(BF16)
        b = b_ref[:, g:g + 1]
        cols = slice(g * GROUP_DIM, (g + 1) * GROUP_DIM)
        for cc in range(SGU_ROWS // c):
            rows = slice(cc * c, (cc + 1) * c)
            mixed = _dot(w, v_ref[rows, cols]) + b
            o_ref[rows, cols] = (u_ref[rows, cols].astype(F32) * mixed).astype(o_ref.dtype)


def _sgu(u, vn, sg_w, sg_b_t):
    s = u.shape[0]
    r = SGU_ROWS
    vmem = 2 * 3 * r * SG_WIDTH * 2 + 2 * GROUPS * CHUNK * CHUNK * 4 + (4 << 20)
    return pl.pallas_call(
        _sgu_kernel,
        grid=(s // r,),
        in_specs=[
            pl.BlockSpec((r, SG_WIDTH), lambda i: (i, 0)),
            pl.BlockSpec((r, SG_WIDTH), lambda i: (i, 0)),
            pl.BlockSpec((GROUPS, CHUNK, CHUNK), lambda i: (0, 0, 0)),
            pl.BlockSpec((CHUNK, GROUPS), lambda i: (0, 0)),
        ],
        out_specs=pl.BlockSpec((r, SG_WIDTH), lambda i: (i, 0)),
        out_shape=jax.ShapeDtypeStruct((s, SG_WIDTH), BF16),
        compiler_params=_params(("arbitrary",), vmem),
        name="spatial_gating",
    )(u, vn, sg_w, sg_b_t)


def _merge_kernel(a_ref, b_ref, gate_ref, x_ref, wa_ref, wb_ref, wo_ref, g_ref, beta_ref,
                  o32_ref, o16_ref):
    d = D_MODEL
    ya = _dot(a_ref[...], wa_ref[...])
    yb = _dot(b_ref[...], wb_ref[...])
    merged = gate_ref[:, :d].astype(F32) * ya + gate_ref[:, d:].astype(F32) * yb
    mix = _dot(merged.astype(BF16), wo_ref[...])
    y = _layer_norm(ALPHA * x_ref[...] + mix, g_ref[...], beta_ref[...])
    o32_ref[...] = y
    o16_ref[...] = y.astype(BF16)


def _merge(att, sgu, gates, x32, wa, wb, wo, ln_g, ln_b):
    s, d = x32.shape
    tm = MERGE_TM
    const = lambda i: (0, 0)
    rowblk = lambda i: (i, 0)
    weights = (SB_WIDTH * d + SG_WIDTH * d + d * d) * 2
    vmem = 2 * weights + 2 * tm * (SB_WIDTH * 2 + SG_WIDTH * 2 + 2 * d * 2 + d * 4 + d * 4 + d * 2) \
        + 6 * tm * d * 4
    return pl.pallas_call(
        _merge_kernel,
        grid=(s // tm,),
        in_specs=[
            pl.BlockSpec((tm, SB_WIDTH), rowblk),
            pl.BlockSpec((tm, SG_WIDTH), rowblk),
            pl.BlockSpec((tm, 2 * d), rowblk),
            pl.BlockSpec((tm, d), rowblk),
            pl.BlockSpec((SB_WIDTH, d), const),
            pl.BlockSpec((SG_WIDTH, d), const),
            pl.BlockSpec((d, d), const),
            pl.BlockSpec((1, d), const),
            pl.BlockSpec((1, d), const),
        ],
        out_specs=[pl.BlockSpec((tm, d), rowblk), pl.BlockSpec((tm, d), rowblk)],
        out_shape=[jax.ShapeDtypeStruct((s, d), F32), jax.ShapeDtypeStruct((s, d), BF16)],
        compiler_params=_params(("arbitrary",), vmem),
        name="merge_outproj_ln",
    )(att, sgu, gates, x32, wa, wb, wo, ln_g, ln_b)


def _swiglu_hidden(x, w1_ref, w3_ref):
    h1 = _dot(x, w1_ref[...])
    return (h1 * jax.nn.sigmoid(h1) * _dot(x, w3_ref[...])).astype(BF16)


def _ffn_kernel(x16_ref, x32_ref, w1_ref, w3_ref, w2_ref, g_ref, b_ref, o32_ref, o16_ref):
    f = pl.program_id(1)

    @pl.when(f == 0)
    def _():
        o32_ref[...] = jnp.zeros_like(o32_ref)

    o32_ref[...] += _dot(_swiglu_hidden(x16_ref[...], w1_ref, w3_ref), w2_ref[...])

    @pl.when(f == pl.num_programs(1) - 1)
    def _():
        def ln_rows(c, carry):
            rows = pl.ds(pl.multiple_of(c * LN_ROWS, LN_ROWS), LN_ROWS)
            y = _layer_norm(ALPHA * x32_ref[rows, :] + o32_ref[rows, :], g_ref[...], b_ref[...])
            o32_ref[rows, :] = y
            o16_ref[rows, :] = y.astype(BF16)
            return carry
        lax.fori_loop(0, o32_ref.shape[0] // LN_ROWS, ln_rows, 0)


def _ffn(x16, x32, w1, w3, w2, ln_g, ln_b):
    s, d = x32.shape
    ff = w1.shape[1]
    tm, tf = FFN_TM, FFN_TF
    rowblk = lambda i, f: (i, 0)
    const = lambda i, f: (0, 0)
    vmem = tm * d * 4 + 2 * (tm * d * (2 + 4 + 2) + 3 * d * tf * 2) + 5 * tm * tf * 4 + 4 * LN_ROWS * d * 4
    return pl.pallas_call(
        _ffn_kernel,
        grid=(s // tm, ff // tf),
        in_specs=[
            pl.BlockSpec((tm, d), rowblk),
            pl.BlockSpec((tm, d), rowblk, pipeline_mode=pl.Buffered(1)),
            pl.BlockSpec((d, tf), lambda i, f: (0, f)),
            pl.BlockSpec((d, tf), lambda i, f: (0, f)),
            pl.BlockSpec((tf, d), lambda i, f: (f, 0)),
            pl.BlockSpec((1, d), const),
            pl.BlockSpec((1, d), const),
        ],
        out_specs=[pl.BlockSpec((tm, d), rowblk), pl.BlockSpec((tm, d), rowblk)],
        out_shape=[jax.ShapeDtypeStruct((s, d), F32), jax.ShapeDtypeStruct((s, d), BF16)],
        compiler_params=_params(("arbitrary", "arbitrary"), vmem),
        name="swiglu_ln",
    )(x16, x32, w1, w3, w2, ln_g, ln_b)


ROUTE_E, ROUTE_W, ROUTE_RANK = 0, 2, 4


def _router_kernel(x_ref, w_ref, route_ref, counts_ref, seen_ref):
    @pl.when(pl.program_id(0) == 0)
    def _():
        seen_ref[...] = jnp.zeros_like(seen_ref)

    logits = jnp.dot(x_ref[...], w_ref[...], preferred_element_type=F32,
                     precision=lax.Precision.HIGHEST)
    tm = logits.shape[0]
    lane = lax.broadcasted_iota(jnp.int32, logits.shape, 1).astype(F32)
    neg = jnp.float32(-jnp.inf)
    l1 = jnp.where(lane < N_EXPERTS, logits, neg)
    m1 = jnp.max(l1, axis=-1, keepdims=True)
    i1 = jnp.min(jnp.where(l1 == m1, lane, float(LANES)), axis=-1, keepdims=True)
    l2 = jnp.where(lane == i1, neg, l1)
    m2 = jnp.max(l2, axis=-1, keepdims=True)
    i2 = jnp.min(jnp.where(l2 == m2, lane, float(LANES)), axis=-1, keepdims=True)
    e2 = jnp.exp(m2 - m1)
    w_top = 1.0 / (1.0 + e2)

    chosen = jnp.where(lane == i1, 1.0, 0.0) + jnp.where(lane == i2, 1.0, 0.0)
    row = lax.broadcasted_iota(jnp.int32, (tm, tm), 0)
    col = lax.broadcasted_iota(jnp.int32, (tm, tm), 1)
    earlier = jnp.where(col < row, 1.0, 0.0).astype(BF16)
    prefix = _dot(earlier, chosen.astype(BF16)) + seen_ref[...]
    r1 = jnp.sum(jnp.where(lane == i1, prefix, 0.0), axis=-1, keepdims=True)
    r2 = jnp.sum(jnp.where(lane == i2, prefix, 0.0), axis=-1, keepdims=True)
    seen_ref[...] += jnp.sum(chosen, axis=0, keepdims=True)
    counts_ref[...] = seen_ref[...]

    fields = ((ROUTE_E, i1), (ROUTE_E + 1, i2), (ROUTE_W, w_top), (ROUTE_W + 1, e2 * w_top),
              (ROUTE_RANK, r1), (ROUTE_RANK + 1, r2))
    route = jnp.zeros_like(logits)
    for at, val in fields:
        route = jnp.where(lane == at, val, route)
    route_ref[...] = route


def _router(x32, w_router_padded):
    s, d = x32.shape
    tm = ROUTER_TM
    vmem = 2 * (tm * d * 4 + d * LANES * 4 + tm * LANES * 4) + 8 * tm * d * 4
    return pl.pallas_call(
        _router_kernel,
        grid=(s // tm,),
        in_specs=[pl.BlockSpec((tm, d), lambda i: (i, 0)), pl.BlockSpec((d, LANES), lambda i: (0, 0))],
        out_specs=[pl.BlockSpec((tm, LANES), lambda i: (i, 0)), pl.BlockSpec((1, LANES), lambda i: (0, 0))],
        out_shape=[jax.ShapeDtypeStruct((s, LANES), F32), jax.ShapeDtypeStruct((1, LANES), F32)],
        scratch_shapes=[pltpu.VMEM((1, LANES), F32)],
        compiler_params=_params(("arbitrary",), vmem),
        name="router_top2",
    )(x32, w_router_padded)


def _row_copy(src, src_row, dst, dst_row, sem):
    return pltpu.make_async_copy(src.at[pl.ds(src_row, 1)], dst.at[pl.ds(dst_row, 1)], sem)


def _dispatch_kernel(pos_ref, x_ref, xs_zero_hbm, xs_hbm, sem):
    del xs_zero_hbm
    tm = DISPATCH_TM

    def start(r, carry):
        for k in range(2):
            _row_copy(x_ref, r, xs_hbm, pos_ref[0, 0, k * tm + r], sem).start(priority=k)
        return carry

    def wait(r, carry):
        for k in range(2):
            _row_copy(x_ref, 0, xs_hbm, 0, sem).wait()
        return carry

    lax.fori_loop(0, tm, start, 0)
    lax.fori_loop(0, tm, wait, 0)


def _dispatch(pos_tiles, x32, n_rows):
    s, d = x32.shape
    tm = DISPATCH_TM
    return pl.pallas_call(
        _dispatch_kernel,
        grid=(s // tm,),
        in_specs=[
            pl.BlockSpec((1, 1, 2 * tm), lambda i: (i, 0, 0), memory_space=pltpu.SMEM),
            pl.BlockSpec((tm, d), lambda i: (i, 0)),
            pl.BlockSpec(memory_space=pl.ANY),
        ],
        out_specs=pl.BlockSpec(memory_space=pl.ANY),
        out_shape=jax.ShapeDtypeStruct((n_rows, d), F32),
        scratch_shapes=[pltpu.SemaphoreType.DMA(())],
        input_output_aliases={2: 0},
        compiler_params=_params(("arbitrary",), 2 * tm * d * 4 + (4 << 20)),
        name="moe_dispatch",
    )(pos_tiles, x32, jnp.zeros((n_rows, d), F32))


def _experts_kernel(tile_expert_ref, tile_rows_ref, n_used_ref, xs_ref, w1_ref, w3_ref, w2_ref, ys_ref,
                    x16_ref):
    del tile_expert_ref, n_used_ref
    f = pl.program_id(1)
    n_rows = tile_rows_ref[pl.program_id(0)]

    @pl.when(f == 0)
    def _():
        ys_ref[...] = jnp.zeros_like(ys_ref)

    def swiglu(rows):
        @pl.when(f == 0)
        def _():
            x16_ref[rows] = xs_ref[rows].astype(BF16)

        x = x16_ref[rows]
        h1 = _dot(x, w1_ref[...].astype(BF16))
        h = (h1 * jax.nn.sigmoid(h1) * _dot(x, w3_ref[...].astype(BF16))).astype(BF16)
        ys_ref[rows] += _dot(h, w2_ref[...].astype(BF16))

    for g in range(1, MOE_GRANULES + 1):
        @pl.when((n_rows > (g - 1) * MOE_TM) & (n_rows <= g * MOE_TM))
        def _():
            swiglu(slice(0, g * MOE_TM))


def _experts(tile_expert, tile_rows, n_used, xs, w1, w3, w2):
    n_rows, d = xs.shape
    ff = w1.shape[2]
    tm, tf = MOE_TM * MOE_GRANULES, MOE_TF
    n_f = ff // tf

    def rows(i, f, te, tr, nu):
        return (jnp.minimum(i, nu[0] - 1), 0)

    def chunk(i, f, nu):
        return jnp.where(i < nu[0], f, n_f - 1)

    vmem = tm * d * 4 + 2 * (tm * d * 4 + 3 * d * tf * 4) + tm * d * 2 + 3 * d * tf * 2 + 6 * tm * tf * 4
    return pl.pallas_call(
        _experts_kernel,
        grid_spec=pltpu.PrefetchScalarGridSpec(
            num_scalar_prefetch=3,
            grid=(n_rows // tm, n_f),
            in_specs=[
                pl.BlockSpec((tm, d), rows, pipeline_mode=pl.Buffered(1)),
                pl.BlockSpec((None, d, tf), lambda i, f, te, tr, nu: (te[i], 0, chunk(i, f, nu))),
                pl.BlockSpec((None, d, tf), lambda i, f, te, tr, nu: (te[i], 0, chunk(i, f, nu))),
                pl.BlockSpec((None, tf, d), lambda i, f, te, tr, nu: (te[i], chunk(i, f, nu), 0)),
            ],
            out_specs=pl.BlockSpec((tm, d), lambda i, f, te, tr, nu: (i, 0)),
            scratch_shapes=[pltpu.VMEM((tm, d), BF16)],
        ),
        out_shape=jax.ShapeDtypeStruct((n_rows, d), F32),
        compiler_params=_params(("arbitrary", "arbitrary"), vmem),
        name="moe_experts",
    )(tile_expert, tile_rows, n_used, xs, w1, w3, w2)


def _combine_kernel(pos_ref, pos_next_ref, route_ref, x_ref, ys_hbm, g_ref, b_ref, o_ref, buf, sems):
    tm = COMBINE_TM
    i = pl.program_id(0)
    slot = i % 2

    def start_tile(p_ref, slot):
        def start(r, carry):
            for k in range(2):
                _row_copy(ys_hbm, p_ref[0, 0, k * tm + r], buf.at[slot, k], r, sems.at[slot]).start(priority=k)
            return carry
        lax.fori_loop(0, tm, start, 0)

    @pl.when(i == 0)
    def _():
        start_tile(pos_ref, 0)

    @pl.when(i + 1 < pl.num_programs(0))
    def _():
        start_tile(pos_next_ref, 1 - slot)

    def wait(r, carry):
        for k in range(2):
            _row_copy(ys_hbm, 0, buf.at[slot, k], 0, sems.at[slot]).wait()
        return carry

    lax.fori_loop(0, tm, wait, 0)
    w0 = route_ref[:, ROUTE_W:ROUTE_W + 1]
    w1 = route_ref[:, ROUTE_W + 1:ROUTE_W + 2]
    y = w0 * buf[slot, 0] + w1 * buf[slot, 1]
    o_ref[...] = _layer_norm(ALPHA * x_ref[...] + y, g_ref[...], b_ref[...])


def _combine(pos_tiles, route, x32, ys, ln_g, ln_b):
    s, d = x32.shape
    tm = COMBINE_TM
    n = s // tm
    rowblk = lambda i: (i, 0)
    const = lambda i: (0, 0)
    vmem = 2 * 2 * tm * d * 4 + 2 * (2 * tm * d * 4 + tm * LANES * 4) + 6 * tm * d * 4
    return pl.pallas_call(
        _combine_kernel,
        grid=(n,),
        in_specs=[
            pl.BlockSpec((1, 1, 2 * tm), lambda i: (i, 0, 0), memory_space=pltpu.SMEM),
            pl.BlockSpec((1, 1, 2 * tm), lambda i: (jnp.minimum(i + 1, n - 1), 0, 0),
                         memory_space=pltpu.SMEM),
            pl.BlockSpec((tm, LANES), rowblk),
            pl.BlockSpec((tm, d), rowblk),
            pl.BlockSpec(memory_space=pl.ANY),
            pl.BlockSpec((1, d), const),
            pl.BlockSpec((1, d), const),
        ],
        out_specs=pl.BlockSpec((tm, d), rowblk),
        out_shape=jax.ShapeDtypeStruct((s, d), F32),
        scratch_shapes=[pltpu.VMEM((2, 2, tm, d), F32), pltpu.SemaphoreType.DMA((2,))],
        compiler_params=_params(("arbitrary",), vmem),
        name="moe_combine_ln",
    )(pos_tiles, pos_tiles, route, x32, ys, ln_g, ln_b)


def _pos_tiles(pos, tm):
    s = pos.shape[0]
    return pos.reshape(s // tm, tm, 2).transpose(0, 2, 1).reshape(s // tm, 1, 2 * tm)


def _moe(x32, w_router, w1, w3, w2, ln_g, ln_b):
    s, d = x32.shape
    tm = MOE_TM * MOE_GRANULES
    n_tiles = (2 * s) // tm + N_EXPERTS
    route, counts = _router(x32, jnp.pad(w_router, ((0, 0), (0, LANES - N_EXPERTS))))

    counts = counts[0, :N_EXPERTS].astype(jnp.int32)
    tiles_per_expert = (counts + tm - 1) // tm
    tile_end = jnp.cumsum(tiles_per_expert)
    tile_start = tile_end - tiles_per_expert
    n_used = tile_end[-1:]
    tile_ids = jnp.arange(n_tiles, dtype=jnp.int32)
    tile_expert = jnp.sum(jnp.minimum(tile_ids, n_used - 1)[:, None] >= tile_end[None, :],
                          axis=1).astype(jnp.int32)
    tile_rows = jnp.clip(counts[tile_expert] - (tile_ids - tile_start[tile_expert]) * tm, 0, tm)
    tile_rows = jnp.where(tile_ids < n_used, tile_rows, 0).astype(jnp.int32)
    experts = route[:, ROUTE_E:ROUTE_E + 2].astype(jnp.int32)
    pos = (tile_start * tm)[experts] + route[:, ROUTE_RANK:ROUTE_RANK + 2].astype(jnp.int32)

    xs = _dispatch(_pos_tiles(pos, DISPATCH_TM), x32, n_tiles * tm)
    ys = _experts(tile_expert, tile_rows, n_used, xs, w1, w3, w2)
    return _combine(_pos_tiles(pos, COMBINE_TM), route, x32, ys, ln_g, ln_b)


def _pad_ff(w, axis, mult):
    pad = (-w.shape[axis]) % mult
    if pad == 0:
        return w
    widths = [(0, 0)] * w.ndim
    widths[axis] = (0, pad)
    return jnp.pad(w, widths)


def kernel(x, w_in, b_gate, sg_w, sg_b, sg_ln_g, sg_ln_b, w_branch_a, w_branch_b, w_out,
           ln1_g, ln1_b, ffn_w1, ffn_w3, ffn_w2, moe_router, moe_w1, moe_w3, moe_w2,
           ln2_g, ln2_b):
    b, s, d = x.shape
    assert (b, s, d) == (1, SEQ, D_MODEL)
    x32 = x.reshape(s, d)
    x16 = x32.astype(BF16)
    for layer in range(DEPTH):
        qkv = _inproj(x16, w_in, layer, 0, 3 * SB_WIDTH, "qkv")
        u = _inproj(x16, w_in, layer, OFF_U, SG_WIDTH, "gelu")
        vn = _inproj(x16, w_in, layer, OFF_VG, SG_WIDTH, "gelu_ln",
                     (sg_ln_g[layer].reshape(1, -1), sg_ln_b[layer].reshape(1, -1)))
        gates = _inproj(x16, w_in, layer, OFF_GATE, 2 * d, "gate", (b_gate[layer].reshape(1, -1),))
        att = _attention(qkv)
        sgu = _sgu(u, vn, sg_w[layer], sg_b[layer].T)
        x32, x16 = _merge(att, sgu, gates, x32,
                          w_branch_a[layer].astype(BF16), w_branch_b[layer].astype(BF16),
                          w_out[layer].astype(BF16),
                          ln1_g[layer].reshape(1, d), ln1_b[layer].reshape(1, d))
        i = layer // 2
        g2, b2 = ln2_g[layer].reshape(1, d), ln2_b[layer].reshape(1, d)
        if layer % 2 == 0:
            w1 = _pad_ff(ffn_w1[i].astype(BF16), 1, FFN_TF)
            w3 = _pad_ff(ffn_w3[i].astype(BF16), 1, FFN_TF)
            w2 = _pad_ff(ffn_w2[i].astype(BF16), 0, FFN_TF)
            x32, x16 = _ffn(x16, x32, w1, w3, w2, g2, b2)
        else:
            x32 = _moe(x32, moe_router[i], moe_w1[i], moe_w3[i], moe_w2[i], g2, b2)
            x16 = x32.astype(BF16) if layer + 1 < DEPTH else None
    return x32.reshape(b, s, d)
```

```python
import functools

import jax
import jax.numpy as jnp
from jax import lax
from jax.experimental import pallas as pl
from jax.experimental.pallas import tpu as pltpu

F32 = jnp.float32
BF16 = jnp.bfloat16
U32 = jnp.uint32

D_MODEL = 2048
SEQ = 8192
DEPTH = 2
HEADS = 8
HEAD_DIM = 128
SB_WIDTH = HEADS * HEAD_DIM
GROUPS = 8
GROUP_DIM = 128
SG_WIDTH = GROUPS * GROUP_DIM
CHUNK = 128
OFF_U = 3 * SB_WIDTH
OFF_VG = OFF_U + SG_WIDTH
OFF_GATE = OFF_VG + SG_WIDTH
D_FF_DENSE = 5504
N_EXPERTS = 8
D_FF_EXPERT = 7168
ALPHA = (2.0 * DEPTH) ** 0.25
LN_EPS = 1e-5
SB_UNDERFLOW = 110.0

LANES = 128
V7X_VMEM_BYTES = 64 * 1024 * 1024

PROJ_TM = 1024
PROJ_TN = 1024
ATT_TQ = 256
ATT_TK = 128
ATT_HEADS = 8
SGU_ROWS = 512
MERGE_TM = 256
FFN_TM = 1024
FFN_TF = 256
LN_ROWS = 256
ROUTER_TM = 512
DISPATCH_TM = 512
MOE_TM = 256
MOE_GRANULES = 5
MOE_TF = 256
COMBINE_TM = 256


def _params(semantics, vmem_bytes):
    assert vmem_bytes < V7X_VMEM_BYTES
    return pltpu.CompilerParams(dimension_semantics=semantics, vmem_limit_bytes=vmem_bytes)


def _dot(a, b):
    return jnp.dot(a, b, preferred_element_type=F32)


def _layer_norm(y, g, b):
    mu = jnp.mean(y, axis=-1, keepdims=True)
    d = y - mu
    var = jnp.mean(d * d, axis=-1, keepdims=True)
    return d * lax.rsqrt(var + LN_EPS) * g + b


def _gelu_tanh(x):
    return 0.5 * x * (1.0 + jnp.tanh(0.7978845608028654 * (x + 0.044715 * (x * x * x))))


def _inproj_kernel(x_ref, w_ref, *rest, mode):
    *rest, w16_ref = rest

    @pl.when(pl.program_id(1) == 0)
    def _():
        w16_ref[...] = w_ref[...].astype(BF16)

    acc = _dot(x_ref[...], w16_ref[...])
    if mode == "qkv":
        (o_ref,) = rest
        scale = jnp.where(pl.program_id(0) == 0, HEAD_DIM ** -0.5, 1.0).astype(F32)
        o_ref[...] = (acc * scale).astype(o_ref.dtype)
    elif mode == "gelu":
        (o_ref,) = rest
        o_ref[...] = _gelu_tanh(acc).astype(o_ref.dtype)
    elif mode == "gelu_ln":
        g_ref, b_ref, o_ref = rest
        act = _gelu_tanh(acc)
        for grp in range(acc.shape[1] // GROUP_DIM):
            cols = slice(grp * GROUP_DIM, (grp + 1) * GROUP_DIM)
            o_ref[:, cols] = _layer_norm(act[:, cols], g_ref[:, cols], b_ref[:, cols]).astype(o_ref.dtype)
    elif mode == "gate":
        b_ref, o_ref = rest
        o_ref[...] = jax.nn.sigmoid(acc + b_ref[...]).astype(o_ref.dtype)
    else:
        raise ValueError(mode)


def _inproj(x16, w_in, layer, col_off, width, mode, extra=()):
    s, d = x16.shape
    tm, tn = PROJ_TM, PROJ_TN
    n_blk = width // tn
    off_blk = col_off // tn
    in_specs = [
        pl.BlockSpec((tm, d), lambda n, m: (m, 0)),
        pl.BlockSpec((None, d, tn), lambda n, m: (layer, 0, off_blk + n)),
    ]
    for _ in extra:
        in_specs.append(pl.BlockSpec((1, tn), lambda n, m: (0, n)))
    vmem = 2 * (tm * d * 2 + d * tn * 4 + tm * tn * 2) + d * tn * 2 + 6 * tm * tn * 4
    return pl.pallas_call(
        functools.partial(_inproj_kernel, mode=mode),
        grid=(n_blk, s // tm),
        in_specs=in_specs,
        out_specs=pl.BlockSpec((tm, tn), lambda n, m: (m, n)),
        out_shape=jax.ShapeDtypeStruct((s, width), BF16),
        scratch_shapes=[pltpu.VMEM((d, tn), BF16)],
        compiler_params=_params(("arbitrary", "arbitrary"), vmem),
        name="inproj_" + mode,
    )(x16, w_in, *extra)


def _attn_kernel(q_ref, k_ref, v_ref, o_ref, z_ref, ls_ref, hl_ref, tail_ref, a_ref, acc_ref, c_ref):
    tq, t = ATT_TQ, ATT_TK
    tiles_per_q = tq // t
    i = pl.program_id(1)
    krow = lax.broadcasted_iota(jnp.int32, (t, t), 0)
    kcol = lax.broadcasted_iota(jnp.int32, (t, t), 1)
    later = jnp.where(krow > kcol, 1.0, 0.0).astype(BF16)
    later2 = jnp.concatenate([later, later], axis=0)
    row = lax.broadcasted_iota(jnp.int32, (tq, t), 0)
    col = lax.broadcasted_iota(jnp.int32, (tq, t), 1)
    heads = [slice(h * HEAD_DIM, (h + 1) * HEAD_DIM) for h in range(ATT_HEADS)]

    def key_tile(j, mask):
        keys = pl.ds(pl.multiple_of(j * t, t), t)
        for h, hs in enumerate(heads):
            z_ref[h] = lax.dot_general(q_ref[:, hs], k_ref[keys, hs], (((1,), (1,)), ((), ())),
                                       preferred_element_type=F32)
        for h in range(ATT_HEADS):
            z = z_ref[h]
            ls = jnp.minimum(z, 0.0) - jnp.log(1.0 + jnp.exp(-jnp.abs(z)))
            lk = ls - z
            if mask is not None:
                lk = jnp.where(mask, lk, 0.0)
            hi = lk.astype(BF16)
            ls_ref[h] = ls
            hl_ref[h, :, :t] = hi
            hl_ref[h, :, t:] = (lk - hi.astype(F32)).astype(BF16)
        for h in range(ATT_HEADS):
            tail_ref[h] = _dot(hl_ref[h], later2)
        for h, hs in enumerate(heads):
            tail = tail_ref[h]
            a = jnp.exp(ls_ref[h] + tail + c_ref[:, hs])
            lk0 = ls_ref[h][:, :1] - z_ref[h][:, :1]
            if mask is not None:
                a = jnp.where(mask, a, 0.0)
                lk0 = jnp.where(mask[:, :1], lk0, 0.0)
            a_ref[h] = a.astype(BF16)
            c_ref[:, hs] += jnp.broadcast_to(tail[:, :1] + lk0, (tq, HEAD_DIM))
        for h, hs in enumerate(heads):
            acc_ref[:, hs] += _dot(a_ref[h], v_ref[keys, hs])

    def any_weight_left():
        return jnp.max(c_ref[...]) > -SB_UNDERFLOW

    acc_ref[...] = jnp.zeros_like(acc_ref)
    c_ref[...] = jnp.zeros_like(c_ref)
    for back in range(tiles_per_q):
        offset = (tiles_per_q - 1 - back) * t
        key_tile(i * tiles_per_q + (tiles_per_q - 1 - back), col + offset < row)

    def cond(carry):
        j, go = carry
        return jnp.logical_and(j >= 0, go)

    def body(carry):
        j, _ = carry
        key_tile(j, None)
        return j - 1, any_weight_left()

    lax.while_loop(cond, body, (i * tiles_per_q - 1, any_weight_left()))
    o_ref[...] = acc_ref[...].astype(o_ref.dtype)


def _attention(qkv):
    s = qkv.shape[0]
    tq, t = ATT_TQ, ATT_TK
    nh = ATT_HEADS
    w = nh * HEAD_DIM
    groups = SB_WIDTH // w
    resident = pl.Buffered(1)
    vmem = 2 * s * w * 2 + 2 * (2 * tq * w * 2) + nh * 16 * tq * t + 2 * tq * w * 4 + (8 << 20)
    return pl.pallas_call(
        _attn_kernel,
        grid=(groups, s // tq),
        in_specs=[
            pl.BlockSpec((tq, w), lambda g, i: (i, g)),
            pl.BlockSpec((s, w), lambda g, i: (0, groups + g), pipeline_mode=resident),
            pl.BlockSpec((s, w), lambda g, i: (0, 2 * groups + g), pipeline_mode=resident),
        ],
        out_specs=pl.BlockSpec((tq, w), lambda g, i: (i, g)),
        out_shape=jax.ShapeDtypeStruct((s, SB_WIDTH), BF16),
        scratch_shapes=[
            pltpu.VMEM((nh, tq, t), F32),
            pltpu.VMEM((nh, tq, t), F32),
            pltpu.VMEM((nh, tq, 2 * t), BF16),
            pltpu.VMEM((nh, tq, t), F32),
            pltpu.VMEM((nh, tq, t), BF16),
            pltpu.VMEM((tq, w), F32),
            pltpu.VMEM((tq, w), F32),
        ],
        compiler_params=_params(("arbitrary", "arbitrary"), vmem),
        name="stickbreak_attn",
    )(qkv, qkv, qkv)


def _sgu_kernel(u_ref, v_ref, w_ref, b_ref, o_ref):
    c = CHUNK
    row = lax.broadcasted_iota(jnp.int32, (c, c), 0)
    col = lax.broadcasted_iota(jnp.int32, (c, c), 1)
    causal = col <= row
    for g in range(GROUPS):
        w = jnp.where(causal, w_ref[g], 0.0).astype(BF16)
        b = b_ref[:, g:g + 1]
        cols = slice(g * GROUP_DIM, (g + 1) * GROUP_DIM)
        for cc in range(SGU_ROWS // c):
            rows = slice(cc * c, (cc + 1) * c)
            mixed = _dot(w, v_ref[rows, cols]) + b
            o_ref[rows, cols] = (u_ref[rows, cols].astype(F32) * mixed).astype(o_ref.dtype)


def _sgu(u, vn, sg_w, sg_b_t):
    s = u.shape[0]
    r = SGU_ROWS
    vmem = 2 * 3 * r * SG_WIDTH * 2 + 2 * GROUPS * CHUNK * CHUNK * 4 + (4 << 20)
    return pl.pallas_call(
        _sgu_kernel,
        grid=(s // r,),
        in_specs=[
            pl.BlockSpec((r, SG_WIDTH), lambda i: (i, 0)),
            pl.BlockSpec((r, SG_WIDTH), lambda i: (i, 0)),
            pl.BlockSpec((GROUPS, CHUNK, CHUNK), lambda i: (0, 0, 0)),
            pl.BlockSpec((CHUNK, GROUPS), lambda i: (0, 0)),
        ],
        out_specs=pl.BlockSpec((r, SG_WIDTH), lambda i: (i, 0)),
        out_shape=jax.ShapeDtypeStruct((s, SG_WIDTH), BF16),
        compiler_params=_params(("arbitrary",), vmem),
        name="spatial_gating",
    )(u, vn, sg_w, sg_b_t)


def _merge_kernel(a_ref, b_ref, gate_ref, x_ref, wa_ref, wb_ref, wo_ref, g_ref, beta_ref,
                  o32_ref, o16_ref):
    d = D_MODEL
    ya = _dot(a_ref[...], wa_ref[...])
    yb = _dot(b_ref[...], wb_ref[...])
    merged = gate_ref[:, :d].astype(F32) * ya + gate_ref[:, d:].astype(F32) * yb
    mix = _dot(merged.astype(BF16), wo_ref[...])
    y = _layer_norm(ALPHA * x_ref[...] + mix, g_ref[...], beta_ref[...])
    o32_ref[...] = y
    o16_ref[...] = y.astype(BF16)


def _merge(att, sgu, gates, x32, wa, wb, wo, ln_g, ln_b):
    s, d = x32.shape
    tm = MERGE_TM
    const = lambda i: (0, 0)
    rowblk = lambda i: (i, 0)
    weights = (SB_WIDTH * d + SG_WIDTH * d + d * d) * 2
    vmem = 2 * weights + 2 * tm * (SB_WIDTH * 2 + SG_WIDTH * 2 + 2 * d * 2 + d * 4 + d * 4 + d * 2) \
        + 6 * tm * d * 4
    return pl.pallas_call(
        _merge_kernel,
        grid=(s // tm,),
        in_specs=[
            pl.BlockSpec((tm, SB_WIDTH), rowblk),
            pl.BlockSpec((tm, SG_WIDTH), rowblk),
            pl.BlockSpec((tm, 2 * d), rowblk),
            pl.BlockSpec((tm, d), rowblk),
            pl.BlockSpec((SB_WIDTH, d), const),
            pl.BlockSpec((SG_WIDTH, d), const),
            pl.BlockSpec((d, d), const),
            pl.BlockSpec((1, d), const),
            pl.BlockSpec((1, d), const),
        ],
        out_specs=[pl.BlockSpec((tm, d), rowblk), pl.BlockSpec((tm, d), rowblk)],
        out_shape=[jax.ShapeDtypeStruct((s, d), F32), jax.ShapeDtypeStruct((s, d), BF16)],
        compiler_params=_params(("arbitrary",), vmem),
        name="merge_outproj_ln",
    )(att, sgu, gates, x32, wa, wb, wo, ln_g, ln_b)


def _swiglu_hidden(x, w1_ref, w3_ref):
    h1 = _dot(x, w1_ref[...])
    return (h1 * jax.nn.sigmoid(h1) * _dot(x, w3_ref[...])).astype(BF16)


def _ffn_kernel(x16_ref, x32_ref, w1_ref, w3_ref, w2_ref, g_ref, b_ref, o32_ref, o16_ref):
    f = pl.program_id(1)

    @pl.when(f == 0)
    def _():
        o32_ref[...] = jnp.zeros_like(o32_ref)

    o32_ref[...] += _dot(_swiglu_hidden(x16_ref[...], w1_ref, w3_ref), w2_ref[...])

    @pl.when(f == pl.num_programs(1) - 1)
    def _():
        def ln_rows(c, carry):
            rows = pl.ds(pl.multiple_of(c * LN_ROWS, LN_ROWS), LN_ROWS)
            y = _layer_norm(ALPHA * x32_ref[rows, :] + o32_ref[rows, :], g_ref[...], b_ref[...])
            o32_ref[rows, :] = y
            o16_ref[rows, :] = y.astype(BF16)
            return carry
        lax.fori_loop(0, o32_ref.shape[0] // LN_ROWS, ln_rows, 0)


def _ffn(x16, x32, w1, w3, w2, ln_g, ln_b):
    s, d = x32.shape
    ff = w1.shape[1]
    tm, tf = FFN_TM, FFN_TF
    rowblk = lambda i, f: (i, 0)
    const = lambda i, f: (0, 0)
    vmem = tm * d * 4 + 2 * (tm * d * (2 + 4 + 2) + 3 * d * tf * 2) + 5 * tm * tf * 4 + 4 * LN_ROWS * d * 4
    return pl.pallas_call(
        _ffn_kernel,
        grid=(s // tm, ff // tf),
        in_specs=[
            pl.BlockSpec((tm, d), rowblk),
            pl.BlockSpec((tm, d), rowblk, pipeline_mode=pl.Buffered(1)),
            pl.BlockSpec((d, tf), lambda i, f: (0, f)),
            pl.BlockSpec((d, tf), lambda i, f: (0, f)),
            pl.BlockSpec((tf, d), lambda i, f: (f, 0)),
            pl.BlockSpec((1, d), const),
            pl.BlockSpec((1, d), const),
        ],
        out_specs=[pl.BlockSpec((tm, d), rowblk), pl.BlockSpec((tm, d), rowblk)],
        out_shape=[jax.ShapeDtypeStruct((s, d), F32), jax.ShapeDtypeStruct((s, d), BF16)],
        compiler_params=_params(("arbitrary", "arbitrary"), vmem),
        name="swiglu_ln",
    )(x16, x32, w1, w3, w2, ln_g, ln_b)


ROUTE_E, ROUTE_W, ROUTE_RANK = 0, 2, 4


def _router_kernel(x_ref, w_ref, route_ref, counts_ref, seen_ref):
    @pl.when(pl.program_id(0) == 0)
    def _():
        seen_ref[...] = jnp.zeros_like(seen_ref)

    logits = jnp.dot(x_ref[...], w_ref[...], preferred_element_type=F32,
                     precision=lax.Precision.HIGHEST)
    tm = logits.shape[0]
    lane = lax.broadcasted_iota(jnp.int32, logits.shape, 1).astype(F32)
    neg = jnp.float32(-jnp.inf)
    l1 = jnp.where(lane < N_EXPERTS, logits, neg)
    m1 = jnp.max(l1, axis=-1, keepdims=True)
    i1 = jnp.min(jnp.where(l1 == m1, lane, float(LANES)), axis=-1, keepdims=True)
    l2 = jnp.where(lane == i1, neg, l1)
    m2 = jnp.max(l2, axis=-1, keepdims=True)
    i2 = jnp.min(jnp.where(l2 == m2, lane, float(LANES)), axis=-1, keepdims=True)
    e2 = jnp.exp(m2 - m1)
    w_top = 1.0 / (1.0 + e2)

    chosen = jnp.where(lane == i1, 1.0, 0.0) + jnp.where(lane == i2, 1.0, 0.0)
    row = lax.broadcasted_iota(jnp.int32, (tm, tm), 0)
    col = lax.broadcasted_iota(jnp.int32, (tm, tm), 1)
    earlier = jnp.where(col < row, 1.0, 0.0).astype(BF16)
    prefix = _dot(earlier, chosen.astype(BF16)) + seen_ref[...]
    r1 = jnp.sum(jnp.where(lane == i1, prefix, 0.0), axis=-1, keepdims=True)
    r2 = jnp.sum(jnp.where(lane == i2, prefix, 0.0), axis=-1, keepdims=True)
    seen_ref[...] += jnp.sum(chosen, axis=0, keepdims=True)
    counts_ref[...] = seen_ref[...]

    fields = ((ROUTE_E, i1), (ROUTE_E + 1, i2), (ROUTE_W, w_top), (ROUTE_W + 1, e2 * w_top),
              (ROUTE_RANK, r1), (ROUTE_RANK + 1, r2))
    route = jnp.zeros_like(logits)
    for at, val in fields:
        route = jnp.where(lane == at, val, route)
    route_ref[...] = route


def _router(x32, w_router_padded):
    s, d = x32.shape
    tm = ROUTER_TM
    vmem = 2 * (tm * d * 4 + d * LANES * 4 + tm * LANES * 4) + 8 * tm * d * 4
    return pl.pallas_call(
        _router_kernel,
        grid=(s // tm,),
        in_specs=[pl.BlockSpec((tm, d), lambda i: (i, 0)), pl.BlockSpec((d, LANES), lambda i: (0, 0))],
        out_specs=[pl.BlockSpec((tm, LANES), lambda i: (i, 0)), pl.BlockSpec((1, LANES), lambda i: (0, 0))],
        out_shape=[jax.ShapeDtypeStruct((s, LANES), F32), jax.ShapeDtypeStruct((1, LANES), F32)],
        scratch_shapes=[pltpu.VMEM((1, LANES), F32)],
        compiler_params=_params(("arbitrary",), vmem),
        name="router_top2",
    )(x32, w_router_padded)


def _row_copy(src, src_row, dst, dst_row, sem):
    return pltpu.make_async_copy(src.at[pl.ds(src_row, 1)], dst.at[pl.ds(dst_row, 1)], sem)


def _pack_bf16_pairs(x):
    n = x.shape[1] // 2
    lo = lax.bitcast_convert_type(x[:, :n].astype(BF16).astype(F32), U32)
    hi = lax.bitcast_convert_type(x[:, n:].astype(BF16).astype(F32), U32)
    return (lo >> 16) | (hi & jnp.uint32(0xFFFF0000))


def _unpack_bf16_pairs(w):
    lo = lax.bitcast_convert_type(w << 16, F32).astype(BF16)
    hi = lax.bitcast_convert_type(w & jnp.uint32(0xFFFF0000), F32).astype(BF16)
    return lo, hi


def _dispatch_kernel(pos_ref, x_ref, xs_zero_hbm, xs_hbm, packed_ref, sem):
    del xs_zero_hbm
    tm = DISPATCH_TM
    packed_ref[...] = _pack_bf16_pairs(x_ref[...])

    def start(r, carry):
        for k in range(2):
            _row_copy(packed_ref, r, xs_hbm, pos_ref[0, 0, k * tm + r], sem).start(priority=k)
        return carry

    def wait(r, carry):
        for k in range(2):
            _row_copy(packed_ref, 0, xs_hbm, 0, sem).wait()
        return carry

    lax.fori_loop(0, tm, start, 0)
    lax.fori_loop(0, tm, wait, 0)


def _dispatch(pos_tiles, x32, n_rows):
    s, d = x32.shape
    tm = DISPATCH_TM
    return pl.pallas_call(
        _dispatch_kernel,
        grid=(s // tm,),
        in_specs=[
            pl.BlockSpec((1, 1, 2 * tm), lambda i: (i, 0, 0), memory_space=pltpu.SMEM),
            pl.BlockSpec((tm, d), lambda i: (i, 0)),
            pl.BlockSpec(memory_space=pl.ANY),
        ],
        out_specs=pl.BlockSpec(memory_space=pl.ANY),
        out_shape=jax.ShapeDtypeStruct((n_rows, d // 2), U32),
        scratch_shapes=[pltpu.VMEM((tm, d // 2), U32), pltpu.SemaphoreType.DMA(())],
        input_output_aliases={2: 0},
        compiler_params=_params(("arbitrary",), 2 * tm * d * 4 + 4 * tm * d * 2 + (4 << 20)),
        name="moe_dispatch",
    )(pos_tiles, x32, jnp.zeros((n_rows, d // 2), U32))


def _experts_kernel(tile_expert_ref, tile_rows_ref, n_used_ref, xs_ref, w1_ref, w3_ref, w2_ref, ys_ref,
                    x16_ref):
    del tile_expert_ref, n_used_ref
    f = pl.program_id(1)
    n_rows = tile_rows_ref[pl.program_id(0)]

    @pl.when(f == 0)
    def _():
        ys_ref[...] = jnp.zeros_like(ys_ref)

    def swiglu(rows):
        @pl.when(f == 0)
        def _():
            half = x16_ref.shape[1] // 2
            x16_ref[rows, :half], x16_ref[rows, half:] = _unpack_bf16_pairs(xs_ref[rows])

        x = x16_ref[rows]
        h1 = _dot(x, w1_ref[...].astype(BF16))
        h = (h1 * jax.nn.sigmoid(h1) * _dot(x, w3_ref[...].astype(BF16))).astype(BF16)
        ys_ref[rows] += _dot(h, w2_ref[...].astype(BF16))

    for g in range(1, MOE_GRANULES + 1):
        @pl.when((n_rows > (g - 1) * MOE_TM) & (n_rows <= g * MOE_TM))
        def _():
            swiglu(slice(0, g * MOE_TM))


def _experts(tile_expert, tile_rows, n_used, xs, w1, w3, w2):
    n_rows, d = xs.shape[0], w1.shape[1]
    ff = w1.shape[2]
    tm, tf = MOE_TM * MOE_GRANULES, MOE_TF
    n_f = ff // tf

    def rows(i, f, te, tr, nu):
        return (jnp.minimum(i, nu[0] - 1), 0)

    def chunk(i, f, nu):
        return jnp.where(i < nu[0], f, n_f - 1)

    vmem = tm * d * 2 + 2 * (tm * d * 4 + 3 * d * tf * 4) + tm * d * 2 + 3 * d * tf * 2 + 6 * tm * tf * 4
    return pl.pallas_call(
        _experts_kernel,
        grid_spec=pltpu.PrefetchScalarGridSpec(
            num_scalar_prefetch=3,
            grid=(n_rows // tm, n_f),
            in_specs=[
                pl.BlockSpec((tm, d // 2), rows, pipeline_mode=pl.Buffered(1)),
                pl.BlockSpec((None, d, tf), lambda i, f, te, tr, nu: (te[i], 0, chunk(i, f, nu))),
                pl.BlockSpec((None, d, tf), lambda i, f, te, tr, nu: (te[i], 0, chunk(i, f, nu))),
                pl.BlockSpec((None, tf, d), lambda i, f, te, tr, nu: (te[i], chunk(i, f, nu), 0)),
            ],
            out_specs=pl.BlockSpec((tm, d), lambda i, f, te, tr, nu: (i, 0)),
            scratch_shapes=[pltpu.VMEM((tm, d), BF16)],
        ),
        out_shape=jax.ShapeDtypeStruct((n_rows, d), F32),
        compiler_params=_params(("arbitrary", "arbitrary"), vmem),
        name="moe_experts",
    )(tile_expert, tile_rows, n_used, xs, w1, w3, w2)


def _combine_kernel(pos_ref, pos_next_ref, route_ref, x_ref, ys_hbm, g_ref, b_ref, o_ref, buf, sems):
    tm = COMBINE_TM
    i = pl.program_id(0)
    slot = i % 2

    def start_tile(p_ref, slot):
        def start(r, carry):
            for k in range(2):
                _row_copy(ys_hbm, p_ref[0, 0, k * tm + r], buf.at[slot, k], r, sems.at[slot]).start(priority=k)
            return carry
        lax.fori_loop(0, tm, start, 0)

    @pl.when(i == 0)
    def _():
        start_tile(pos_ref, 0)

    @pl.when(i + 1 < pl.num_programs(0))
    def _():
        start_tile(pos_next_ref, 1 - slot)

    def wait(r, carry):
        for k in range(2):
            _row_copy(ys_hbm, 0, buf.at[slot, k], 0, sems.at[slot]).wait()
        return carry

    lax.fori_loop(0, tm, wait, 0)
    w0 = route_ref[:, ROUTE_W:ROUTE_W + 1]
    w1 = route_ref[:, ROUTE_W + 1:ROUTE_W + 2]
    y = w0 * buf[slot, 0] + w1 * buf[slot, 1]
    o_ref[...] = _layer_norm(ALPHA * x_ref[...] + y, g_ref[...], b_ref[...])


def _combine(pos_tiles, route, x32, ys, ln_g, ln_b):
    s, d = x32.shape
    tm = COMBINE_TM
    n = s // tm
    rowblk = lambda i: (i, 0)
    const = lambda i: (0, 0)
    vmem = 2 * 2 * tm * d * 4 + 2 * (2 * tm * d * 4 + tm * LANES * 4) + 6 * tm * d * 4
    return pl.pallas_call(
        _combine_kernel,
        grid=(n,),
        in_specs=[
            pl.BlockSpec((1, 1, 2 * tm), lambda i: (i, 0, 0), memory_space=pltpu.SMEM),
            pl.BlockSpec((1, 1, 2 * tm), lambda i: (jnp.minimum(i + 1, n - 1), 0, 0),
                         memory_space=pltpu.SMEM),
            pl.BlockSpec((tm, LANES), rowblk),
            pl.BlockSpec((tm, d), rowblk),
            pl.BlockSpec(memory_space=pl.ANY),
            pl.BlockSpec((1, d), const),
            pl.BlockSpec((1, d), const),
        ],
        out_specs=pl.BlockSpec((tm, d), rowblk),
        out_shape=jax.ShapeDtypeStruct((s, d), F32),
        scratch_shapes=[pltpu.VMEM((2, 2, tm, d), F32), pltpu.SemaphoreType.DMA((2,))],
        compiler_params=_params(("arbitrary",), vmem),
        name="moe_combine_ln",
    )(pos_tiles, pos_tiles, route, x32, ys, ln_g, ln_b)


def _pos_tiles(pos, tm):
    s = pos.shape[0]
    return pos.reshape(s // tm, tm, 2).transpose(0, 2, 1).reshape(s // tm, 1, 2 * tm)


def _moe(x32, w_router, w1, w3, w2, ln_g, ln_b):
    s, d = x32.shape
    tm = MOE_TM * MOE_GRANULES
    n_tiles = (2 * s) // tm + N_EXPERTS
    route, counts = _router(x32, jnp.pad(w_router, ((0, 0), (0, LANES - N_EXPERTS))))

    counts = counts[0, :N_EXPERTS].astype(jnp.int32)
    tiles_per_expert = (counts + tm - 1) // tm
    tile_end = jnp.cumsum(tiles_per_expert)
    tile_start = tile_end - tiles_per_expert
    n_used = tile_end[-1:]
    tile_ids = jnp.arange(n_tiles, dtype=jnp.int32)
    tile_expert = jnp.sum(jnp.minimum(tile_ids, n_used - 1)[:, None] >= tile_end[None, :],
                          axis=1).astype(jnp.int32)
    tile_rows = jnp.clip(counts[tile_expert] - (tile_ids - tile_start[tile_expert]) * tm, 0, tm)
    tile_rows = jnp.where(tile_ids < n_used, tile_rows, 0).astype(jnp.int32)
    experts = route[:, ROUTE_E:ROUTE_E + 2].astype(jnp.int32)
    pos = (tile_start * tm)[experts] + route[:, ROUTE_RANK:ROUTE_RANK + 2].astype(jnp.int32)

    xs = _dispatch(_pos_tiles(pos, DISPATCH_TM), x32, n_tiles * tm)
    ys = _experts(tile_expert, tile_rows, n_used, xs, w1, w3, w2)
    return _combine(_pos_tiles(pos, COMBINE_TM), route, x32, ys, ln_g, ln_b)


def _pad_ff(w, axis, mult):
    pad = (-w.shape[axis]) % mult
    if pad == 0:
        return w
    widths = [(0, 0)] * w.ndim
    widths[axis] = (0, pad)
    return jnp.pad(w, widths)


def kernel(x, w_in, b_gate, sg_w, sg_b, sg_ln_g, sg_ln_b, w_branch_a, w_branch_b, w_out,
           ln1_g, ln1_b, ffn_w1, ffn_w3, ffn_w2, moe_router, moe_w1, moe_w3, moe_w2,
           ln2_g, ln2_b):
    b, s, d = x.shape
    assert (b, s, d) == (1, SEQ, D_MODEL)
    x32 = x.reshape(s, d)
    x16 = x32.astype(BF16)
    for layer in range(DEPTH):
        qkv = _inproj(x16, w_in, layer, 0, 3 * SB_WIDTH, "qkv")
        u = _inproj(x16, w_in, layer, OFF_U, SG_WIDTH, "gelu")
        vn = _inproj(x16, w_in, layer, OFF_VG, SG_WIDTH, "gelu_ln",
                     (sg_ln_g[layer].reshape(1, -1), sg_ln_b[layer].reshape(1, -1)))
        gates = _inproj(x16, w_in, layer, OFF_GATE, 2 * d, "gate", (b_gate[layer].reshape(1, -1),))
        att = _attention(qkv)
        sgu = _sgu(u, vn, sg_w[layer], sg_b[layer].T)
        x32, x16 = _merge(att, sgu, gates, x32,
                          w_branch_a[layer].astype(BF16), w_branch_b[layer].astype(BF16),
                          w_out[layer].astype(BF16),
                          ln1_g[layer].reshape(1, d), ln1_b[layer].reshape(1, d))
        i = layer // 2
        g2, b2 = ln2_g[layer].reshape(1, d), ln2_b[layer].reshape(1, d)
        if layer % 2 == 0:
            w1 = _pad_ff(ffn_w1[i].astype(BF16), 1, FFN_TF)
            w3 = _pad_ff(ffn_w3[i].astype(BF16), 1, FFN_TF)
            w2 = _pad_ff(ffn_w2[i].astype(BF16), 0, FFN_TF)
            x32, x16 = _ffn(x16, x32, w1, w3, w2, g2, b2)
        else:
            x32 = _moe(x32, moe_router[i], moe_w1[i], moe_w3[i], moe_w2[i], g2, b2)
            x16 = x32.astype(BF16) if layer + 1 < DEPTH else None
    return x32.reshape(b, s, d)
```

```python
import functools

import jax
import jax.numpy as jnp
from jax import lax
from jax.experimental import pallas as pl
from jax.experimental.pallas import tpu as pltpu

F32 = jnp.float32
BF16 = jnp.bfloat16
U32 = jnp.uint32

D_MODEL = 2048
SEQ = 8192
DEPTH = 2
HEADS = 8
HEAD_DIM = 128
SB_WIDTH = HEADS * HEAD_DIM
GROUPS = 8
GROUP_DIM = 128
SG_WIDTH = GROUPS * GROUP_DIM
CHUNK = 128
OFF_U = 3 * SB_WIDTH
OFF_VG = OFF_U + SG_WIDTH
OFF_GATE = OFF_VG + SG_WIDTH
D_FF_DENSE = 5504
N_EXPERTS = 8
D_FF_EXPERT = 7168
ALPHA = (2.0 * DEPTH) ** 0.25
LN_EPS = 1e-5
SB_UNDERFLOW = 110.0

LANES = 128
V7X_VMEM_BYTES = 64 * 1024 * 1024

PROJ_TM = 1024
PROJ_TN = 1024
ATT_TQ = 256
ATT_TK = 256
ATT_HEADS = 8
SGU_ROWS = 512
MERGE_TM = 256
FFN_TM = 1024
FFN_TF = 256
LN_ROWS = 256
ROUTER_TM = 512
DISPATCH_TM = 512
MOE_TM = 256
MOE_GRANULES = 5
MOE_TF = 256
COMBINE_TM = 256


def _params(semantics, vmem_bytes):
    assert vmem_bytes < V7X_VMEM_BYTES
    return pltpu.CompilerParams(dimension_semantics=semantics, vmem_limit_bytes=vmem_bytes)


def _dot(a, b):
    return jnp.dot(a, b, preferred_element_type=F32)


def _layer_norm(y, g, b):
    mu = jnp.mean(y, axis=-1, keepdims=True)
    d = y - mu
    var = jnp.mean(d * d, axis=-1, keepdims=True)
    return d * lax.rsqrt(var + LN_EPS) * g + b


def _gelu_tanh(x):
    return 0.5 * x * (1.0 + jnp.tanh(0.7978845608028654 * (x + 0.044715 * (x * x * x))))


def _inproj_kernel(x_ref, w_ref, *rest, mode):
    *rest, w16_ref = rest

    @pl.when(pl.program_id(1) == 0)
    def _():
        w16_ref[...] = w_ref[...].astype(BF16)

    acc = _dot(x_ref[...], w16_ref[...])
    if mode == "qkv":
        (o_ref,) = rest
        scale = jnp.where(pl.program_id(0) == 0, HEAD_DIM ** -0.5, 1.0).astype(F32)
        o_ref[...] = (acc * scale).astype(o_ref.dtype)
    elif mode == "gelu":
        (o_ref,) = rest
        o_ref[...] = _gelu_tanh(acc).astype(o_ref.dtype)
    elif mode == "gelu_ln":
        g_ref, b_ref, o_ref = rest
        act = _gelu_tanh(acc)
        for grp in range(acc.shape[1] // GROUP_DIM):
            cols = slice(grp * GROUP_DIM, (grp + 1) * GROUP_DIM)
            o_ref[:, cols] = _layer_norm(act[:, cols], g_ref[:, cols], b_ref[:, cols]).astype(o_ref.dtype)
    elif mode == "gate":
        b_ref, o_ref = rest
        o_ref[...] = jax.nn.sigmoid(acc + b_ref[...]).astype(o_ref.dtype)
    else:
        raise ValueError(mode)


def _inproj(x16, w_in, layer, col_off, width, mode, extra=()):
    s, d = x16.shape
    tm, tn = PROJ_TM, PROJ_TN
    n_blk = width // tn
    off_blk = col_off // tn
    in_specs = [
        pl.BlockSpec((tm, d), lambda n, m: (m, 0)),
        pl.BlockSpec((None, d, tn), lambda n, m: (layer, 0, off_blk + n)),
    ]
    for _ in extra:
        in_specs.append(pl.BlockSpec((1, tn), lambda n, m: (0, n)))
    vmem = 2 * (tm * d * 2 + d * tn * 4 + tm * tn * 2) + d * tn * 2 + 6 * tm * tn * 4
    return pl.pallas_call(
        functools.partial(_inproj_kernel, mode=mode),
        grid=(n_blk, s // tm),
        in_specs=in_specs,
        out_specs=pl.BlockSpec((tm, tn), lambda n, m: (m, n)),
        out_shape=jax.ShapeDtypeStruct((s, width), BF16),
        scratch_shapes=[pltpu.VMEM((d, tn), BF16)],
        compiler_params=_params(("arbitrary", "arbitrary"), vmem),
        name="inproj_" + mode,
    )(x16, w_in, *extra)


def _attn_kernel(q_ref, k_ref, v_ref, o_ref, z_ref, ls_ref, hl_ref, tail_ref, a_ref, acc_ref, c_ref):
    tq, t = ATT_TQ, ATT_TK
    tiles_per_q = tq // t
    i = pl.program_id(1)
    krow = lax.broadcasted_iota(jnp.int32, (t, t), 0)
    kcol = lax.broadcasted_iota(jnp.int32, (t, t), 1)
    later = jnp.where(krow > kcol, 1.0, 0.0).astype(BF16)
    later2 = jnp.concatenate([later, later], axis=0)
    row = lax.broadcasted_iota(jnp.int32, (tq, t), 0)
    col = lax.broadcasted_iota(jnp.int32, (tq, t), 1)
    heads = [slice(h * HEAD_DIM, (h + 1) * HEAD_DIM) for h in range(ATT_HEADS)]

    def key_tile(j, mask):
        keys = pl.ds(pl.multiple_of(j * t, t), t)
        for h, hs in enumerate(heads):
            z_ref[h] = lax.dot_general(q_ref[:, hs], k_ref[keys, hs], (((1,), (1,)), ((), ())),
                                       preferred_element_type=F32)
        for h in range(ATT_HEADS):
            z = z_ref[h]
            ls = jnp.minimum(z, 0.0) - jnp.log(1.0 + jnp.exp(-jnp.abs(z)))
            lk = ls - z
            if mask is not None:
                lk = jnp.where(mask, lk, 0.0)
            hi = lk.astype(BF16)
            ls_ref[h] = ls
            hl_ref[h, :, :t] = hi
            hl_ref[h, :, t:] = (lk - hi.astype(F32)).astype(BF16)
        for h in range(ATT_HEADS):
            tail_ref[h] = _dot(hl_ref[h], later2)
        for h, hs in enumerate(heads):
            tail = tail_ref[h]
            a = jnp.exp(ls_ref[h] + tail + jnp.tile(c_ref[:, hs], (1, t // HEAD_DIM)))
            lk0 = ls_ref[h][:, :1] - z_ref[h][:, :1]
            if mask is not None:
                a = jnp.where(mask, a, 0.0)
                lk0 = jnp.where(mask[:, :1], lk0, 0.0)
            a_ref[h] = a.astype(BF16)
            c_ref[:, hs] += jnp.broadcast_to(tail[:, :1] + lk0, (tq, HEAD_DIM))
        for h, hs in enumerate(heads):
            acc_ref[:, hs] += _dot(a_ref[h], v_ref[keys, hs])

    def any_weight_left():
        return jnp.max(c_ref[...]) > -SB_UNDERFLOW

    acc_ref[...] = jnp.zeros_like(acc_ref)
    c_ref[...] = jnp.zeros_like(c_ref)
    for back in range(tiles_per_q):
        offset = (tiles_per_q - 1 - back) * t
        key_tile(i * tiles_per_q + (tiles_per_q - 1 - back), col + offset < row)

    def cond(carry):
        j, go = carry
        return jnp.logical_and(j >= 0, go)

    def body(carry):
        j, _ = carry
        key_tile(j, None)
        return j - 1, any_weight_left()

    lax.while_loop(cond, body, (i * tiles_per_q - 1, any_weight_left()))
    o_ref[...] = acc_ref[...].astype(o_ref.dtype)


def _attention(qkv):
    s = qkv.shape[0]
    tq, t = ATT_TQ, ATT_TK
    nh = ATT_HEADS
    w = nh * HEAD_DIM
    groups = SB_WIDTH // w
    resident = pl.Buffered(1)
    vmem = 2 * s * w * 2 + 2 * (2 * tq * w * 2) + nh * 16 * tq * t + 2 * tq * w * 4 + (8 << 20)
    return pl.pallas_call(
        _attn_kernel,
        grid=(groups, s // tq),
        in_specs=[
            pl.BlockSpec((tq, w), lambda g, i: (i, g)),
            pl.BlockSpec((s, w), lambda g, i: (0, groups + g), pipeline_mode=resident),
            pl.BlockSpec((s, w), lambda g, i: (0, 2 * groups + g), pipeline_mode=resident),
        ],
        out_specs=pl.BlockSpec((tq, w), lambda g, i: (i, g)),
        out_shape=jax.ShapeDtypeStruct((s, SB_WIDTH), BF16),
        scratch_shapes=[
            pltpu.VMEM((nh, tq, t), F32),
            pltpu.VMEM((nh, tq, t), F32),
            pltpu.VMEM((nh, tq, 2 * t), BF16),
            pltpu.VMEM((nh, tq, t), F32),
            pltpu.VMEM((nh, tq, t), BF16),
            pltpu.VMEM((tq, w), F32),
            pltpu.VMEM((tq, w), F32),
        ],
        compiler_params=_params(("arbitrary", "arbitrary"), vmem),
        name="stickbreak_attn",
    )(qkv, qkv, qkv)


def _sgu_kernel(u_ref, v_ref, w_ref, b_ref, o_ref):
    c = CHUNK
    row = lax.broadcasted_iota(jnp.int32, (c, c), 0)
    col = lax.broadcasted_iota(jnp.int32, (c, c), 1)
    causal = col <= row
    for g in range(GROUPS):
        w = jnp.where(causal, w_ref[g], 0.0).astype(BF16)
        b = b_ref[:, g:g + 1]
        cols = slice(g * GROUP_DIM, (g + 1) * GROUP_DIM)
        for cc in range(SGU_ROWS // c):
            rows = slice(cc * c, (cc + 1) * c)
            mixed = _dot(w, v_ref[rows, cols]) + b
            o_ref[rows, cols] = (u_ref[rows, cols].astype(F32) * mixed).astype(o_ref.dtype)


def _sgu(u, vn, sg_w, sg_b_t):
    s = u.shape[0]
    r = SGU_ROWS
    vmem = 2 * 3 * r * SG_WIDTH * 2 + 2 * GROUPS * CHUNK * CHUNK * 4 + (4 << 20)
    return pl.pallas_call(
        _sgu_kernel,
        grid=(s // r,),
        in_specs=[
            pl.BlockSpec((r, SG_WIDTH), lambda i: (i, 0)),
            pl.BlockSpec((r, SG_WIDTH), lambda i: (i, 0)),
            pl.BlockSpec((GROUPS, CHUNK, CHUNK), lambda i: (0, 0, 0)),
            pl.BlockSpec((CHUNK, GROUPS), lambda i: (0, 0)),
        ],
        out_specs=pl.BlockSpec((r, SG_WIDTH), lambda i: (i, 0)),
        out_shape=jax.ShapeDtypeStruct((s, SG_WIDTH), BF16),
        compiler_params=_params(("arbitrary",), vmem),
        name="spatial_gating",
    )(u, vn, sg_w, sg_b_t)


def _merge_kernel(a_ref, b_ref, gate_ref, x_ref, wa_ref, wb_ref, wo_ref, g_ref, beta_ref,
                  o32_ref, o16_ref):
    d = D_MODEL
    ya = _dot(a_ref[...], wa_ref[...])
    yb = _dot(b_ref[...], wb_ref[...])
    merged = gate_ref[:, :d].astype(F32) * ya + gate_ref[:, d:].astype(F32) * yb
    mix = _dot(merged.astype(BF16), wo_ref[...])
    y = _layer_norm(ALPHA * x_ref[...] + mix, g_ref[...], beta_ref[...])
    o32_ref[...] = y
    o16_ref[...] = y.astype(BF16)


def _merge(att, sgu, gates, x32, wa, wb, wo, ln_g, ln_b):
    s, d = x32.shape
    tm = MERGE_TM
    const = lambda i: (0, 0)
    rowblk = lambda i: (i, 0)
    weights = (SB_WIDTH * d + SG_WIDTH * d + d * d) * 2
    vmem = 2 * weights + 2 * tm * (SB_WIDTH * 2 + SG_WIDTH * 2 + 2 * d * 2 + d * 4 + d * 4 + d * 2) \
        + 6 * tm * d * 4
    return pl.pallas_call(
        _merge_kernel,
        grid=(s // tm,),
        in_specs=[
            pl.BlockSpec((tm, SB_WIDTH), rowblk),
            pl.BlockSpec((tm, SG_WIDTH), rowblk),
            pl.BlockSpec((tm, 2 * d), rowblk),
            pl.BlockSpec((tm, d), rowblk),
            pl.BlockSpec((SB_WIDTH, d), const),
            pl.BlockSpec((SG_WIDTH, d), const),
            pl.BlockSpec((d, d), const),
            pl.BlockSpec((1, d), const),
            pl.BlockSpec((1, d), const),
        ],
        out_specs=[pl.BlockSpec((tm, d), rowblk), pl.BlockSpec((tm, d), rowblk)],
        out_shape=[jax.ShapeDtypeStruct((s, d), F32), jax.ShapeDtypeStruct((s, d), BF16)],
        compiler_params=_params(("arbitrary",), vmem),
        name="merge_outproj_ln",
    )(att, sgu, gates, x32, wa, wb, wo, ln_g, ln_b)


def _swiglu_hidden(x, w1_ref, w3_ref):
    h1 = _dot(x, w1_ref[...])
    return (h1 * jax.nn.sigmoid(h1) * _dot(x, w3_ref[...])).astype(BF16)


def _ffn_kernel(x16_ref, x32_ref, w1_ref, w3_ref, w2_ref, w1t_ref, w3t_ref, w2t_ref, g_ref, b_ref,
                o32_ref, o16_ref):
    f = pl.program_id(1)

    @pl.when(f == 0)
    def _():
        o32_ref[...] = jnp.zeros_like(o32_ref)

    o32_ref[...] += _dot(_swiglu_hidden(x16_ref[...], w1_ref, w3_ref), w2_ref[...])

    @pl.when(f == pl.num_programs(1) - 1)
    def _():
        o32_ref[...] += _dot(_swiglu_hidden(x16_ref[...], w1t_ref, w3t_ref), w2t_ref[...])

        def ln_rows(c, carry):
            rows = pl.ds(pl.multiple_of(c * LN_ROWS, LN_ROWS), LN_ROWS)
            y = _layer_norm(ALPHA * x32_ref[rows, :] + o32_ref[rows, :], g_ref[...], b_ref[...])
            o32_ref[rows, :] = y
            o16_ref[rows, :] = y.astype(BF16)
            return carry
        lax.fori_loop(0, o32_ref.shape[0] // LN_ROWS, ln_rows, 0)


def _ffn(x16, x32, w1, w3, w2, ln_g, ln_b):
    s, d = x32.shape
    ff = w1.shape[1]
    tm, tf = FFN_TM, FFN_TF
    n_f = ff // tf
    tail = ff - n_f * tf
    assert 0 < tail < tf and tail % LANES == 0 and (n_f * tf) % tail == 0
    tail_blk = (n_f * tf) // tail
    rowblk = lambda i, f: (i, 0)
    const = lambda i, f: (0, 0)
    vmem = tm * d * 4 + 2 * (tm * d * (2 + 4 + 2) + 3 * d * (tf + tail) * 2) + 5 * tm * tf * 4 \
        + 4 * LN_ROWS * d * 4
    return pl.pallas_call(
        _ffn_kernel,
        grid=(s // tm, n_f),
        in_specs=[
            pl.BlockSpec((tm, d), rowblk),
            pl.BlockSpec((tm, d), rowblk, pipeline_mode=pl.Buffered(1)),
            pl.BlockSpec((d, tf), lambda i, f: (0, f)),
            pl.BlockSpec((d, tf), lambda i, f: (0, f)),
            pl.BlockSpec((tf, d), lambda i, f: (f, 0)),
            pl.BlockSpec((d, tail), lambda i, f: (0, tail_blk)),
            pl.BlockSpec((d, tail), lambda i, f: (0, tail_blk)),
            pl.BlockSpec((tail, d), lambda i, f: (tail_blk, 0)),
            pl.BlockSpec((1, d), const),
            pl.BlockSpec((1, d), const),
        ],
        out_specs=[pl.BlockSpec((tm, d), rowblk), pl.BlockSpec((tm, d), rowblk)],
        out_shape=[jax.ShapeDtypeStruct((s, d), F32), jax.ShapeDtypeStruct((s, d), BF16)],
        compiler_params=_params(("arbitrary", "arbitrary"), vmem),
        name="swiglu_ln",
    )(x16, x32, w1, w3, w2, w1, w3, w2, ln_g, ln_b)


ROUTE_E, ROUTE_W, ROUTE_RANK = 0, 2, 4


def _router_kernel(x_ref, w_ref, route_ref, counts_ref, seen_ref):
    @pl.when(pl.program_id(0) == 0)
    def _():
        seen_ref[...] = jnp.zeros_like(seen_ref)

    logits = jnp.dot(x_ref[...], w_ref[...], preferred_element_type=F32,
                     precision=lax.Precision.HIGHEST)
    tm = logits.shape[0]
    lane = lax.broadcasted_iota(jnp.int32, logits.shape, 1).astype(F32)
    neg = jnp.float32(-jnp.inf)
    l1 = jnp.where(lane < N_EXPERTS, logits, neg)
    m1 = jnp.max(l1, axis=-1, keepdims=True)
    i1 = jnp.min(jnp.where(l1 == m1, lane, float(LANES)), axis=-1, keepdims=True)
    l2 = jnp.where(lane == i1, neg, l1)
    m2 = jnp.max(l2, axis=-1, keepdims=True)
    i2 = jnp.min(jnp.where(l2 == m2, lane, float(LANES)), axis=-1, keepdims=True)
    e2 = jnp.exp(m2 - m1)
    w_top = 1.0 / (1.0 + e2)

    chosen = jnp.where(lane == i1, 1.0, 0.0) + jnp.where(lane == i2, 1.0, 0.0)
    row = lax.broadcasted_iota(jnp.int32, (tm, tm), 0)
    col = lax.broadcasted_iota(jnp.int32, (tm, tm), 1)
    earlier = jnp.where(col < row, 1.0, 0.0).astype(BF16)
    prefix = _dot(earlier, chosen.astype(BF16)) + seen_ref[...]
    r1 = jnp.sum(jnp.where(lane == i1, prefix, 0.0), axis=-1, keepdims=True)
    r2 = jnp.sum(jnp.where(lane == i2, prefix, 0.0), axis=-1, keepdims=True)
    seen_ref[...] += jnp.sum(chosen, axis=0, keepdims=True)
    counts_ref[...] = seen_ref[...]

    fields = ((ROUTE_E, i1), (ROUTE_E + 1, i2), (ROUTE_W, w_top), (ROUTE_W + 1, e2 * w_top),
              (ROUTE_RANK, r1), (ROUTE_RANK + 1, r2))
    route = jnp.zeros_like(logits)
    for at, val in fields:
        route = jnp.where(lane == at, val, route)
    route_ref[...] = route


def _router(x32, w_router_padded):
    s, d = x32.shape
    tm = ROUTER_TM
    vmem = 2 * (tm * d * 4 + d * LANES * 4 + tm * LANES * 4) + 8 * tm * d * 4
    return pl.pallas_call(
        _router_kernel,
        grid=(s // tm,),
        in_specs=[pl.BlockSpec((tm, d), lambda i: (i, 0)), pl.BlockSpec((d, LANES), lambda i: (0, 0))],
        out_specs=[pl.BlockSpec((tm, LANES), lambda i: (i, 0)), pl.BlockSpec((1, LANES), lambda i: (0, 0))],
        out_shape=[jax.ShapeDtypeStruct((s, LANES), F32), jax.ShapeDtypeStruct((1, LANES), F32)],
        scratch_shapes=[pltpu.VMEM((1, LANES), F32)],
        compiler_params=_params(("arbitrary",), vmem),
        name="router_top2",
    )(x32, w_router_padded)


def _row_copy(src, src_row, dst, dst_row, sem):
    return pltpu.make_async_copy(src.at[pl.ds(src_row, 1)], dst.at[pl.ds(dst_row, 1)], sem)


def _pack_bf16_pairs(x):
    n = x.shape[1] // 2
    lo = lax.bitcast_convert_type(x[:, :n].astype(BF16).astype(F32), U32)
    hi = lax.bitcast_convert_type(x[:, n:].astype(BF16).astype(F32), U32)
    return (lo >> 16) | (hi & jnp.uint32(0xFFFF0000))


def _unpack_bf16_pairs(w):
    lo = lax.bitcast_convert_type(w << 16, F32).astype(BF16)
    hi = lax.bitcast_convert_type(w & jnp.uint32(0xFFFF0000), F32).astype(BF16)
    return lo, hi


def _dispatch_kernel(pos_ref, x_ref, xs_zero_hbm, xs_hbm, packed_ref, sem):
    del xs_zero_hbm
    tm = DISPATCH_TM
    packed_ref[...] = _pack_bf16_pairs(x_ref[...])

    def start(r, carry):
        for k in range(2):
            _row_copy(packed_ref, r, xs_hbm, pos_ref[0, 0, k * tm + r], sem).start(priority=k)
        return carry

    def wait(r, carry):
        for k in range(2):
            _row_copy(packed_ref, 0, xs_hbm, 0, sem).wait()
        return carry

    lax.fori_loop(0, tm, start, 0)
    lax.fori_loop(0, tm, wait, 0)


def _dispatch(pos_tiles, x32, n_rows):
    s, d = x32.shape
    tm = DISPATCH_TM
    return pl.pallas_call(
        _dispatch_kernel,
        grid=(s // tm,),
        in_specs=[
            pl.BlockSpec((1, 1, 2 * tm), lambda i: (i, 0, 0), memory_space=pltpu.SMEM),
            pl.BlockSpec((tm, d), lambda i: (i, 0)),
            pl.BlockSpec(memory_space=pl.ANY),
        ],
        out_specs=pl.BlockSpec(memory_space=pl.ANY),
        out_shape=jax.ShapeDtypeStruct((n_rows, d // 2), U32),
        scratch_shapes=[pltpu.VMEM((tm, d // 2), U32), pltpu.SemaphoreType.DMA(())],
        input_output_aliases={2: 0},
        compiler_params=_params(("arbitrary",), 2 * tm * d * 4 + 4 * tm * d * 2 + (4 << 20)),
        name="moe_dispatch",
    )(pos_tiles, x32, jnp.zeros((n_rows, d // 2), U32))


def _experts_kernel(tile_expert_ref, tile_rows_ref, n_used_ref, xs_ref, w1_ref, w3_ref, w2_ref, ys_ref,
                    x16_ref):
    del tile_expert_ref, n_used_ref
    f = pl.program_id(1)
    n_rows = tile_rows_ref[pl.program_id(0)]

    @pl.when(f == 0)
    def _():
        ys_ref[...] = jnp.zeros_like(ys_ref)

    def swiglu(rows):
        @pl.when(f == 0)
        def _():
            half = x16_ref.shape[1] // 2
            x16_ref[rows, :half], x16_ref[rows, half:] = _unpack_bf16_pairs(xs_ref[rows])

        x = x16_ref[rows]
        h1 = _dot(x, w1_ref[...].astype(BF16))
        h = (h1 * jax.nn.sigmoid(h1) * _dot(x, w3_ref[...].astype(BF16))).astype(BF16)
        ys_ref[rows] += _dot(h, w2_ref[...].astype(BF16))

    for g in range(1, MOE_GRANULES + 1):
        @pl.when((n_rows > (g - 1) * MOE_TM) & (n_rows <= g * MOE_TM))
        def _():
            swiglu(slice(0, g * MOE_TM))


def _experts(tile_expert, tile_rows, n_used, xs, w1, w3, w2):
    n_rows, d = xs.shape[0], w1.shape[1]
    ff = w1.shape[2]
    tm, tf = MOE_TM * MOE_GRANULES, MOE_TF
    n_f = ff // tf

    def rows(i, f, te, tr, nu):
        return (jnp.minimum(i, nu[0] - 1), 0)

    def chunk(i, f, nu):
        return jnp.where(i < nu[0], f, n_f - 1)

    vmem = tm * d * 2 + 2 * (tm * d * 4 + 3 * d * tf * 4) + tm * d * 2 + 3 * d * tf * 2 + 6 * tm * tf * 4
    return pl.pallas_call(
        _experts_kernel,
        grid_spec=pltpu.PrefetchScalarGridSpec(
            num_scalar_prefetch=3,
            grid=(n_rows // tm, n_f),
            in_specs=[
                pl.BlockSpec((tm, d // 2), rows, pipeline_mode=pl.Buffered(1)),
                pl.BlockSpec((None, d, tf), lambda i, f, te, tr, nu: (te[i], 0, chunk(i, f, nu))),
                pl.BlockSpec((None, d, tf), lambda i, f, te, tr, nu: (te[i], 0, chunk(i, f, nu))),
                pl.BlockSpec((None, tf, d), lambda i, f, te, tr, nu: (te[i], chunk(i, f, nu), 0)),
            ],
            out_specs=pl.BlockSpec((tm, d), lambda i, f, te, tr, nu: (i, 0)),
            scratch_shapes=[pltpu.VMEM((tm, d), BF16)],
        ),
        out_shape=jax.ShapeDtypeStruct((n_rows, d), F32),
        compiler_params=_params(("arbitrary", "arbitrary"), vmem),
        name="moe_experts",
    )(tile_expert, tile_rows, n_used, xs, w1, w3, w2)


def _combine_kernel(pos_ref, pos_next_ref, route_ref, x_ref, ys_hbm, g_ref, b_ref, o_ref, buf, sems):
    tm = COMBINE_TM
    i = pl.program_id(0)
    slot = i % 2

    def start_tile(p_ref, slot):
        def start(r, carry):
            for k in range(2):
                _row_copy(ys_hbm, p_ref[0, 0, k * tm + r], buf.at[slot, k], r, sems.at[slot]).start(priority=k)
            return carry
        lax.fori_loop(0, tm, start, 0)

    @pl.when(i == 0)
    def _():
        start_tile(pos_ref, 0)

    @pl.when(i + 1 < pl.num_programs(0))
    def _():
        start_tile(pos_next_ref, 1 - slot)

    def wait(r, carry):
        for k in range(2):
            _row_copy(ys_hbm, 0, buf.at[slot, k], 0, sems.at[slot]).wait()
        return carry

    lax.fori_loop(0, tm, wait, 0)
    w0 = route_ref[:, ROUTE_W:ROUTE_W + 1]
    w1 = route_ref[:, ROUTE_W + 1:ROUTE_W + 2]
    y = w0 * buf[slot, 0] + w1 * buf[slot, 1]
    o_ref[...] = _layer_norm(ALPHA * x_ref[...] + y, g_ref[...], b_ref[...])


def _combine(pos_tiles, route, x32, ys, ln_g, ln_b):
    s, d = x32.shape
    tm = COMBINE_TM
    n = s // tm
    rowblk = lambda i: (i, 0)
    const = lambda i: (0, 0)
    vmem = 2 * 2 * tm * d * 4 + 2 * (2 * tm * d * 4 + tm * LANES * 4) + 6 * tm * d * 4
    return pl.pallas_call(
        _combine_kernel,
        grid=(n,),
        in_specs=[
            pl.BlockSpec((1, 1, 2 * tm), lambda i: (i, 0, 0), memory_space=pltpu.SMEM),
            pl.BlockSpec((1, 1, 2 * tm), lambda i: (jnp.minimum(i + 1, n - 1), 0, 0),
                         memory_space=pltpu.SMEM),
            pl.BlockSpec((tm, LANES), rowblk),
            pl.BlockSpec((tm, d), rowblk),
            pl.BlockSpec(memory_space=pl.ANY),
            pl.BlockSpec((1, d), const),
            pl.BlockSpec((1, d), const),
        ],
        out_specs=pl.BlockSpec((tm, d), rowblk),
        out_shape=jax.ShapeDtypeStruct((s, d), F32),
        scratch_shapes=[pltpu.VMEM((2, 2, tm, d), F32), pltpu.SemaphoreType.DMA((2,))],
        compiler_params=_params(("arbitrary",), vmem),
        name="moe_combine_ln",
    )(pos_tiles, pos_tiles, route, x32, ys, ln_g, ln_b)


def _pos_tiles(pos, tm):
    s = pos.shape[0]
    return pos.reshape(s // tm, tm, 2).transpose(0, 2, 1).reshape(s // tm, 1, 2 * tm)


def _moe(x32, w_router, w1, w3, w2, ln_g, ln_b):
    s, d = x32.shape
    tm = MOE_TM * MOE_GRANULES
    n_tiles = (2 * s) // tm + N_EXPERTS
    route, counts = _router(x32, jnp.pad(w_router, ((0, 0), (0, LANES - N_EXPERTS))))

    counts = counts[0, :N_EXPERTS].astype(jnp.int32)
    tiles_per_expert = (counts + tm - 1) // tm
    tile_end = jnp.cumsum(tiles_per_expert)
    tile_start = tile_end - tiles_per_expert
    n_used = tile_end[-1:]
    tile_ids = jnp.arange(n_tiles, dtype=jnp.int32)
    tile_expert = jnp.sum(jnp.minimum(tile_ids, n_used - 1)[:, None] >= tile_end[None, :],
                          axis=1).astype(jnp.int32)
    tile_rows = jnp.clip(counts[tile_expert] - (tile_ids - tile_start[tile_expert]) * tm, 0, tm)
    tile_rows = jnp.where(tile_ids < n_used, tile_rows, 0).astype(jnp.int32)
    experts = route[:, ROUTE_E:ROUTE_E + 2].astype(jnp.int32)
    pos = (tile_start * tm)[experts] + route[:, ROUTE_RANK:ROUTE_RANK + 2].astype(jnp.int32)

    xs = _dispatch(_pos_tiles(pos, DISPATCH_TM), x32, n_tiles * tm)
    ys = _experts(tile_expert, tile_rows, n_used, xs, w1, w3, w2)
    return _combine(_pos_tiles(pos, COMBINE_TM), route, x32, ys, ln_g, ln_b)


def kernel(x, w_in, b_gate, sg_w, sg_b, sg_ln_g, sg_ln_b, w_branch_a, w_branch_b, w_out,
           ln1_g, ln1_b, ffn_w1, ffn_w3, ffn_w2, moe_router, moe_w1, moe_w3, moe_w2,
           ln2_g, ln2_b):
    b, s, d = x.shape
    assert (b, s, d) == (1, SEQ, D_MODEL)
    x32 = x.reshape(s, d)
    x16 = x32.astype(BF16)
    for layer in range(DEPTH):
        qkv = _inproj(x16, w_in, layer, 0, 3 * SB_WIDTH, "qkv")
        u = _inproj(x16, w_in, layer, OFF_U, SG_WIDTH, "gelu")
        vn = _inproj(x16, w_in, layer, OFF_VG, SG_WIDTH, "gelu_ln",
                     (sg_ln_g[layer].reshape(1, -1), sg_ln_b[layer].reshape(1, -1)))
        gates = _inproj(x16, w_in, layer, OFF_GATE, 2 * d, "gate", (b_gate[layer].reshape(1, -1),))
        att = _attention(qkv)
        sgu = _sgu(u, vn, sg_w[layer], sg_b[layer].T)
        x32, x16 = _merge(att, sgu, gates, x32,
                          w_branch_a[layer].astype(BF16), w_branch_b[layer].astype(BF16),
                          w_out[layer].astype(BF16),
                          ln1_g[layer].reshape(1, d), ln1_b[layer].reshape(1, d))
        i = layer // 2
        g2, b2 = ln2_g[layer].reshape(1, d), ln2_b[layer].reshape(1, d)
        if layer % 2 == 0:
            x32, x16 = _ffn(x16, x32, ffn_w1[i].astype(BF16), ffn_w3[i].astype(BF16),
                            ffn_w2[i].astype(BF16), g2, b2)
        else:
            x32 = _moe(x32, moe_router[i], moe_w1[i], moe_w3[i], moe_w2[i], g2, b2)
            x16 = x32.astype(BF16) if layer + 1 < DEPTH else None
    return x32.reshape(b, s, d)
```

```python
import functools

import jax
import jax.numpy as jnp
from jax import lax
from jax.experimental import pallas as pl
from jax.experimental.pallas import tpu as pltpu

F32 = jnp.float32
BF16 = jnp.bfloat16
U32 = jnp.uint32

D_MODEL = 2048
SEQ = 8192
DEPTH = 2
HEADS = 8
HEAD_DIM = 128
SB_WIDTH = HEADS * HEAD_DIM
GROUPS = 8
GROUP_DIM = 128
SG_WIDTH = GROUPS * GROUP_DIM
CHUNK = 128
OFF_U = 3 * SB_WIDTH
OFF_VG = OFF_U + SG_WIDTH
OFF_GATE = OFF_VG + SG_WIDTH
D_FF_DENSE = 5504
N_EXPERTS = 8
D_FF_EXPERT = 7168
ALPHA = (2.0 * DEPTH) ** 0.25
LN_EPS = 1e-5
SB_UNDERFLOW = 110.0

LANES = 128
V7X_VMEM_BYTES = 64 * 1024 * 1024

PROJ_TM = 1024
PROJ_TN = 1024
ATT_TQ = 256
ATT_TK = 256
ATT_HEADS = 8
SGU_ROWS = 512
MERGE_TM = 256
FFN_TM = 1024
FFN_TF = 256
LN_ROWS = 256
ROUTER_TM = 512
DISPATCH_TM = 512
MOE_TM = 256
MOE_GRANULES = 5
MOE_TF = 256
COMBINE_TM = 256


def _params(semantics, vmem_bytes):
    assert vmem_bytes < V7X_VMEM_BYTES
    return pltpu.CompilerParams(dimension_semantics=semantics, vmem_limit_bytes=vmem_bytes)


def _dot(a, b):
    return jnp.dot(a, b, preferred_element_type=F32)


def _layer_norm(y, g, b):
    mu = jnp.mean(y, axis=-1, keepdims=True)
    d = y - mu
    var = jnp.mean(d * d, axis=-1, keepdims=True)
    return d * lax.rsqrt(var + LN_EPS) * g + b


def _gelu_tanh(x):
    return 0.5 * x * (1.0 + jnp.tanh(0.7978845608028654 * (x + 0.044715 * (x * x * x))))


def _inproj_kernel(x_ref, w_ref, *rest, mode):
    *rest, w16_ref = rest

    @pl.when(pl.program_id(1) == 0)
    def _():
        w16_ref[...] = w_ref[...].astype(BF16)

    acc = _dot(x_ref[...], w16_ref[...])
    if mode == "qkv":
        (o_ref,) = rest
        scale = jnp.where(pl.program_id(0) == 0, HEAD_DIM ** -0.5, 1.0).astype(F32)
        o_ref[...] = (acc * scale).astype(o_ref.dtype)
    elif mode == "gelu":
        (o_ref,) = rest
        o_ref[...] = _gelu_tanh(acc).astype(o_ref.dtype)
    elif mode == "gelu_ln":
        g_ref, b_ref, o_ref = rest
        act = _gelu_tanh(acc)
        for grp in range(acc.shape[1] // GROUP_DIM):
            cols = slice(grp * GROUP_DIM, (grp + 1) * GROUP_DIM)
            o_ref[:, cols] = _layer_norm(act[:, cols], g_ref[:, cols], b_ref[:, cols]).astype(o_ref.dtype)
    elif mode == "gate":
        b_ref, o_ref = rest
        o_ref[...] = jax.nn.sigmoid(acc + b_ref[...]).astype(o_ref.dtype)
    else:
        raise ValueError(mode)


def _inproj(x16, w_in, layer, col_off, width, mode, extra=()):
    s, d = x16.shape
    tm, tn = PROJ_TM, PROJ_TN
    n_blk = width // tn
    off_blk = col_off // tn
    in_specs = [
        pl.BlockSpec((tm, d), lambda n, m: (m, 0)),
        pl.BlockSpec((None, d, tn), lambda n, m: (layer, 0, off_blk + n)),
    ]
    for _ in extra:
        in_specs.append(pl.BlockSpec((1, tn), lambda n, m: (0, n)))
    vmem = 2 * (tm * d * 2 + d * tn * 4 + tm * tn * 2) + d * tn * 2 + 6 * tm * tn * 4
    return pl.pallas_call(
        functools.partial(_inproj_kernel, mode=mode),
        grid=(n_blk, s // tm),
        in_specs=in_specs,
        out_specs=pl.BlockSpec((tm, tn), lambda n, m: (m, n)),
        out_shape=jax.ShapeDtypeStruct((s, width), BF16),
        scratch_shapes=[pltpu.VMEM((d, tn), BF16)],
        compiler_params=_params(("arbitrary", "arbitrary"), vmem),
        name="inproj_" + mode,
    )(x16, w_in, *extra)


def _attn_kernel(q_ref, k_ref, v_ref, o_ref, z_ref, ls_ref, hl_ref, tail_ref, a_ref, acc_ref, c_ref):
    tq, t = ATT_TQ, ATT_TK
    tiles_per_q = tq // t
    i = pl.program_id(1)
    krow = lax.broadcasted_iota(jnp.int32, (t, t), 0)
    kcol = lax.broadcasted_iota(jnp.int32, (t, t), 1)
    later = jnp.where(krow > kcol, 1.0, 0.0).astype(BF16)
    later2 = jnp.concatenate([later, later], axis=0)
    row = lax.broadcasted_iota(jnp.int32, (tq, t), 0)
    col = lax.broadcasted_iota(jnp.int32, (tq, t), 1)
    heads = [slice(h * HEAD_DIM, (h + 1) * HEAD_DIM) for h in range(ATT_HEADS)]

    def key_tile(j, mask):
        keys = pl.ds(pl.multiple_of(j * t, t), t)
        for h, hs in enumerate(heads):
            z_ref[h] = lax.dot_general(q_ref[:, hs], k_ref[keys, hs], (((1,), (1,)), ((), ())),
                                       preferred_element_type=F32)
        for h in range(ATT_HEADS):
            z = z_ref[h]
            ls = jnp.minimum(z, 0.0) - jnp.log(1.0 + jnp.exp(-jnp.abs(z)))
            lk = ls - z
            if mask is not None:
                lk = jnp.where(mask, lk, 0.0)
            hi = lk.astype(BF16)
            ls_ref[h] = ls
            hl_ref[h, :, :t] = hi
            hl_ref[h, :, t:] = (lk - hi.astype(F32)).astype(BF16)
        for h in range(ATT_HEADS):
            tail_ref[h] = _dot(hl_ref[h], later2)
        for h, hs in enumerate(heads):
            tail = tail_ref[h]
            a = jnp.exp(ls_ref[h] + tail + jnp.tile(c_ref[:, hs], (1, t // HEAD_DIM)))
            lk0 = ls_ref[h][:, :1] - z_ref[h][:, :1]
            if mask is not None:
                a = jnp.where(mask, a, 0.0)
                lk0 = jnp.where(mask[:, :1], lk0, 0.0)
            a_ref[h] = a.astype(BF16)
            c_ref[:, hs] += jnp.broadcast_to(tail[:, :1] + lk0, (tq, HEAD_DIM))
        for h, hs in enumerate(heads):
            acc_ref[:, hs] += _dot(a_ref[h], v_ref[keys, hs])

    def any_weight_left():
        return jnp.max(c_ref[...]) > -SB_UNDERFLOW

    acc_ref[...] = jnp.zeros_like(acc_ref)
    c_ref[...] = jnp.zeros_like(c_ref)
    for back in range(tiles_per_q):
        offset = (tiles_per_q - 1 - back) * t
        key_tile(i * tiles_per_q + (tiles_per_q - 1 - back), col + offset < row)

    def cond(carry):
        j, go = carry
        return jnp.logical_and(j >= 0, go)

    def body(carry):
        j, _ = carry
        key_tile(j, None)
        return j - 1, any_weight_left()

    lax.while_loop(cond, body, (i * tiles_per_q - 1, any_weight_left()))
    o_ref[...] = acc_ref[...].astype(o_ref.dtype)


def _attention(qkv):
    s = qkv.shape[0]
    tq, t = ATT_TQ, ATT_TK
    nh = ATT_HEADS
    w = nh * HEAD_DIM
    groups = SB_WIDTH // w
    resident = pl.Buffered(1)
    vmem = 2 * s * w * 2 + 2 * (2 * tq * w * 2) + nh * 16 * tq * t + 2 * tq * w * 4 + (8 << 20)
    return pl.pallas_call(
        _attn_kernel,
        grid=(groups, s // tq),
        in_specs=[
            pl.BlockSpec((tq, w), lambda g, i: (i, g)),
            pl.BlockSpec((s, w), lambda g, i: (0, groups + g), pipeline_mode=resident),
            pl.BlockSpec((s, w), lambda g, i: (0, 2 * groups + g), pipeline_mode=resident),
        ],
        out_specs=pl.BlockSpec((tq, w), lambda g, i: (i, g)),
        out_shape=jax.ShapeDtypeStruct((s, SB_WIDTH), BF16),
        scratch_shapes=[
            pltpu.VMEM((nh, tq, t), F32),
            pltpu.VMEM((nh, tq, t), F32),
            pltpu.VMEM((nh, tq, 2 * t), BF16),
            pltpu.VMEM((nh, tq, t), F32),
            pltpu.VMEM((nh, tq, t), BF16),
            pltpu.VMEM((tq, w), F32),
            pltpu.VMEM((tq, w), F32),
        ],
        compiler_params=_params(("arbitrary", "arbitrary"), vmem),
        name="stickbreak_attn",
    )(qkv, qkv, qkv)


def _sgu_kernel(u_ref, v_ref, w_ref, b_ref, o_ref):
    c = CHUNK
    row = lax.broadcasted_iota(jnp.int32, (c, c), 0)
    col = lax.broadcasted_iota(jnp.int32, (c, c), 1)
    causal = col <= row
    for g in range(GROUPS):
        w = jnp.where(causal, w_ref[g], 0.0).astype(BF16)
        b = b_ref[:, g:g + 1]
        cols = slice(g * GROUP_DIM, (g + 1) * GROUP_DIM)
        for cc in range(SGU_ROWS // c):
            rows = slice(cc * c, (cc + 1) * c)
            mixed = _dot(w, v_ref[rows, cols]) + b
            o_ref[rows, cols] = (u_ref[rows, cols].astype(F32) * mixed).astype(o_ref.dtype)


def _sgu(u, vn, sg_w, sg_b_t):
    s = u.shape[0]
    r = SGU_ROWS
    vmem = 2 * 3 * r * SG_WIDTH * 2 + 2 * GROUPS * CHUNK * CHUNK * 4 + (4 << 20)
    return pl.pallas_call(
        _sgu_kernel,
        grid=(s // r,),
        in_specs=[
            pl.BlockSpec((r, SG_WIDTH), lambda i: (i, 0)),
            pl.BlockSpec((r, SG_WIDTH), lambda i: (i, 0)),
            pl.BlockSpec((GROUPS, CHUNK, CHUNK), lambda i: (0, 0, 0)),
            pl.BlockSpec((CHUNK, GROUPS), lambda i: (0, 0)),
        ],
        out_specs=pl.BlockSpec((r, SG_WIDTH), lambda i: (i, 0)),
        out_shape=jax.ShapeDtypeStruct((s, SG_WIDTH), BF16),
        compiler_params=_params(("arbitrary",), vmem),
        name="spatial_gating",
    )(u, vn, sg_w, sg_b_t)


def _merge_kernel(a_ref, b_ref, gate_ref, x_ref, wa_ref, wb_ref, wo_ref, g_ref, beta_ref,
                  o32_ref, o16_ref):
    d = D_MODEL
    ya = _dot(a_ref[...], wa_ref[...])
    yb = _dot(b_ref[...], wb_ref[...])
    merged = gate_ref[:, :d].astype(F32) * ya + gate_ref[:, d:].astype(F32) * yb
    mix = _dot(merged.astype(BF16), wo_ref[...])
    y = _layer_norm(ALPHA * x_ref[...] + mix, g_ref[...], beta_ref[...])
    o32_ref[...] = y
    o16_ref[...] = y.astype(BF16)


def _merge(att, sgu, gates, x32, wa, wb, wo, ln_g, ln_b):
    s, d = x32.shape
    tm = MERGE_TM
    const = lambda i: (0, 0)
    rowblk = lambda i: (i, 0)
    weights = (SB_WIDTH * d + SG_WIDTH * d + d * d) * 2
    vmem = 2 * weights + 2 * tm * (SB_WIDTH * 2 + SG_WIDTH * 2 + 2 * d * 2 + d * 4 + d * 4 + d * 2) \
        + 6 * tm * d * 4
    return pl.pallas_call(
        _merge_kernel,
        grid=(s // tm,),
        in_specs=[
            pl.BlockSpec((tm, SB_WIDTH), rowblk),
            pl.BlockSpec((tm, SG_WIDTH), rowblk),
            pl.BlockSpec((tm, 2 * d), rowblk),
            pl.BlockSpec((tm, d), rowblk),
            pl.BlockSpec((SB_WIDTH, d), const),
            pl.BlockSpec((SG_WIDTH, d), const),
            pl.BlockSpec((d, d), const),
            pl.BlockSpec((1, d), const),
            pl.BlockSpec((1, d), const),
        ],
        out_specs=[pl.BlockSpec((tm, d), rowblk), pl.BlockSpec((tm, d), rowblk)],
        out_shape=[jax.ShapeDtypeStruct((s, d), F32), jax.ShapeDtypeStruct((s, d), BF16)],
        compiler_params=_params(("arbitrary",), vmem),
        name="merge_outproj_ln",
    )(att, sgu, gates, x32, wa, wb, wo, ln_g, ln_b)


def _swiglu_hidden(x, w1_ref, w3_ref):
    h1 = _dot(x, w1_ref[...])
    return (h1 * jax.nn.sigmoid(h1) * _dot(x, w3_ref[...])).astype(BF16)


def _ffn_kernel(x16_ref, x32_ref, w1_ref, w3_ref, w2_ref, w1t_ref, w3t_ref, w2t_ref, g_ref, b_ref,
                o32_ref, o16_ref):
    f = pl.program_id(1)

    @pl.when(f == 0)
    def _():
        o32_ref[...] = jnp.zeros_like(o32_ref)

    o32_ref[...] += _dot(_swiglu_hidden(x16_ref[...], w1_ref, w3_ref), w2_ref[...])

    @pl.when(f == pl.num_programs(1) - 1)
    def _():
        o32_ref[...] += _dot(_swiglu_hidden(x16_ref[...], w1t_ref, w3t_ref), w2t_ref[...])

        def ln_rows(c, carry):
            rows = pl.ds(pl.multiple_of(c * LN_ROWS, LN_ROWS), LN_ROWS)
            y = _layer_norm(ALPHA * x32_ref[rows, :] + o32_ref[rows, :], g_ref[...], b_ref[...])
            o32_ref[rows, :] = y
            o16_ref[rows, :] = y.astype(BF16)
            return carry
        lax.fori_loop(0, o32_ref.shape[0] // LN_ROWS, ln_rows, 0)


def _ffn(x16, x32, w1, w3, w2, ln_g, ln_b):
    s, d = x32.shape
    ff = w1.shape[1]
    tm, tf = FFN_TM, FFN_TF
    n_f = ff // tf
    tail = ff - n_f * tf
    assert 0 < tail < tf and tail % LANES == 0 and (n_f * tf) % tail == 0
    tail_blk = (n_f * tf) // tail
    rowblk = lambda i, f: (i, 0)
    const = lambda i, f: (0, 0)
    vmem = tm * d * 4 + 2 * (tm * d * (2 + 4 + 2) + 3 * d * (tf + tail) * 2) + 5 * tm * tf * 4 \
        + 4 * LN_ROWS * d * 4
    return pl.pallas_call(
        _ffn_kernel,
        grid=(s // tm, n_f),
        in_specs=[
            pl.BlockSpec((tm, d), rowblk),
            pl.BlockSpec((tm, d), rowblk, pipeline_mode=pl.Buffered(1)),
            pl.BlockSpec((d, tf), lambda i, f: (0, f)),
            pl.BlockSpec((d, tf), lambda i, f: (0, f)),
            pl.BlockSpec((tf, d), lambda i, f: (f, 0)),
            pl.BlockSpec((d, tail), lambda i, f: (0, tail_blk)),
            pl.BlockSpec((d, tail), lambda i, f: (0, tail_blk)),
            pl.BlockSpec((tail, d), lambda i, f: (tail_blk, 0)),
            pl.BlockSpec((1, d), const),
            pl.BlockSpec((1, d), const),
        ],
        out_specs=[pl.BlockSpec((tm, d), rowblk), pl.BlockSpec((tm, d), rowblk)],
        out_shape=[jax.ShapeDtypeStruct((s, d), F32), jax.ShapeDtypeStruct((s, d), BF16)],
        compiler_params=_params(("arbitrary", "arbitrary"), vmem),
        name="swiglu_ln",
    )(x16, x32, w1, w3, w2, w1, w3, w2, ln_g, ln_b)


ROUTE_E, ROUTE_W, ROUTE_RANK = 0, 2, 4


def _router_kernel(x_ref, w_ref, route_ref, counts_ref, seen_ref):
    @pl.when(pl.program_id(0) == 0)
    def _():
        seen_ref[...] = jnp.zeros_like(seen_ref)

    logits = jnp.dot(x_ref[...], w_ref[...], preferred_element_type=F32,
                     precision=lax.Precision.HIGHEST)
    tm = logits.shape[0]
    lane = lax.broadcasted_iota(jnp.int32, logits.shape, 1).astype(F32)
    neg = jnp.float32(-jnp.inf)
    l1 = jnp.where(lane < N_EXPERTS, logits, neg)
    m1 = jnp.max(l1, axis=-1, keepdims=True)
    i1 = jnp.min(jnp.where(l1 == m1, lane, float(LANES)), axis=-1, keepdims=True)
    l2 = jnp.where(lane == i1, neg, l1)
    m2 = jnp.max(l2, axis=-1, keepdims=True)
    i2 = jnp.min(jnp.where(l2 == m2, lane, float(LANES)), axis=-1, keepdims=True)
    e2 = jnp.exp(m2 - m1)
    w_top = 1.0 / (1.0 + e2)

    chosen = jnp.where(lane == i1, 1.0, 0.0) + jnp.where(lane == i2, 1.0, 0.0)
    row = lax.broadcasted_iota(jnp.int32, (tm, tm), 0)
    col = lax.broadcasted_iota(jnp.int32, (tm, tm), 1)
    earlier = jnp.where(col < row, 1.0, 0.0).astype(BF16)
    prefix = _dot(earlier, chosen.astype(BF16)) + seen_ref[...]
    r1 = jnp.sum(jnp.where(lane == i1, prefix, 0.0), axis=-1, keepdims=True)
    r2 = jnp.sum(jnp.where(lane == i2, prefix, 0.0), axis=-1, keepdims=True)
    seen_ref[...] += jnp.sum(chosen, axis=0, keepdims=True)
    counts_ref[...] = seen_ref[...]

    fields = ((ROUTE_E, i1), (ROUTE_E + 1, i2), (ROUTE_W, w_top), (ROUTE_W + 1, e2 * w_top),
              (ROUTE_RANK, r1), (ROUTE_RANK + 1, r2))
    route = jnp.zeros_like(logits)
    for at, val in fields:
        route = jnp.where(lane == at, val, route)
    route_ref[...] = route


def _router(x32, w_router_padded):
    s, d = x32.shape
    tm = ROUTER_TM
    vmem = 2 * (tm * d * 4 + d * LANES * 4 + tm * LANES * 4) + 8 * tm * d * 4
    return pl.pallas_call(
        _router_kernel,
        grid=(s // tm,),
        in_specs=[pl.BlockSpec((tm, d), lambda i: (i, 0)), pl.BlockSpec((d, LANES), lambda i: (0, 0))],
        out_specs=[pl.BlockSpec((tm, LANES), lambda i: (i, 0)), pl.BlockSpec((1, LANES), lambda i: (0, 0))],
        out_shape=[jax.ShapeDtypeStruct((s, LANES), F32), jax.ShapeDtypeStruct((1, LANES), F32)],
        scratch_shapes=[pltpu.VMEM((1, LANES), F32)],
        compiler_params=_params(("arbitrary",), vmem),
        name="router_top2",
    )(x32, w_router_padded)


def _row_copy(src, src_row, dst, dst_row, sem):
    return pltpu.make_async_copy(src.at[pl.ds(src_row, 1)], dst.at[pl.ds(dst_row, 1)], sem)


def _pack_bf16_pairs(x):
    n = x.shape[1] // 2
    lo = lax.bitcast_convert_type(x[:, :n].astype(BF16).astype(F32), U32)
    hi = lax.bitcast_convert_type(x[:, n:].astype(BF16).astype(F32), U32)
    return (lo >> 16) | (hi & jnp.uint32(0xFFFF0000))


def _unpack_bf16_pairs(w):
    lo = lax.bitcast_convert_type(w << 16, F32).astype(BF16)
    hi = lax.bitcast_convert_type(w & jnp.uint32(0xFFFF0000), F32).astype(BF16)
    return lo, hi


def _dispatch_kernel(pos_ref, x_ref, xs_zero_hbm, xs_hbm, packed_ref, sem):
    del xs_zero_hbm
    tm = DISPATCH_TM
    packed_ref[...] = _pack_bf16_pairs(x_ref[...])

    def start(r, carry):
        for k in range(2):
            _row_copy(packed_ref, r, xs_hbm, pos_ref[0, 0, k * tm + r], sem).start(priority=k)
        return carry

    def wait(r, carry):
        for k in range(2):
            _row_copy(packed_ref, 0, xs_hbm, 0, sem).wait()
        return carry

    lax.fori_loop(0, tm, start, 0)
    lax.fori_loop(0, tm, wait, 0)


def _dispatch(pos_tiles, x32, n_rows):
    s, d = x32.shape
    tm = DISPATCH_TM
    return pl.pallas_call(
        _dispatch_kernel,
        grid=(s // tm,),
        in_specs=[
            pl.BlockSpec((1, 1, 2 * tm), lambda i: (i, 0, 0), memory_space=pltpu.SMEM),
            pl.BlockSpec((tm, d), lambda i: (i, 0)),
            pl.BlockSpec(memory_space=pl.ANY),
        ],
        out_specs=pl.BlockSpec(memory_space=pl.ANY),
        out_shape=jax.ShapeDtypeStruct((n_rows, d // 2), U32),
        scratch_shapes=[pltpu.VMEM((tm, d // 2), U32), pltpu.SemaphoreType.DMA(())],
        input_output_aliases={2: 0},
        compiler_params=_params(("arbitrary",), 2 * tm * d * 4 + 4 * tm * d * 2 + (4 << 20)),
        name="moe_dispatch",
    )(pos_tiles, x32, jnp.zeros((n_rows, d // 2), U32))


def _experts_kernel(tile_expert_ref, tile_rows_ref, n_used_ref, xs_ref, w1_ref, w3_ref, w2_ref, ys_ref,
                    x16_ref, h_ref):
    del tile_expert_ref, n_used_ref
    f = pl.program_id(1)
    n_rows = tile_rows_ref[pl.program_id(0)]

    @pl.when(f == 0)
    def _():
        ys_ref[...] = jnp.zeros_like(ys_ref)

    def swiglu(rows):
        @pl.when(f == 0)
        def _():
            half = x16_ref.shape[1] // 2
            x16_ref[rows, :half], x16_ref[rows, half:] = _unpack_bf16_pairs(xs_ref[rows])

        x = x16_ref[rows]
        h1 = _dot(x, w1_ref[...].astype(BF16))
        h = (h1 * jax.nn.sigmoid(h1) * _dot(x, w3_ref[...].astype(BF16))).astype(BF16)

        @pl.when(f % 2 == 0)
        def _():
            h_ref[rows, :MOE_TF] = h

        @pl.when(f % 2 == 1)
        def _():
            h_ref[rows, MOE_TF:] = h
            ys_ref[rows] += _dot(h_ref[rows], w2_ref[...].astype(BF16))

    for g in range(1, MOE_GRANULES + 1):
        @pl.when((n_rows > (g - 1) * MOE_TM) & (n_rows <= g * MOE_TM))
        def _():
            swiglu(slice(0, g * MOE_TM))


def _experts(tile_expert, tile_rows, n_used, xs, w1, w3, w2):
    n_rows, d = xs.shape[0], w1.shape[1]
    ff = w1.shape[2]
    tm, tf = MOE_TM * MOE_GRANULES, MOE_TF
    n_f = ff // tf
    assert n_f % 2 == 0

    def rows(i, f, te, tr, nu):
        return (jnp.minimum(i, nu[0] - 1), 0)

    def chunk(i, f, nu):
        return jnp.where(i < nu[0], f, n_f - 1)

    vmem = tm * d * 2 + 2 * (tm * d * 4 + 4 * d * tf * 4) + tm * d * 2 + 4 * d * tf * 2 + 8 * tm * tf * 4
    return pl.pallas_call(
        _experts_kernel,
        grid_spec=pltpu.PrefetchScalarGridSpec(
            num_scalar_prefetch=3,
            grid=(n_rows // tm, n_f),
            in_specs=[
                pl.BlockSpec((tm, d // 2), rows, pipeline_mode=pl.Buffered(1)),
                pl.BlockSpec((None, d, tf), lambda i, f, te, tr, nu: (te[i], 0, chunk(i, f, nu))),
                pl.BlockSpec((None, d, tf), lambda i, f, te, tr, nu: (te[i], 0, chunk(i, f, nu))),
                pl.BlockSpec((None, 2 * tf, d), lambda i, f, te, tr, nu: (te[i], chunk(i, f, nu) // 2, 0)),
            ],
            out_specs=pl.BlockSpec((tm, d), lambda i, f, te, tr, nu: (i, 0)),
            scratch_shapes=[pltpu.VMEM((tm, d), BF16), pltpu.VMEM((tm, 2 * tf), BF16)],
        ),
        out_shape=jax.ShapeDtypeStruct((n_rows, d), F32),
        compiler_params=_params(("arbitrary", "arbitrary"), vmem),
        name="moe_experts",
    )(tile_expert, tile_rows, n_used, xs, w1, w3, w2)


def _combine_kernel(pos_ref, pos_next_ref, route_ref, x_ref, ys_hbm, g_ref, b_ref, o_ref, buf, sems):
    tm = COMBINE_TM
    i = pl.program_id(0)
    slot = i % 2

    def start_tile(p_ref, slot):
        def start(r, carry):
            for k in range(2):
                _row_copy(ys_hbm, p_ref[0, 0, k * tm + r], buf.at[slot, k], r, sems.at[slot]).start(priority=k)
            return carry
        lax.fori_loop(0, tm, start, 0)

    @pl.when(i == 0)
    def _():
        start_tile(pos_ref, 0)

    @pl.when(i + 1 < pl.num_programs(0))
    def _():
        start_tile(pos_next_ref, 1 - slot)

    def wait(r, carry):
        for k in range(2):
            _row_copy(ys_hbm, 0, buf.at[slot, k], 0, sems.at[slot]).wait()
        return carry

    lax.fori_loop(0, tm, wait, 0)
    w0 = route_ref[:, ROUTE_W:ROUTE_W + 1]
    w1 = route_ref[:, ROUTE_W + 1:ROUTE_W + 2]
    y = w0 * buf[slot, 0] + w1 * buf[slot, 1]
    o_ref[...] = _layer_norm(ALPHA * x_ref[...] + y, g_ref[...], b_ref[...])


def _combine(pos_tiles, route, x32, ys, ln_g, ln_b):
    s, d = x32.shape
    tm = COMBINE_TM
    n = s // tm
    rowblk = lambda i: (i, 0)
    const = lambda i: (0, 0)
    vmem = 2 * 2 * tm * d * 4 + 2 * (2 * tm * d * 4 + tm * LANES * 4) + 6 * tm * d * 4
    return pl.pallas_call(
        _combine_kernel,
        grid=(n,),
        in_specs=[
            pl.BlockSpec((1, 1, 2 * tm), lambda i: (i, 0, 0), memory_space=pltpu.SMEM),
            pl.BlockSpec((1, 1, 2 * tm), lambda i: (jnp.minimum(i + 1, n - 1), 0, 0),
                         memory_space=pltpu.SMEM),
            pl.BlockSpec((tm, LANES), rowblk),
            pl.BlockSpec((tm, d), rowblk),
            pl.BlockSpec(memory_space=pl.ANY),
            pl.BlockSpec((1, d), const),
            pl.BlockSpec((1, d), const),
        ],
        out_specs=pl.BlockSpec((tm, d), rowblk),
        out_shape=jax.ShapeDtypeStruct((s, d), F32),
        scratch_shapes=[pltpu.VMEM((2, 2, tm, d), F32), pltpu.SemaphoreType.DMA((2,))],
        compiler_params=_params(("arbitrary",), vmem),
        name="moe_combine_ln",
    )(pos_tiles, pos_tiles, route, x32, ys, ln_g, ln_b)


def _pos_tiles(pos, tm):
    s = pos.shape[0]
    return pos.reshape(s // tm, tm, 2).transpose(0, 2, 1).reshape(s // tm, 1, 2 * tm)


def _moe(x32, w_router, w1, w3, w2, ln_g, ln_b):
    s, d = x32.shape
    tm = MOE_TM * MOE_GRANULES
    n_tiles = (2 * s) // tm + N_EXPERTS
    route, counts = _router(x32, jnp.pad(w_router, ((0, 0), (0, LANES - N_EXPERTS))))

    counts = counts[0, :N_EXPERTS].astype(jnp.int32)
    tiles_per_expert = (counts + tm - 1) // tm
    tile_end = jnp.cumsum(tiles_per_expert)
    tile_start = tile_end - tiles_per_expert
    n_used = tile_end[-1:]
    tile_ids = jnp.arange(n_tiles, dtype=jnp.int32)
    tile_expert = jnp.sum(jnp.minimum(tile_ids, n_used - 1)[:, None] >= tile_end[None, :],
                          axis=1).astype(jnp.int32)
    tile_rows = jnp.clip(counts[tile_expert] - (tile_ids - tile_start[tile_expert]) * tm, 0, tm)
    tile_rows = jnp.where(tile_ids < n_used, tile_rows, 0).astype(jnp.int32)
    experts = route[:, ROUTE_E:ROUTE_E + 2].astype(jnp.int32)
    pos = (tile_start * tm)[experts] + route[:, ROUTE_RANK:ROUTE_RANK + 2].astype(jnp.int32)

    xs = _dispatch(_pos_tiles(pos, DISPATCH_TM), x32, n_tiles * tm)
    ys = _experts(tile_expert, tile_rows, n_used, xs, w1, w3, w2)
    return _combine(_pos_tiles(pos, COMBINE_TM), route, x32, ys, ln_g, ln_b)


def kernel(x, w_in, b_gate, sg_w, sg_b, sg_ln_g, sg_ln_b, w_branch_a, w_branch_b, w_out,
           ln1_g, ln1_b, ffn_w1, ffn_w3, ffn_w2, moe_router, moe_w1, moe_w3, moe_w2,
           ln2_g, ln2_b):
    b, s, d = x.shape
    assert (b, s, d) == (1, SEQ, D_MODEL)
    x32 = x.reshape(s, d)
    x16 = x32.astype(BF16)
    for layer in range(DEPTH):
        qkv = _inproj(x16, w_in, layer, 0, 3 * SB_WIDTH, "qkv")
        u = _inproj(x16, w_in, layer, OFF_U, SG_WIDTH, "gelu")
        vn = _inproj(x16, w_in, layer, OFF_VG, SG_WIDTH, "gelu_ln",
                     (sg_ln_g[layer].reshape(1, -1), sg_ln_b[layer].reshape(1, -1)))
        gates = _inproj(x16, w_in, layer, OFF_GATE, 2 * d, "gate", (b_gate[layer].reshape(1, -1),))
        att = _attention(qkv)
        sgu = _sgu(u, vn, sg_w[layer], sg_b[layer].T)
        x32, x16 = _merge(att, sgu, gates, x32,
                          w_branch_a[layer].astype(BF16), w_branch_b[layer].astype(BF16),
                          w_out[layer].astype(BF16),
                          ln1_g[layer].reshape(1, d), ln1_b[layer].reshape(1, d))
        i = layer // 2
        g2, b2 = ln2_g[layer].reshape(1, d), ln2_b[layer].reshape(1, d)
        if layer % 2 == 0:
            x32, x16 = _ffn(x16, x32, ffn_w1[i].astype(BF16), ffn_w3[i].astype(BF16),
                            ffn_w2[i].astype(BF16), g2, b2)
        else:
            x32 = _moe(x32, moe_router[i], moe_w1[i], moe_w3[i], moe_w2[i], g2, b2)
            x16 = x32.astype(BF16) if layer + 1 < DEPTH else None
    return x32.reshape(b, s, d)
```

```python
import functools

import jax
import jax.numpy as jnp
from jax import lax
from jax.experimental import pallas as pl
from jax.experimental.pallas import tpu as pltpu

F32 = jnp.float32
BF16 = jnp.bfloat16
U32 = jnp.uint32

D_MODEL = 2048
SEQ = 8192
DEPTH = 2
HEADS = 8
HEAD_DIM = 128
SB_WIDTH = HEADS * HEAD_DIM
GROUPS = 8
GROUP_DIM = 128
SG_WIDTH = GROUPS * GROUP_DIM
CHUNK = 128
OFF_U = 3 * SB_WIDTH
OFF_VG = OFF_U + SG_WIDTH
OFF_GATE = OFF_VG + SG_WIDTH
D_FF_DENSE = 5504
N_EXPERTS = 8
D_FF_EXPERT = 7168
ALPHA = (2.0 * DEPTH) ** 0.25
LN_EPS = 1e-5
SB_UNDERFLOW = 110.0
LOG2_E = 1.4426950408889634

LANES = 128
V7X_VMEM_BYTES = 64 * 1024 * 1024

PROJ_TM = 1024
PROJ_TN = 1024
ATT_TQ = 256
ATT_TK = 256
ATT_HEADS = 8
SGU_ROWS = 512
MERGE_TM = 256
FFN_TM = 1024
FFN_TF = 256
LN_ROWS = 256
ROUTER_TM = 512
DISPATCH_TM = 512
MOE_TM = 256
MOE_GRANULES = 5
MOE_TF = 256
COMBINE_TM = 256


def _params(semantics, vmem_bytes):
    assert vmem_bytes < V7X_VMEM_BYTES
    return pltpu.CompilerParams(dimension_semantics=semantics, vmem_limit_bytes=vmem_bytes)


def _dot(a, b):
    return jnp.dot(a, b, preferred_element_type=F32)


def _layer_norm(y, g, b):
    mu = jnp.mean(y, axis=-1, keepdims=True)
    d = y - mu
    var = jnp.mean(d * d, axis=-1, keepdims=True)
    return d * lax.rsqrt(var + LN_EPS) * g + b


def _gelu_tanh(x):
    return 0.5 * x * (1.0 + jnp.tanh(0.7978845608028654 * (x + 0.044715 * (x * x * x))))


def _inproj_kernel(x_ref, w_ref, *rest, mode):
    *rest, w16_ref = rest

    @pl.when(pl.program_id(1) == 0)
    def _():
        w16_ref[...] = w_ref[...].astype(BF16)

    acc = _dot(x_ref[...], w16_ref[...])
    if mode == "qkv":
        (o_ref,) = rest
        scale = jnp.where(pl.program_id(0) == 0, HEAD_DIM ** -0.5 * LOG2_E, 1.0).astype(F32)
        o_ref[...] = (acc * scale).astype(o_ref.dtype)
    elif mode == "gelu":
        (o_ref,) = rest
        o_ref[...] = _gelu_tanh(acc).astype(o_ref.dtype)
    elif mode == "gelu_ln":
        g_ref, b_ref, o_ref = rest
        act = _gelu_tanh(acc)
        for grp in range(acc.shape[1] // GROUP_DIM):
            cols = slice(grp * GROUP_DIM, (grp + 1) * GROUP_DIM)
            o_ref[:, cols] = _layer_norm(act[:, cols], g_ref[:, cols], b_ref[:, cols]).astype(o_ref.dtype)
    elif mode == "gate":
        b_ref, o_ref = rest
        o_ref[...] = jax.nn.sigmoid(acc + b_ref[...]).astype(o_ref.dtype)
    else:
        raise ValueError(mode)


def _inproj(x16, w_in, layer, col_off, width, mode, extra=()):
    s, d = x16.shape
    tm, tn = PROJ_TM, PROJ_TN
    n_blk = width // tn
    off_blk = col_off // tn
    in_specs = [
        pl.BlockSpec((tm, d), lambda n, m: (m, 0)),
        pl.BlockSpec((None, d, tn), lambda n, m: (layer, 0, off_blk + n)),
    ]
    for _ in extra:
        in_specs.append(pl.BlockSpec((1, tn), lambda n, m: (0, n)))
    vmem = 2 * (tm * d * 2 + d * tn * 4 + tm * tn * 2) + d * tn * 2 + 6 * tm * tn * 4
    return pl.pallas_call(
        functools.partial(_inproj_kernel, mode=mode),
        grid=(n_blk, s // tm),
        in_specs=in_specs,
        out_specs=pl.BlockSpec((tm, tn), lambda n, m: (m, n)),
        out_shape=jax.ShapeDtypeStruct((s, width), BF16),
        scratch_shapes=[pltpu.VMEM((d, tn), BF16)],
        compiler_params=_params(("arbitrary", "arbitrary"), vmem),
        name="inproj_" + mode,
    )(x16, w_in, *extra)


def _attn_kernel(q_ref, k_ref, v_ref, o_ref, z_ref, ls_ref, hl_ref, tail_ref, a_ref, acc_ref, c_ref):
    tq, t = ATT_TQ, ATT_TK
    tiles_per_q = tq // t
    i = pl.program_id(1)
    krow = lax.broadcasted_iota(jnp.int32, (t, t), 0)
    kcol = lax.broadcasted_iota(jnp.int32, (t, t), 1)
    later = jnp.where(krow > kcol, 1.0, 0.0).astype(BF16)
    later2 = jnp.concatenate([later, later], axis=0)
    row = lax.broadcasted_iota(jnp.int32, (tq, t), 0)
    col = lax.broadcasted_iota(jnp.int32, (tq, t), 1)
    heads = [slice(h * HEAD_DIM, (h + 1) * HEAD_DIM) for h in range(ATT_HEADS)]

    def key_tile(j, mask):
        keys = pl.ds(pl.multiple_of(j * t, t), t)
        for h, hs in enumerate(heads):
            z_ref[h] = lax.dot_general(q_ref[:, hs], k_ref[keys, hs], (((1,), (1,)), ((), ())),
                                       preferred_element_type=F32)
        for h in range(ATT_HEADS):
            z = z_ref[h]
            ls = jnp.minimum(z, 0.0) - jnp.log2(1.0 + jnp.exp2(-jnp.abs(z)))
            lk = ls - z
            if mask is not None:
                lk = jnp.where(mask, lk, 0.0)
            hi = lk.astype(BF16)
            ls_ref[h] = ls
            hl_ref[h, :, :t] = hi
            hl_ref[h, :, t:] = (lk - hi.astype(F32)).astype(BF16)
        for h in range(ATT_HEADS):
            tail_ref[h] = _dot(hl_ref[h], later2)
        for h, hs in enumerate(heads):
            tail = tail_ref[h]
            a = jnp.exp2(ls_ref[h] + tail + jnp.tile(c_ref[:, hs], (1, t // HEAD_DIM)))
            lk0 = ls_ref[h][:, :1] - z_ref[h][:, :1]
            if mask is not None:
                a = jnp.where(mask, a, 0.0)
                lk0 = jnp.where(mask[:, :1], lk0, 0.0)
            a_ref[h] = a.astype(BF16)
            c_ref[:, hs] += jnp.broadcast_to(tail[:, :1] + lk0, (tq, HEAD_DIM))
        for h, hs in enumerate(heads):
            acc_ref[:, hs] += _dot(a_ref[h], v_ref[keys, hs])

    def any_weight_left():
        return jnp.max(c_ref[...]) > -SB_UNDERFLOW * LOG2_E

    acc_ref[...] = jnp.zeros_like(acc_ref)
    c_ref[...] = jnp.zeros_like(c_ref)
    for back in range(tiles_per_q):
        offset = (tiles_per_q - 1 - back) * t
        key_tile(i * tiles_per_q + (tiles_per_q - 1 - back), col + offset < row)

    def cond(carry):
        j, go = carry
        return jnp.logical_and(j >= 0, go)

    def body(carry):
        j, _ = carry
        key_tile(j, None)
        return j - 1, any_weight_left()

    lax.while_loop(cond, body, (i * tiles_per_q - 1, any_weight_left()))
    o_ref[...] = acc_ref[...].astype(o_ref.dtype)


def _attention(qkv):
    s = qkv.shape[0]
    tq, t = ATT_TQ, ATT_TK
    nh = ATT_HEADS
    w = nh * HEAD_DIM
    groups = SB_WIDTH // w
    resident = pl.Buffered(1)
    vmem = 2 * s * w * 2 + 2 * (2 * tq * w * 2) + nh * 16 * tq * t + 2 * tq * w * 4 + (8 << 20)
    return pl.pallas_call(
        _attn_kernel,
        grid=(groups, s // tq),
        in_specs=[
            pl.BlockSpec((tq, w), lambda g, i: (i, g)),
            pl.BlockSpec((s, w), lambda g, i: (0, groups + g), pipeline_mode=resident),
            pl.BlockSpec((s, w), lambda g, i: (0, 2 * groups + g), pipeline_mode=resident),
        ],
        out_specs=pl.BlockSpec((tq, w), lambda g, i: (i, g)),
        out_shape=jax.ShapeDtypeStruct((s, SB_WIDTH), BF16),
        scratch_shapes=[
            pltpu.VMEM((nh, tq, t), F32),
            pltpu.VMEM((nh, tq, t), F32),
            pltpu.VMEM((nh, tq, 2 * t), BF16),
            pltpu.VMEM((nh, tq, t), F32),
            pltpu.VMEM((nh, tq, t), BF16),
            pltpu.VMEM((tq, w), F32),
            pltpu.VMEM((tq, w), F32),
        ],
        compiler_params=_params(("arbitrary", "arbitrary"), vmem),
        name="stickbreak_attn",
    )(qkv, qkv, qkv)


def _sgu_kernel(u_ref, v_ref, w_ref, b_ref, o_ref):
    c = CHUNK
    row = lax.broadcasted_iota(jnp.int32, (c, c), 0)
    col = lax.broadcasted_iota(jnp.int32, (c, c), 1)
    causal = col <= row
    for g in range(GROUPS):
        w = jnp.where(causal, w_ref[g], 0.0).astype(BF16)
        b = b_ref[:, g:g + 1]
        cols = slice(g * GROUP_DIM, (g + 1) * GROUP_DIM)
        for cc in range(SGU_ROWS // c):
            rows = slice(cc * c, (cc + 1) * c)
            mixed = _dot(w, v_ref[rows, cols]) + b
            o_ref[rows, cols] = (u_ref[rows, cols].astype(F32) * mixed).astype(o_ref.dtype)


def _sgu(u, vn, sg_w, sg_b_t):
    s = u.shape[0]
    r = SGU_ROWS
    vmem = 2 * 3 * r * SG_WIDTH * 2 + 2 * GROUPS * CHUNK * CHUNK * 4 + (4 << 20)
    return pl.pallas_call(
        _sgu_kernel,
        grid=(s // r,),
        in_specs=[
            pl.BlockSpec((r, SG_WIDTH), lambda i: (i, 0)),
            pl.BlockSpec((r, SG_WIDTH), lambda i: (i, 0)),
            pl.BlockSpec((GROUPS, CHUNK, CHUNK), lambda i: (0, 0, 0)),
            pl.BlockSpec((CHUNK, GROUPS), lambda i: (0, 0)),
        ],
        out_specs=pl.BlockSpec((r, SG_WIDTH), lambda i: (i, 0)),
        out_shape=jax.ShapeDtypeStruct((s, SG_WIDTH), BF16),
        compiler_params=_params(("arbitrary",), vmem),
        name="spatial_gating",
    )(u, vn, sg_w, sg_b_t)


def _merge_kernel(a_ref, b_ref, gate_ref, x_ref, wa_ref, wb_ref, wo_ref, g_ref, beta_ref,
                  o32_ref, o16_ref):
    d = D_MODEL
    ya = _dot(a_ref[...], wa_ref[...])
    yb = _dot(b_ref[...], wb_ref[...])
    merged = gate_ref[:, :d].astype(F32) * ya + gate_ref[:, d:].astype(F32) * yb
    mix = _dot(merged.astype(BF16), wo_ref[...])
    y = _layer_norm(ALPHA * x_ref[...] + mix, g_ref[...], beta_ref[...])
    o32_ref[...] = y
    o16_ref[...] = y.astype(BF16)


def _merge(att, sgu, gates, x32, wa, wb, wo, ln_g, ln_b):
    s, d = x32.shape
    tm = MERGE_TM
    const = lambda i: (0, 0)
    rowblk = lambda i: (i, 0)
    weights = (SB_WIDTH * d + SG_WIDTH * d + d * d) * 2
    vmem = 2 * weights + 2 * tm * (SB_WIDTH * 2 + SG_WIDTH * 2 + 2 * d * 2 + d * 4 + d * 4 + d * 2) \
        + 6 * tm * d * 4
    return pl.pallas_call(
        _merge_kernel,
        grid=(s // tm,),
        in_specs=[
            pl.BlockSpec((tm, SB_WIDTH), rowblk),
            pl.BlockSpec((tm, SG_WIDTH), rowblk),
            pl.BlockSpec((tm, 2 * d), rowblk),
            pl.BlockSpec((tm, d), rowblk),
            pl.BlockSpec((SB_WIDTH, d), const),
            pl.BlockSpec((SG_WIDTH, d), const),
            pl.BlockSpec((d, d), const),
            pl.BlockSpec((1, d), const),
            pl.BlockSpec((1, d), const),
        ],
        out_specs=[pl.BlockSpec((tm, d), rowblk), pl.BlockSpec((tm, d), rowblk)],
        out_shape=[jax.ShapeDtypeStruct((s, d), F32), jax.ShapeDtypeStruct((s, d), BF16)],
        compiler_params=_params(("arbitrary",), vmem),
        name="merge_outproj_ln",
    )(att, sgu, gates, x32, wa, wb, wo, ln_g, ln_b)


def _swiglu_hidden(x, w1_ref, w3_ref):
    h1 = _dot(x, w1_ref[...])
    return (h1 * jax.nn.sigmoid(h1) * _dot(x, w3_ref[...])).astype(BF16)


def _ffn_kernel(x16_ref, x32_ref, w1_ref, w3_ref, w2_ref, w1t_ref, w3t_ref, w2t_ref, g_ref, b_ref,
                o32_ref, o16_ref):
    f = pl.program_id(1)

    @pl.when(f == 0)
    def _():
        o32_ref[...] = jnp.zeros_like(o32_ref)

    o32_ref[...] += _dot(_swiglu_hidden(x16_ref[...], w1_ref, w3_ref), w2_ref[...])

    @pl.when(f == pl.num_programs(1) - 1)
    def _():
        o32_ref[...] += _dot(_swiglu_hidden(x16_ref[...], w1t_ref, w3t_ref), w2t_ref[...])

        def ln_rows(c, carry):
            rows = pl.ds(pl.multiple_of(c * LN_ROWS, LN_ROWS), LN_ROWS)
            y = _layer_norm(ALPHA * x32_ref[rows, :] + o32_ref[rows, :], g_ref[...], b_ref[...])
            o32_ref[rows, :] = y
            o16_ref[rows, :] = y.astype(BF16)
            return carry
        lax.fori_loop(0, o32_ref.shape[0] // LN_ROWS, ln_rows, 0)


def _ffn(x16, x32, w1, w3, w2, ln_g, ln_b):
    s, d = x32.shape
    ff = w1.shape[1]
    tm, tf = FFN_TM, FFN_TF
    n_f = ff // tf
    tail = ff - n_f * tf
    assert 0 < tail < tf and tail % LANES == 0 and (n_f * tf) % tail == 0
    tail_blk = (n_f * tf) // tail
    rowblk = lambda i, f: (i, 0)
    const = lambda i, f: (0, 0)
    vmem = tm * d * 4 + 2 * (tm * d * (2 + 4 + 2) + 3 * d * (tf + tail) * 2) + 5 * tm * tf * 4 \
        + 4 * LN_ROWS * d * 4
    return pl.pallas_call(
        _ffn_kernel,
        grid=(s // tm, n_f),
        in_specs=[
            pl.BlockSpec((tm, d), rowblk),
            pl.BlockSpec((tm, d), rowblk, pipeline_mode=pl.Buffered(1)),
            pl.BlockSpec((d, tf), lambda i, f: (0, f)),
            pl.BlockSpec((d, tf), lambda i, f: (0, f)),
            pl.BlockSpec((tf, d), lambda i, f: (f, 0)),
            pl.BlockSpec((d, tail), lambda i, f: (0, tail_blk)),
            pl.BlockSpec((d, tail), lambda i, f: (0, tail_blk)),
            pl.BlockSpec((tail, d), lambda i, f: (tail_blk, 0)),
            pl.BlockSpec((1, d), const),
            pl.BlockSpec((1, d), const),
        ],
        out_specs=[pl.BlockSpec((tm, d), rowblk), pl.BlockSpec((tm, d), rowblk)],
        out_shape=[jax.ShapeDtypeStruct((s, d), F32), jax.ShapeDtypeStruct((s, d), BF16)],
        compiler_params=_params(("arbitrary", "arbitrary"), vmem),
        name="swiglu_ln",
    )(x16, x32, w1, w3, w2, w1, w3, w2, ln_g, ln_b)


ROUTE_E, ROUTE_W, ROUTE_RANK = 0, 2, 4


def _router_kernel(x_ref, w_ref, route_ref, counts_ref, seen_ref):
    @pl.when(pl.program_id(0) == 0)
    def _():
        seen_ref[...] = jnp.zeros_like(seen_ref)

    logits = jnp.dot(x_ref[...], w_ref[...], preferred_element_type=F32,
                     precision=lax.Precision.HIGHEST)
    tm = logits.shape[0]
    lane = lax.broadcasted_iota(jnp.int32, logits.shape, 1).astype(F32)
    neg = jnp.float32(-jnp.inf)
    l1 = jnp.where(lane < N_EXPERTS, logits, neg)
    m1 = jnp.max(l1, axis=-1, keepdims=True)
    i1 = jnp.min(jnp.where(l1 == m1, lane, float(LANES)), axis=-1, keepdims=True)
    l2 = jnp.where(lane == i1, neg, l1)
    m2 = jnp.max(l2, axis=-1, keepdims=True)
    i2 = jnp.min(jnp.where(l2 == m2, lane, float(LANES)), axis=-1, keepdims=True)
    e2 = jnp.exp(m2 - m1)
    w_top = 1.0 / (1.0 + e2)

    chosen = jnp.where(lane == i1, 1.0, 0.0) + jnp.where(lane == i2, 1.0, 0.0)
    row = lax.broadcasted_iota(jnp.int32, (tm, tm), 0)
    col = lax.broadcasted_iota(jnp.int32, (tm, tm), 1)
    earlier = jnp.where(col < row, 1.0, 0.0).astype(BF16)
    prefix = _dot(earlier, chosen.astype(BF16)) + seen_ref[...]
    r1 = jnp.sum(jnp.where(lane == i1, prefix, 0.0), axis=-1, keepdims=True)
    r2 = jnp.sum(jnp.where(lane == i2, prefix, 0.0), axis=-1, keepdims=True)
    seen_ref[...] += jnp.sum(chosen, axis=0, keepdims=True)
    counts_ref[...] = seen_ref[...]

    fields = ((ROUTE_E, i1), (ROUTE_E + 1, i2), (ROUTE_W, w_top), (ROUTE_W + 1, e2 * w_top),
              (ROUTE_RANK, r1), (ROUTE_RANK + 1, r2))
    route = jnp.zeros_like(logits)
    for at, val in fields:
        route = jnp.where(lane == at, val, route)
    route_ref[...] = route


def _router(x32, w_router_padded):
    s, d = x32.shape
    tm = ROUTER_TM
    vmem = 2 * (tm * d * 4 + d * LANES * 4 + tm * LANES * 4) + 8 * tm * d * 4
    return pl.pallas_call(
        _router_kernel,
        grid=(s // tm,),
        in_specs=[pl.BlockSpec((tm, d), lambda i: (i, 0)), pl.BlockSpec((d, LANES), lambda i: (0, 0))],
        out_specs=[pl.BlockSpec((tm, LANES), lambda i: (i, 0)), pl.BlockSpec((1, LANES), lambda i: (0, 0))],
        out_shape=[jax.ShapeDtypeStruct((s, LANES), F32), jax.ShapeDtypeStruct((1, LANES), F32)],
        scratch_shapes=[pltpu.VMEM((1, LANES), F32)],
        compiler_params=_params(("arbitrary",), vmem),
        name="router_top2",
    )(x32, w_router_padded)


def _row_copy(src, src_row, dst, dst_row, sem):
    return pltpu.make_async_copy(src.at[pl.ds(src_row, 1)], dst.at[pl.ds(dst_row, 1)], sem)


def _pack_bf16_pairs(x):
    n = x.shape[1] // 2
    lo = lax.bitcast_convert_type(x[:, :n].astype(BF16).astype(F32), U32)
    hi = lax.bitcast_convert_type(x[:, n:].astype(BF16).astype(F32), U32)
    return (lo >> 16) | (hi & jnp.uint32(0xFFFF0000))


def _unpack_bf16_pairs(w):
    lo = lax.bitcast_convert_type(w << 16, F32).astype(BF16)
    hi = lax.bitcast_convert_type(w & jnp.uint32(0xFFFF0000), F32).astype(BF16)
    return lo, hi


def _dispatch_kernel(pos_ref, x_ref, xs_zero_hbm, xs_hbm, packed_ref, sem):
    del xs_zero_hbm
    tm = DISPATCH_TM
    packed_ref[...] = _pack_bf16_pairs(x_ref[...])

    def start(r, carry):
        for k in range(2):
            _row_copy(packed_ref, r, xs_hbm, pos_ref[0, 0, k * tm + r], sem).start(priority=k)
        return carry

    lax.fori_loop(0, tm, start, 0)
    for k in range(2):
        pltpu.make_async_copy(packed_ref, xs_hbm.at[pl.ds(0, tm)], sem).wait()


def _dispatch(pos_tiles, x32, n_rows):
    s, d = x32.shape
    tm = DISPATCH_TM
    return pl.pallas_call(
        _dispatch_kernel,
        grid=(s // tm,),
        in_specs=[
            pl.BlockSpec((1, 1, 2 * tm), lambda i: (i, 0, 0), memory_space=pltpu.SMEM),
            pl.BlockSpec((tm, d), lambda i: (i, 0)),
            pl.BlockSpec(memory_space=pl.ANY),
        ],
        out_specs=pl.BlockSpec(memory_space=pl.ANY),
        out_shape=jax.ShapeDtypeStruct((n_rows, d // 2), U32),
        scratch_shapes=[pltpu.VMEM((tm, d // 2), U32), pltpu.SemaphoreType.DMA(())],
        input_output_aliases={2: 0},
        compiler_params=_params(("arbitrary",), 2 * tm * d * 4 + 4 * tm * d * 2 + (4 << 20)),
        name="moe_dispatch",
    )(pos_tiles, x32, jnp.zeros((n_rows, d // 2), U32))


def _experts_kernel(tile_expert_ref, tile_rows_ref, n_used_ref, xs_ref, w1_ref, w3_ref, w2_ref, ys_ref,
                    x16_ref):
    del tile_expert_ref, n_used_ref
    f = pl.program_id(1)
    n_rows = tile_rows_ref[pl.program_id(0)]

    @pl.when(f == 0)
    def _():
        ys_ref[...] = jnp.zeros_like(ys_ref)

    def swiglu(rows):
        @pl.when(f == 0)
        def _():
            half = x16_ref.shape[1] // 2
            x16_ref[rows, :half], x16_ref[rows, half:] = _unpack_bf16_pairs(xs_ref[rows])

        x = x16_ref[rows]
        h1 = _dot(x, w1_ref[...].astype(BF16))
        h = (h1 * jax.nn.sigmoid(h1) * _dot(x, w3_ref[...].astype(BF16))).astype(BF16)
        ys_ref[rows] += _dot(h, w2_ref[...].astype(BF16))

    for g in range(1, MOE_GRANULES + 1):
        @pl.when((n_rows > (g - 1) * MOE_TM) & (n_rows <= g * MOE_TM))
        def _():
            swiglu(slice(0, g * MOE_TM))


def _experts(tile_expert, tile_rows, n_used, xs, w1, w3, w2):
    n_rows, d = xs.shape[0], w1.shape[1]
    ff = w1.shape[2]
    tm, tf = MOE_TM * MOE_GRANULES, MOE_TF
    n_f = ff // tf

    def rows(i, f, te, tr, nu):
        return (jnp.minimum(i, nu[0] - 1), 0)

    def chunk(i, f, nu):
        return jnp.where(i < nu[0], f, n_f - 1)

    vmem = tm * d * 2 + 2 * (tm * d * 4 + 3 * d * tf * 4) + tm * d * 2 + 3 * d * tf * 2 + 6 * tm * tf * 4
    return pl.pallas_call(
        _experts_kernel,
        grid_spec=pltpu.PrefetchScalarGridSpec(
            num_scalar_prefetch=3,
            grid=(n_rows // tm, n_f),
            in_specs=[
                pl.BlockSpec((tm, d // 2), rows, pipeline_mode=pl.Buffered(1)),
                pl.BlockSpec((None, d, tf), lambda i, f, te, tr, nu: (te[i], 0, chunk(i, f, nu))),
                pl.BlockSpec((None, d, tf), lambda i, f, te, tr, nu: (te[i], 0, chunk(i, f, nu))),
                pl.BlockSpec((None, tf, d), lambda i, f, te, tr, nu: (te[i], chunk(i, f, nu), 0)),
            ],
            out_specs=pl.BlockSpec((tm, d), lambda i, f, te, tr, nu: (i, 0)),
            scratch_shapes=[pltpu.VMEM((tm, d), BF16)],
        ),
        out_shape=jax.ShapeDtypeStruct((n_rows, d), F32),
        compiler_params=_params(("arbitrary", "arbitrary"), vmem),
        name="moe_experts",
    )(tile_expert, tile_rows, n_used, xs, w1, w3, w2)


def _combine_kernel(pos_ref, pos_next_ref, route_ref, x_ref, ys_hbm, g_ref, b_ref, o_ref, buf, sems):
    tm = COMBINE_TM
    i = pl.program_id(0)
    slot = i % 2

    def start_tile(p_ref, slot):
        def start(r, carry):
            for k in range(2):
                _row_copy(ys_hbm, p_ref[0, 0, k * tm + r], buf.at[slot, k], r, sems.at[slot]).start(priority=k)
            return carry
        lax.fori_loop(0, tm, start, 0)

    @pl.when(i == 0)
    def _():
        start_tile(pos_ref, 0)

    @pl.when(i + 1 < pl.num_programs(0))
    def _():
        start_tile(pos_next_ref, 1 - slot)

    for k in range(2):
        pltpu.make_async_copy(ys_hbm.at[pl.ds(0, tm)], buf.at[slot, k], sems.at[slot]).wait()
    w0 = route_ref[:, ROUTE_W:ROUTE_W + 1]
    w1 = route_ref[:, ROUTE_W + 1:ROUTE_W + 2]
    y = w0 * buf[slot, 0] + w1 * buf[slot, 1]
    o_ref[...] = _layer_norm(ALPHA * x_ref[...] + y, g_ref[...], b_ref[...])


def _combine(pos_tiles, route, x32, ys, ln_g, ln_b):
    s, d = x32.shape
    tm = COMBINE_TM
    n = s // tm
    rowblk = lambda i: (i, 0)
    const = lambda i: (0, 0)
    vmem = 2 * 2 * tm * d * 4 + 2 * (2 * tm * d * 4 + tm * LANES * 4) + 6 * tm * d * 4
    return pl.pallas_call(
        _combine_kernel,
        grid=(n,),
        in_specs=[
            pl.BlockSpec((1, 1, 2 * tm), lambda i: (i, 0, 0), memory_space=pltpu.SMEM),
            pl.BlockSpec((1, 1, 2 * tm), lambda i: (jnp.minimum(i + 1, n - 1), 0, 0),
                         memory_space=pltpu.SMEM),
            pl.BlockSpec((tm, LANES), rowblk),
            pl.BlockSpec((tm, d), rowblk),
            pl.BlockSpec(memory_space=pl.ANY),
            pl.BlockSpec((1, d), const),
            pl.BlockSpec((1, d), const),
        ],
        out_specs=pl.BlockSpec((tm, d), rowblk),
        out_shape=jax.ShapeDtypeStruct((s, d), F32),
        scratch_shapes=[pltpu.VMEM((2, 2, tm, d), F32), pltpu.SemaphoreType.DMA((2,))],
        compiler_params=_params(("arbitrary",), vmem),
        name="moe_combine_ln",
    )(pos_tiles, pos_tiles, route, x32, ys, ln_g, ln_b)


def _pos_tiles(pos, tm):
    s = pos.shape[0]
    return pos.reshape(s // tm, tm, 2).transpose(0, 2, 1).reshape(s // tm, 1, 2 * tm)


def _moe(x32, w_router, w1, w3, w2, ln_g, ln_b):
    s, d = x32.shape
    tm = MOE_TM * MOE_GRANULES
    n_tiles = (2 * s) // tm + N_EXPERTS
    route, counts = _router(x32, jnp.pad(w_router, ((0, 0), (0, LANES - N_EXPERTS))))

    counts = counts[0, :N_EXPERTS].astype(jnp.int32)
    tiles_per_expert = (counts + tm - 1) // tm
    tile_end = jnp.cumsum(tiles_per_expert)
    tile_start = tile_end - tiles_per_expert
    n_used = tile_end[-1:]
    tile_ids = jnp.arange(n_tiles, dtype=jnp.int32)
    tile_expert = jnp.sum(jnp.minimum(tile_ids, n_used - 1)[:, None] >= tile_end[None, :],
                          axis=1).astype(jnp.int32)
    tile_rows = jnp.clip(counts[tile_expert] - (tile_ids - tile_start[tile_expert]) * tm, 0, tm)
    tile_rows = jnp.where(tile_ids < n_used, tile_rows, 0).astype(jnp.int32)
    experts = route[:, ROUTE_E:ROUTE_E + 2].astype(jnp.int32)
    pos = (tile_start * tm)[experts] + route[:, ROUTE_RANK:ROUTE_RANK + 2].astype(jnp.int32)

    xs = _dispatch(_pos_tiles(pos, DISPATCH_TM), x32, n_tiles * tm)
    ys = _experts(tile_expert, tile_rows, n_used, xs, w1, w3, w2)
    return _combine(_pos_tiles(pos, COMBINE_TM), route, x32, ys, ln_g, ln_b)


def kernel(x, w_in, b_gate, sg_w, sg_b, sg_ln_g, sg_ln_b, w_branch_a, w_branch_b, w_out,
           ln1_g, ln1_b, ffn_w1, ffn_w3, ffn_w2, moe_router, moe_w1, moe_w3, moe_w2,
           ln2_g, ln2_b):
    b, s, d = x.shape
    assert (b, s, d) == (1, SEQ, D_MODEL)
    x32 = x.reshape(s, d)
    x16 = x32.astype(BF16)
    for layer in range(DEPTH):
        qkv = _inproj(x16, w_in, layer, 0, 3 * SB_WIDTH, "qkv")
        u = _inproj(x16, w_in, layer, OFF_U, SG_WIDTH, "gelu")
        vn = _inproj(x16, w_in, layer, OFF_VG, SG_WIDTH, "gelu_ln",
                     (sg_ln_g[layer].reshape(1, -1), sg_ln_b[layer].reshape(1, -1)))
        gates = _inproj(x16, w_in, layer, OFF_GATE, 2 * d, "gate", (b_gate[layer].reshape(1, -1),))
        att = _attention(qkv)
        sgu = _sgu(u, vn, sg_w[layer], sg_b[layer].T)
        x32, x16 = _merge(att, sgu, gates, x32,
                          w_branch_a[layer].astype(BF16), w_branch_b[layer].astype(BF16),
                          w_out[layer].astype(BF16),
                          ln1_g[layer].reshape(1, d), ln1_b[layer].reshape(1, d))
        i = layer // 2
        g2, b2 = ln2_g[layer].reshape(1, d), ln2_b[layer].reshape(1, d)
        if layer % 2 == 0:
            x32, x16 = _ffn(x16, x32, ffn_w1[i].astype(BF16), ffn_w3[i].astype(BF16),
                            ffn_w2[i].astype(BF16), g2, b2)
        else:
            x32 = _moe(x32, moe_router[i], moe_w1[i], moe_w3[i], moe_w2[i], g2, b2)
            x16 = x32.astype(BF16) if layer + 1 < DEPTH else None
    return x32.reshape(b, s, d)
```

```python
import functools

import jax
import jax.numpy as jnp
from jax import lax
from jax.experimental import pallas as pl
from jax.experimental.pallas import tpu as pltpu

F32 = jnp.float32
BF16 = jnp.bfloat16
U32 = jnp.uint32

D_MODEL = 2048
SEQ = 8192
DEPTH = 2
HEADS = 8
HEAD_DIM = 128
SB_WIDTH = HEADS * HEAD_DIM
GROUPS = 8
GROUP_DIM = 128
SG_WIDTH = GROUPS * GROUP_DIM
CHUNK = 128
OFF_U = 3 * SB_WIDTH
OFF_VG = OFF_U + SG_WIDTH
OFF_GATE = OFF_VG + SG_WIDTH
D_FF_DENSE = 5504
N_EXPERTS = 8
D_FF_EXPERT = 7168
ALPHA = (2.0 * DEPTH) ** 0.25
LN_EPS = 1e-5
SB_UNDERFLOW = 110.0
LOG2_E = 1.4426950408889634

LANES = 128
V7X_VMEM_BYTES = 64 * 1024 * 1024

PROJ_TM = 1024
PROJ_TN = 1024
ATT_TQ = 256
ATT_TK = 256
ATT_HEADS = 8
SGU_ROWS = 512
MERGE_TM = 256
FFN_TM = 1024
FFN_TF = 256
LN_ROWS = 256
ROUTER_TM = 512
DISPATCH_TM = 512
MOE_TM = 256
MOE_GRANULES = 5
MOE_TF = 256
COMBINE_TM = 256
ROW_DMA_UNROLL = 8


def _params(semantics, vmem_bytes):
    assert vmem_bytes < V7X_VMEM_BYTES
    return pltpu.CompilerParams(dimension_semantics=semantics, vmem_limit_bytes=vmem_bytes)


def _dot(a, b):
    return jnp.dot(a, b, preferred_element_type=F32)


def _layer_norm(y, g, b):
    mu = jnp.mean(y, axis=-1, keepdims=True)
    d = y - mu
    var = jnp.mean(d * d, axis=-1, keepdims=True)
    return d * lax.rsqrt(var + LN_EPS) * g + b


def _gelu_tanh(x):
    return 0.5 * x * (1.0 + jnp.tanh(0.7978845608028654 * (x + 0.044715 * (x * x * x))))


def _inproj_kernel(x_ref, w_ref, *rest, mode):
    *rest, w16_ref = rest

    @pl.when(pl.program_id(1) == 0)
    def _():
        w16_ref[...] = w_ref[...].astype(BF16)

    acc = _dot(x_ref[...], w16_ref[...])
    if mode == "qkv":
        (o_ref,) = rest
        scale = jnp.where(pl.program_id(0) == 0, HEAD_DIM ** -0.5 * LOG2_E, 1.0).astype(F32)
        o_ref[...] = (acc * scale).astype(o_ref.dtype)
    elif mode == "gelu":
        (o_ref,) = rest
        o_ref[...] = _gelu_tanh(acc).astype(o_ref.dtype)
    elif mode == "gelu_ln":
        g_ref, b_ref, o_ref = rest
        act = _gelu_tanh(acc)
        for grp in range(acc.shape[1] // GROUP_DIM):
            cols = slice(grp * GROUP_DIM, (grp + 1) * GROUP_DIM)
            o_ref[:, cols] = _layer_norm(act[:, cols], g_ref[:, cols], b_ref[:, cols]).astype(o_ref.dtype)
    elif mode == "gate":
        b_ref, o_ref = rest
        o_ref[...] = jax.nn.sigmoid(acc + b_ref[...]).astype(o_ref.dtype)
    else:
        raise ValueError(mode)


def _inproj(x16, w_in, layer, col_off, width, mode, extra=()):
    s, d = x16.shape
    tm, tn = PROJ_TM, PROJ_TN
    n_blk = width // tn
    off_blk = col_off // tn
    in_specs = [
        pl.BlockSpec((tm, d), lambda n, m: (m, 0)),
        pl.BlockSpec((None, d, tn), lambda n, m: (layer, 0, off_blk + n)),
    ]
    for _ in extra:
        in_specs.append(pl.BlockSpec((1, tn), lambda n, m: (0, n)))
    vmem = 2 * (tm * d * 2 + d * tn * 4 + tm * tn * 2) + d * tn * 2 + 6 * tm * tn * 4
    return pl.pallas_call(
        functools.partial(_inproj_kernel, mode=mode),
        grid=(n_blk, s // tm),
        in_specs=in_specs,
        out_specs=pl.BlockSpec((tm, tn), lambda n, m: (m, n)),
        out_shape=jax.ShapeDtypeStruct((s, width), BF16),
        scratch_shapes=[pltpu.VMEM((d, tn), BF16)],
        compiler_params=_params(("arbitrary", "arbitrary"), vmem),
        name="inproj_" + mode,
    )(x16, w_in, *extra)


def _attn_kernel(q_ref, k_ref, v_ref, o_ref, z_ref, ls_ref, hl_ref, tail_ref, a_ref, acc_ref, c_ref):
    tq, t = ATT_TQ, ATT_TK
    tiles_per_q = tq // t
    i = pl.program_id(1)
    krow = lax.broadcasted_iota(jnp.int32, (t, t), 0)
    kcol = lax.broadcasted_iota(jnp.int32, (t, t), 1)
    later = jnp.where(krow > kcol, 1.0, 0.0).astype(BF16)
    later2 = jnp.concatenate([later, later], axis=0)
    row = lax.broadcasted_iota(jnp.int32, (tq, t), 0)
    col = lax.broadcasted_iota(jnp.int32, (tq, t), 1)
    heads = [slice(h * HEAD_DIM, (h + 1) * HEAD_DIM) for h in range(ATT_HEADS)]

    def key_tile(j, mask):
        keys = pl.ds(pl.multiple_of(j * t, t), t)
        for h, hs in enumerate(heads):
            z_ref[h] = lax.dot_general(q_ref[:, hs], k_ref[keys, hs], (((1,), (1,)), ((), ())),
                                       preferred_element_type=F32)
        for h in range(ATT_HEADS):
            z = z_ref[h]
            ls = jnp.minimum(z, 0.0) - jnp.log2(1.0 + jnp.exp2(-jnp.abs(z)))
            lk = ls - z
            if mask is not None:
                lk = jnp.where(mask, lk, 0.0)
            hi = lk.astype(BF16)
            ls_ref[h] = ls
            hl_ref[h, :, :t] = hi
            hl_ref[h, :, t:] = (lk - hi.astype(F32)).astype(BF16)
        for h in range(ATT_HEADS):
            tail_ref[h] = _dot(hl_ref[h], later2)
        for h, hs in enumerate(heads):
            tail = tail_ref[h]
            a = jnp.exp2(ls_ref[h] + tail + jnp.tile(c_ref[:, hs], (1, t // HEAD_DIM)))
            lk0 = ls_ref[h][:, :1] - z_ref[h][:, :1]
            if mask is not None:
                a = jnp.where(mask, a, 0.0)
                lk0 = jnp.where(mask[:, :1], lk0, 0.0)
            a_ref[h] = a.astype(BF16)
            c_ref[:, hs] += jnp.broadcast_to(tail[:, :1] + lk0, (tq, HEAD_DIM))
        for h, hs in enumerate(heads):
            acc_ref[:, hs] += _dot(a_ref[h], v_ref[keys, hs])

    def any_weight_left():
        return jnp.max(c_ref[...]) > -SB_UNDERFLOW * LOG2_E

    acc_ref[...] = jnp.zeros_like(acc_ref)
    c_ref[...] = jnp.zeros_like(c_ref)
    for back in range(tiles_per_q):
        offset = (tiles_per_q - 1 - back) * t
        key_tile(i * tiles_per_q + (tiles_per_q - 1 - back), col + offset < row)

    def cond(carry):
        j, go = carry
        return jnp.logical_and(j >= 0, go)

    def body(carry):
        j, _ = carry
        key_tile(j, None)
        return j - 1, any_weight_left()

    lax.while_loop(cond, body, (i * tiles_per_q - 1, any_weight_left()))
    o_ref[...] = acc_ref[...].astype(o_ref.dtype)


def _attention(qkv):
    s = qkv.shape[0]
    tq, t = ATT_TQ, ATT_TK
    nh = ATT_HEADS
    w = nh * HEAD_DIM
    groups = SB_WIDTH // w
    resident = pl.Buffered(1)
    vmem = 2 * s * w * 2 + 2 * (2 * tq * w * 2) + nh * 16 * tq * t + 2 * tq * w * 4 + (8 << 20)
    return pl.pallas_call(
        _attn_kernel,
        grid=(groups, s // tq),
        in_specs=[
            pl.BlockSpec((tq, w), lambda g, i: (i, g)),
            pl.BlockSpec((s, w), lambda g, i: (0, groups + g), pipeline_mode=resident),
            pl.BlockSpec((s, w), lambda g, i: (0, 2 * groups + g), pipeline_mode=resident),
        ],
        out_specs=pl.BlockSpec((tq, w), lambda g, i: (i, g)),
        out_shape=jax.ShapeDtypeStruct((s, SB_WIDTH), BF16),
        scratch_shapes=[
            pltpu.VMEM((nh, tq, t), F32),
            pltpu.VMEM((nh, tq, t), F32),
            pltpu.VMEM((nh, tq, 2 * t), BF16),
            pltpu.VMEM((nh, tq, t), F32),
            pltpu.VMEM((nh, tq, t), BF16),
            pltpu.VMEM((tq, w), F32),
            pltpu.VMEM((tq, w), F32),
        ],
        compiler_params=_params(("arbitrary", "arbitrary"), vmem),
        name="stickbreak_attn",
    )(qkv, qkv, qkv)


def _sgu_kernel(u_ref, v_ref, w_ref, b_ref, o_ref):
    c = CHUNK
    row = lax.broadcasted_iota(jnp.int32, (c, c), 0)
    col = lax.broadcasted_iota(jnp.int32, (c, c), 1)
    causal = col <= row
    for g in range(GROUPS):
        w = jnp.where(causal, w_ref[g], 0.0).astype(BF16)
        b = b_ref[:, g:g + 1]
        cols = slice(g * GROUP_DIM, (g + 1) * GROUP_DIM)
        for cc in range(SGU_ROWS // c):
            rows = slice(cc * c, (cc + 1) * c)
            mixed = _dot(w, v_ref[rows, cols]) + b
            o_ref[rows, cols] = (u_ref[rows, cols].astype(F32) * mixed).astype(o_ref.dtype)


def _sgu(u, vn, sg_w, sg_b_t):
    s = u.shape[0]
    r = SGU_ROWS
    vmem = 2 * 3 * r * SG_WIDTH * 2 + 2 * GROUPS * CHUNK * CHUNK * 4 + (4 << 20)
    return pl.pallas_call(
        _sgu_kernel,
        grid=(s // r,),
        in_specs=[
            pl.BlockSpec((r, SG_WIDTH), lambda i: (i, 0)),
            pl.BlockSpec((r, SG_WIDTH), lambda i: (i, 0)),
            pl.BlockSpec((GROUPS, CHUNK, CHUNK), lambda i: (0, 0, 0)),
            pl.BlockSpec((CHUNK, GROUPS), lambda i: (0, 0)),
        ],
        out_specs=pl.BlockSpec((r, SG_WIDTH), lambda i: (i, 0)),
        out_shape=jax.ShapeDtypeStruct((s, SG_WIDTH), BF16),
        compiler_params=_params(("arbitrary",), vmem),
        name="spatial_gating",
    )(u, vn, sg_w, sg_b_t)


def _merge_kernel(a_ref, b_ref, gate_ref, x_ref, wa_ref, wb_ref, wo_ref, g_ref, beta_ref,
                  o32_ref, o16_ref):
    d = D_MODEL
    ya = _dot(a_ref[...], wa_ref[...])
    yb = _dot(b_ref[...], wb_ref[...])
    merged = gate_ref[:, :d].astype(F32) * ya + gate_ref[:, d:].astype(F32) * yb
    mix = _dot(merged.astype(BF16), wo_ref[...])
    y = _layer_norm(ALPHA * x_ref[...] + mix, g_ref[...], beta_ref[...])
    o32_ref[...] = y
    o16_ref[...] = y.astype(BF16)


def _merge(att, sgu, gates, x32, wa, wb, wo, ln_g, ln_b):
    s, d = x32.shape
    tm = MERGE_TM
    const = lambda i: (0, 0)
    rowblk = lambda i: (i, 0)
    weights = (SB_WIDTH * d + SG_WIDTH * d + d * d) * 2
    vmem = 2 * weights + 2 * tm * (SB_WIDTH * 2 + SG_WIDTH * 2 + 2 * d * 2 + d * 4 + d * 4 + d * 2) \
        + 6 * tm * d * 4
    return pl.pallas_call(
        _merge_kernel,
        grid=(s // tm,),
        in_specs=[
            pl.BlockSpec((tm, SB_WIDTH), rowblk),
            pl.BlockSpec((tm, SG_WIDTH), rowblk),
            pl.BlockSpec((tm, 2 * d), rowblk),
            pl.BlockSpec((tm, d), rowblk),
            pl.BlockSpec((SB_WIDTH, d), const),
            pl.BlockSpec((SG_WIDTH, d), const),
            pl.BlockSpec((d, d), const),
            pl.BlockSpec((1, d), const),
            pl.BlockSpec((1, d), const),
        ],
        out_specs=[pl.BlockSpec((tm, d), rowblk), pl.BlockSpec((tm, d), rowblk)],
        out_shape=[jax.ShapeDtypeStruct((s, d), F32), jax.ShapeDtypeStruct((s, d), BF16)],
        compiler_params=_params(("arbitrary",), vmem),
        name="merge_outproj_ln",
    )(att, sgu, gates, x32, wa, wb, wo, ln_g, ln_b)


def _swiglu_hidden(x, w1_ref, w3_ref):
    h1 = _dot(x, w1_ref[...])
    return (h1 * jax.nn.sigmoid(h1) * _dot(x, w3_ref[...])).astype(BF16)


def _ffn_kernel(x16_ref, x32_ref, w1_ref, w3_ref, w2_ref, w1t_ref, w3t_ref, w2t_ref, g_ref, b_ref,
                o32_ref, o16_ref):
    f = pl.program_id(1)

    @pl.when(f == 0)
    def _():
        o32_ref[...] = jnp.zeros_like(o32_ref)

    o32_ref[...] += _dot(_swiglu_hidden(x16_ref[...], w1_ref, w3_ref), w2_ref[...])

    @pl.when(f == pl.num_programs(1) - 1)
    def _():
        o32_ref[...] += _dot(_swiglu_hidden(x16_ref[...], w1t_ref, w3t_ref), w2t_ref[...])

        def ln_rows(c, carry):
            rows = pl.ds(pl.multiple_of(c * LN_ROWS, LN_ROWS), LN_ROWS)
            y = _layer_norm(ALPHA * x32_ref[rows, :] + o32_ref[rows, :], g_ref[...], b_ref[...])
            o32_ref[rows, :] = y
            o16_ref[rows, :] = y.astype(BF16)
            return carry
        lax.fori_loop(0, o32_ref.shape[0] // LN_ROWS, ln_rows, 0)


def _ffn(x16, x32, w1, w3, w2, ln_g, ln_b):
    s, d = x32.shape
    ff = w1.shape[1]
    tm, tf = FFN_TM, FFN_TF
    n_f = ff // tf
    tail = ff - n_f * tf
    assert 0 < tail < tf and tail % LANES == 0 and (n_f * tf) % tail == 0
    tail_blk = (n_f * tf) // tail
    rowblk = lambda i, f: (i, 0)
    const = lambda i, f: (0, 0)
    vmem = tm * d * 4 + 2 * (tm * d * (2 + 4 + 2) + 3 * d * (tf + tail) * 2) + 5 * tm * tf * 4 \
        + 4 * LN_ROWS * d * 4
    return pl.pallas_call(
        _ffn_kernel,
        grid=(s // tm, n_f),
        in_specs=[
            pl.BlockSpec((tm, d), rowblk),
            pl.BlockSpec((tm, d), rowblk, pipeline_mode=pl.Buffered(1)),
            pl.BlockSpec((d, tf), lambda i, f: (0, f)),
            pl.BlockSpec((d, tf), lambda i, f: (0, f)),
            pl.BlockSpec((tf, d), lambda i, f: (f, 0)),
            pl.BlockSpec((d, tail), lambda i, f: (0, tail_blk)),
            pl.BlockSpec((d, tail), lambda i, f: (0, tail_blk)),
            pl.BlockSpec((tail, d), lambda i, f: (tail_blk, 0)),
            pl.BlockSpec((1, d), const),
            pl.BlockSpec((1, d), const),
        ],
        out_specs=[pl.BlockSpec((tm, d), rowblk), pl.BlockSpec((tm, d), rowblk)],
        out_shape=[jax.ShapeDtypeStruct((s, d), F32), jax.ShapeDtypeStruct((s, d), BF16)],
        compiler_params=_params(("arbitrary", "arbitrary"), vmem),
        name="swiglu_ln",
    )(x16, x32, w1, w3, w2, w1, w3, w2, ln_g, ln_b)


ROUTE_E, ROUTE_W, ROUTE_RANK = 0, 2, 4


def _router_kernel(x_ref, w_ref, route_ref, counts_ref, seen_ref):
    @pl.when(pl.program_id(0) == 0)
    def _():
        seen_ref[...] = jnp.zeros_like(seen_ref)

    x, w = x_ref[...], w_ref[...]
    x_hi, w_hi = x.astype(BF16), w.astype(BF16)
    x_lo, w_lo = (x - x_hi.astype(F32)).astype(BF16), (w - w_hi.astype(F32)).astype(BF16)
    logits = _dot(x_hi, w_hi) + (_dot(x_hi, w_lo) + _dot(x_lo, w_hi))
    tm = logits.shape[0]
    lane = lax.broadcasted_iota(jnp.int32, logits.shape, 1).astype(F32)
    neg = jnp.float32(-jnp.inf)
    l1 = jnp.where(lane < N_EXPERTS, logits, neg)
    m1 = jnp.max(l1, axis=-1, keepdims=True)
    i1 = jnp.min(jnp.where(l1 == m1, lane, float(LANES)), axis=-1, keepdims=True)
    l2 = jnp.where(lane == i1, neg, l1)
    m2 = jnp.max(l2, axis=-1, keepdims=True)
    i2 = jnp.min(jnp.where(l2 == m2, lane, float(LANES)), axis=-1, keepdims=True)
    e2 = jnp.exp(m2 - m1)
    w_top = 1.0 / (1.0 + e2)

    chosen = jnp.where(lane == i1, 1.0, 0.0) + jnp.where(lane == i2, 1.0, 0.0)
    row = lax.broadcasted_iota(jnp.int32, (tm, tm), 0)
    col = lax.broadcasted_iota(jnp.int32, (tm, tm), 1)
    earlier = jnp.where(col < row, 1.0, 0.0).astype(BF16)
    prefix = _dot(earlier, chosen.astype(BF16)) + seen_ref[...]
    r1 = jnp.sum(jnp.where(lane == i1, prefix, 0.0), axis=-1, keepdims=True)
    r2 = jnp.sum(jnp.where(lane == i2, prefix, 0.0), axis=-1, keepdims=True)
    seen_ref[...] += jnp.sum(chosen, axis=0, keepdims=True)
    counts_ref[...] = seen_ref[...]

    fields = ((ROUTE_E, i1), (ROUTE_E + 1, i2), (ROUTE_W, w_top), (ROUTE_W + 1, e2 * w_top),
              (ROUTE_RANK, r1), (ROUTE_RANK + 1, r2))
    route = jnp.zeros_like(logits)
    for at, val in fields:
        route = jnp.where(lane == at, val, route)
    route_ref[...] = route


def _router(x32, w_router_padded):
    s, d = x32.shape
    tm = ROUTER_TM
    vmem = 2 * (tm * d * 4 + d * LANES * 4 + tm * LANES * 4) + 8 * tm * d * 4
    return pl.pallas_call(
        _router_kernel,
        grid=(s // tm,),
        in_specs=[pl.BlockSpec((tm, d), lambda i: (i, 0)), pl.BlockSpec((d, LANES), lambda i: (0, 0))],
        out_specs=[pl.BlockSpec((tm, LANES), lambda i: (i, 0)), pl.BlockSpec((1, LANES), lambda i: (0, 0))],
        out_shape=[jax.ShapeDtypeStruct((s, LANES), F32), jax.ShapeDtypeStruct((1, LANES), F32)],
        scratch_shapes=[pltpu.VMEM((1, LANES), F32)],
        compiler_params=_params(("arbitrary",), vmem),
        name="router_top2",
    )(x32, w_router_padded)


def _row_copy(src, src_row, dst, dst_row, sem):
    return pltpu.make_async_copy(src.at[pl.ds(src_row, 1)], dst.at[pl.ds(dst_row, 1)], sem)


def _pack_bf16_pairs(x):
    n = x.shape[1] // 2
    lo = lax.bitcast_convert_type(x[:, :n].astype(BF16).astype(F32), U32)
    hi = lax.bitcast_convert_type(x[:, n:].astype(BF16).astype(F32), U32)
    return (lo >> 16) | (hi & jnp.uint32(0xFFFF0000))


def _unpack_bf16_pairs(w):
    lo = lax.bitcast_convert_type(w << 16, F32).astype(BF16)
    hi = lax.bitcast_convert_type(w & jnp.uint32(0xFFFF0000), F32).astype(BF16)
    return lo, hi


def _dispatch_kernel(pos_ref, x_ref, xs_zero_hbm, xs_hbm, packed_ref, sem):
    del xs_zero_hbm
    tm = DISPATCH_TM
    packed_ref[...] = _pack_bf16_pairs(x_ref[...])

    def start(r, carry):
        for k in range(2):
            _row_copy(packed_ref, r, xs_hbm, pos_ref[0, 0, k * tm + r], sem).start(priority=k)
        return carry

    lax.fori_loop(0, tm, start, 0, unroll=ROW_DMA_UNROLL)
    for k in range(2):
        pltpu.make_async_copy(packed_ref, xs_hbm.at[pl.ds(0, tm)], sem).wait()


def _dispatch(pos_tiles, x32, n_rows):
    s, d = x32.shape
    tm = DISPATCH_TM
    return pl.pallas_call(
        _dispatch_kernel,
        grid=(s // tm,),
        in_specs=[
            pl.BlockSpec((1, 1, 2 * tm), lambda i: (i, 0, 0), memory_space=pltpu.SMEM),
            pl.BlockSpec((tm, d), lambda i: (i, 0)),
            pl.BlockSpec(memory_space=pl.ANY),
        ],
        out_specs=pl.BlockSpec(memory_space=pl.ANY),
        out_shape=jax.ShapeDtypeStruct((n_rows, d // 2), U32),
        scratch_shapes=[pltpu.VMEM((tm, d // 2), U32), pltpu.SemaphoreType.DMA(())],
        input_output_aliases={2: 0},
        compiler_params=_params(("arbitrary",), 2 * tm * d * 4 + 4 * tm * d * 2 + (4 << 20)),
        name="moe_dispatch",
    )(pos_tiles, x32, jnp.zeros((n_rows, d // 2), U32))


def _experts_kernel(tile_expert_ref, tile_rows_ref, n_used_ref, xs_ref, w1_ref, w3_ref, w2_ref, ys_ref,
                    x16_ref):
    del tile_expert_ref, n_used_ref
    f = pl.program_id(1)
    n_rows = tile_rows_ref[pl.program_id(0)]

    @pl.when(f == 0)
    def _():
        ys_ref[...] = jnp.zeros_like(ys_ref)

    def swiglu(rows):
        @pl.when(f == 0)
        def _():
            half = x16_ref.shape[1] // 2
            x16_ref[rows, :half], x16_ref[rows, half:] = _unpack_bf16_pairs(xs_ref[rows])

        x = x16_ref[rows]
        h1 = _dot(x, w1_ref[...].astype(BF16))
        h = (h1 * jax.nn.sigmoid(h1) * _dot(x, w3_ref[...].astype(BF16))).astype(BF16)
        ys_ref[rows] += _dot(h, w2_ref[...].astype(BF16))

    for g in range(1, MOE_GRANULES + 1):
        @pl.when((n_rows > (g - 1) * MOE_TM) & (n_rows <= g * MOE_TM))
        def _():
            swiglu(slice(0, g * MOE_TM))


def _experts(tile_expert, tile_rows, n_used, xs, w1, w3, w2):
    n_rows, d = xs.shape[0], w1.shape[1]
    ff = w1.shape[2]
    tm, tf = MOE_TM * MOE_GRANULES, MOE_TF
    n_f = ff // tf

    def rows(i, f, te, tr, nu):
        return (jnp.minimum(i, nu[0] - 1), 0)

    def chunk(i, f, nu):
        return jnp.where(i < nu[0], f, n_f - 1)

    vmem = tm * d * 2 + 2 * (tm * d * 4 + 3 * d * tf * 4) + tm * d * 2 + 3 * d * tf * 2 + 6 * tm * tf * 4
    return pl.pallas_call(
        _experts_kernel,
        grid_spec=pltpu.PrefetchScalarGridSpec(
            num_scalar_prefetch=3,
            grid=(n_rows // tm, n_f),
            in_specs=[
                pl.BlockSpec((tm, d // 2), rows, pipeline_mode=pl.Buffered(1)),
                pl.BlockSpec((None, d, tf), lambda i, f, te, tr, nu: (te[i], 0, chunk(i, f, nu))),
                pl.BlockSpec((None, d, tf), lambda i, f, te, tr, nu: (te[i], 0, chunk(i, f, nu))),
                pl.BlockSpec((None, tf, d), lambda i, f, te, tr, nu: (te[i], chunk(i, f, nu), 0)),
            ],
            out_specs=pl.BlockSpec((tm, d), lambda i, f, te, tr, nu: (i, 0)),
            scratch_shapes=[pltpu.VMEM((tm, d), BF16)],
        ),
        out_shape=jax.ShapeDtypeStruct((n_rows, d), F32),
        compiler_params=_params(("arbitrary", "arbitrary"), vmem),
        name="moe_experts",
    )(tile_expert, tile_rows, n_used, xs, w1, w3, w2)


def _combine_kernel(pos_ref, pos_next_ref, route_ref, x_ref, ys_hbm, g_ref, b_ref, o_ref, buf, sems):
    tm = COMBINE_TM
    i = pl.program_id(0)
    slot = i % 2

    def start_tile(p_ref, slot):
        def start(r, carry):
            for k in range(2):
                _row_copy(ys_hbm, p_ref[0, 0, k * tm + r], buf.at[slot, k], r, sems.at[slot]).start(priority=k)
            return carry
        lax.fori_loop(0, tm, start, 0, unroll=ROW_DMA_UNROLL)

    @pl.when(i == 0)
    def _():
        start_tile(pos_ref, 0)

    @pl.when(i + 1 < pl.num_programs(0))
    def _():
        start_tile(pos_next_ref, 1 - slot)

    for k in range(2):
        pltpu.make_async_copy(ys_hbm.at[pl.ds(0, tm)], buf.at[slot, k], sems.at[slot]).wait()
    w0 = route_ref[:, ROUTE_W:ROUTE_W + 1]
    w1 = route_ref[:, ROUTE_W + 1:ROUTE_W + 2]
    y = w0 * buf[slot, 0] + w1 * buf[slot, 1]
    o_ref[...] = _layer_norm(ALPHA * x_ref[...] + y, g_ref[...], b_ref[...])


def _combine(pos_tiles, route, x32, ys, ln_g, ln_b):
    s, d = x32.shape
    tm = COMBINE_TM
    n = s // tm
    rowblk = lambda i: (i, 0)
    const = lambda i: (0, 0)
    vmem = 2 * 2 * tm * d * 4 + 2 * (2 * tm * d * 4 + tm * LANES * 4) + 6 * tm * d * 4
    return pl.pallas_call(
        _combine_kernel,
        grid=(n,),
        in_specs=[
            pl.BlockSpec((1, 1, 2 * tm), lambda i: (i, 0, 0), memory_space=pltpu.SMEM),
            pl.BlockSpec((1, 1, 2 * tm), lambda i: (jnp.minimum(i + 1, n - 1), 0, 0),
                         memory_space=pltpu.SMEM),
            pl.BlockSpec((tm, LANES), rowblk),
            pl.BlockSpec((tm, d), rowblk),
            pl.BlockSpec(memory_space=pl.ANY),
            pl.BlockSpec((1, d), const),
            pl.BlockSpec((1, d), const),
        ],
        out_specs=pl.BlockSpec((tm, d), rowblk),
        out_shape=jax.ShapeDtypeStruct((s, d), F32),
        scratch_shapes=[pltpu.VMEM((2, 2, tm, d), F32), pltpu.SemaphoreType.DMA((2,))],
        compiler_params=_params(("arbitrary",), vmem),
        name="moe_combine_ln",
    )(pos_tiles, pos_tiles, route, x32, ys, ln_g, ln_b)


def _pos_tiles(pos, tm):
    s = pos.shape[0]
    return pos.reshape(s // tm, tm, 2).transpose(0, 2, 1).reshape(s // tm, 1, 2 * tm)


def _moe(x32, w_router, w1, w3, w2, ln_g, ln_b):
    s, d = x32.shape
    tm = MOE_TM * MOE_GRANULES
    n_tiles = (2 * s) // tm + N_EXPERTS
    route, counts = _router(x32, jnp.pad(w_router, ((0, 0), (0, LANES - N_EXPERTS))))

    counts = counts[0, :N_EXPERTS].astype(jnp.int32)
    tiles_per_expert = (counts + tm - 1) // tm
    tile_end = jnp.cumsum(tiles_per_expert)
    tile_start = tile_end - tiles_per_expert
    n_used = tile_end[-1:]
    tile_ids = jnp.arange(n_tiles, dtype=jnp.int32)
    tile_expert = jnp.sum(jnp.minimum(tile_ids, n_used - 1)[:, None] >= tile_end[None, :],
                          axis=1).astype(jnp.int32)
    tile_rows = jnp.clip(counts[tile_expert] - (tile_ids - tile_start[tile_expert]) * tm, 0, tm)
    tile_rows = jnp.where(tile_ids < n_used, tile_rows, 0).astype(jnp.int32)
    experts = route[:, ROUTE_E:ROUTE_E + 2].astype(jnp.int32)
    pos = (tile_start * tm)[experts] + route[:, ROUTE_RANK:ROUTE_RANK + 2].astype(jnp.int32)

    xs = _dispatch(_pos_tiles(pos, DISPATCH_TM), x32, n_tiles * tm)
    ys = _experts(tile_expert, tile_rows, n_used, xs, w1, w3, w2)
    return _combine(_pos_tiles(pos, COMBINE_TM), route, x32, ys, ln_g, ln_b)


def kernel(x, w_in, b_gate, sg_w, sg_b, sg_ln_g, sg_ln_b, w_branch_a, w_branch_b, w_out,
           ln1_g, ln1_b, ffn_w1, ffn_w3, ffn_w2, moe_router, moe_w1, moe_w3, moe_w2,
           ln2_g, ln2_b):
    b, s, d = x.shape
    assert (b, s, d) == (1, SEQ, D_MODEL)
    x32 = x.reshape(s, d)
    x16 = x32.astype(BF16)
    for layer in range(DEPTH):
        qkv = _inproj(x16, w_in, layer, 0, 3 * SB_WIDTH, "qkv")
        u = _inproj(x16, w_in, layer, OFF_U, SG_WIDTH, "gelu")
        vn = _inproj(x16, w_in, layer, OFF_VG, SG_WIDTH, "gelu_ln",
                     (sg_ln_g[layer].reshape(1, -1), sg_ln_b[layer].reshape(1, -1)))
        gates = _inproj(x16, w_in, layer, OFF_GATE, 2 * d, "gate", (b_gate[layer].reshape(1, -1),))
        att = _attention(qkv)
        sgu = _sgu(u, vn, sg_w[layer], sg_b[layer].T)
        x32, x16 = _merge(att, sgu, gates, x32,
                          w_branch_a[layer].astype(BF16), w_branch_b[layer].astype(BF16),
                          w_out[layer].astype(BF16),
                          ln1_g[layer].reshape(1, d), ln1_b[layer].reshape(1, d))
        i = layer // 2
        g2, b2 = ln2_g[layer].reshape(1, d), ln2_b[layer].reshape(1, d)
        if layer % 2 == 0:
            x32, x16 = _ffn(x16, x32, ffn_w1[i].astype(BF16), ffn_w3[i].astype(BF16),
                            ffn_w2[i].astype(BF16), g2, b2)
        else:
            x32 = _moe(x32, moe_router[i], moe_w1[i], moe_w3[i], moe_w2[i], g2, b2)
            x16 = x32.astype(BF16) if layer + 1 < DEPTH else None
    return x32.reshape(b, s, d)
```

```python
import functools

import jax
import jax.numpy as jnp
from jax import lax
from jax.experimental import pallas as pl
from jax.experimental.pallas import tpu as pltpu

F32 = jnp.float32
BF16 = jnp.bfloat16

D_MODEL = 2048
SEQ = 8192
DEPTH = 2
HEADS = 8
HEAD_DIM = 128
SB_WIDTH = HEADS * HEAD_DIM
GROUPS = 8
GROUP_DIM = 128
SG_WIDTH = GROUPS * GROUP_DIM
CHUNK = 128
OFF_U = 3 * SB_WIDTH
OFF_VG = OFF_U + SG_WIDTH
OFF_GATE = OFF_VG + SG_WIDTH
D_FF_DENSE = 5504
N_EXPERTS = 8
D_FF_EXPERT = 7168
ALPHA = (2.0 * DEPTH) ** 0.25
LN_EPS = 1e-5
SB_UNDERFLOW = 110.0
LOG2_E = 1.4426950408889634

LANES = 128
V7X_VMEM_BYTES = 64 * 1024 * 1024

PROJ_TM = 1024
PROJ_TN = 1024
ATT_TQ = 256
ATT_TK = 256
ATT_HEADS = 8
SGU_ROWS = 512
MERGE_TM = 256
FFN_TM = 1024
FFN_TF = 256
LN_ROWS = 256
ROUTER_TM = 512
DISPATCH_TM = 512
MOE_TM = 256
MOE_GRANULES = 5
MOE_TF = 256
COMBINE_TM = 256
ROW_DMA_UNROLL = 8


def _params(semantics, vmem_bytes):
    assert vmem_bytes < V7X_VMEM_BYTES
    return pltpu.CompilerParams(dimension_semantics=semantics, vmem_limit_bytes=vmem_bytes)


def _dot(a, b):
    return jnp.dot(a, b, preferred_element_type=F32)


def _layer_norm(y, g, b):
    mu = jnp.mean(y, axis=-1, keepdims=True)
    d = y - mu
    var = jnp.mean(d * d, axis=-1, keepdims=True)
    return d * lax.rsqrt(var + LN_EPS) * g + b


def _gelu_tanh(x):
    return 0.5 * x * (1.0 + jnp.tanh(0.7978845608028654 * (x + 0.044715 * (x * x * x))))


def _inproj_kernel(x_ref, w_ref, *rest, mode):
    *rest, w16_ref = rest

    @pl.when(pl.program_id(1) == 0)
    def _():
        w16_ref[...] = w_ref[...].astype(BF16)

    acc = _dot(x_ref[...], w16_ref[...])
    if mode == "qkv":
        (o_ref,) = rest
        scale = jnp.where(pl.program_id(0) == 0, HEAD_DIM ** -0.5 * LOG2_E, 1.0).astype(F32)
        o_ref[...] = (acc * scale).astype(o_ref.dtype)
    elif mode == "gelu":
        (o_ref,) = rest
        o_ref[...] = _gelu_tanh(acc).astype(o_ref.dtype)
    elif mode == "gelu_ln":
        g_ref, b_ref, o_ref = rest
        act = _gelu_tanh(acc)
        for grp in range(acc.shape[1] // GROUP_DIM):
            cols = slice(grp * GROUP_DIM, (grp + 1) * GROUP_DIM)
            o_ref[:, cols] = _layer_norm(act[:, cols], g_ref[:, cols], b_ref[:, cols]).astype(o_ref.dtype)
    elif mode == "gate":
        b_ref, o_ref = rest
        o_ref[...] = jax.nn.sigmoid(acc + b_ref[...]).astype(o_ref.dtype)
    else:
        raise ValueError(mode)


def _inproj(x16, w_in, layer, col_off, width, mode, extra=()):
    s, d = x16.shape
    tm, tn = PROJ_TM, PROJ_TN
    n_blk = width // tn
    off_blk = col_off // tn
    in_specs = [
        pl.BlockSpec((tm, d), lambda n, m: (m, 0)),
        pl.BlockSpec((None, d, tn), lambda n, m: (layer, 0, off_blk + n)),
    ]
    for _ in extra:
        in_specs.append(pl.BlockSpec((1, tn), lambda n, m: (0, n)))
    vmem = 2 * (tm * d * 2 + d * tn * 4 + tm * tn * 2) + d * tn * 2 + 6 * tm * tn * 4
    return pl.pallas_call(
        functools.partial(_inproj_kernel, mode=mode),
        grid=(n_blk, s // tm),
        in_specs=in_specs,
        out_specs=pl.BlockSpec((tm, tn), lambda n, m: (m, n)),
        out_shape=jax.ShapeDtypeStruct((s, width), BF16),
        scratch_shapes=[pltpu.VMEM((d, tn), BF16)],
        compiler_params=_params(("arbitrary", "arbitrary"), vmem),
        name="inproj_" + mode,
    )(x16, w_in, *extra)


def _attn_kernel(q_ref, k_ref, v_ref, o_ref, z_ref, ls_ref, hl_ref, tail_ref, a_ref, acc_ref, c_ref):
    tq, t = ATT_TQ, ATT_TK
    tiles_per_q = tq // t
    i = pl.program_id(1)
    krow = lax.broadcasted_iota(jnp.int32, (t, t), 0)
    kcol = lax.broadcasted_iota(jnp.int32, (t, t), 1)
    later = jnp.where(krow > kcol, 1.0, 0.0).astype(BF16)
    later2 = jnp.concatenate([later, later], axis=0)
    row = lax.broadcasted_iota(jnp.int32, (tq, t), 0)
    col = lax.broadcasted_iota(jnp.int32, (tq, t), 1)
    heads = [slice(h * HEAD_DIM, (h + 1) * HEAD_DIM) for h in range(ATT_HEADS)]

    def key_tile(j, mask):
        keys = pl.ds(pl.multiple_of(j * t, t), t)
        for h, hs in enumerate(heads):
            z_ref[h] = lax.dot_general(q_ref[:, hs], k_ref[keys, hs], (((1,), (1,)), ((), ())),
                                       preferred_element_type=F32)
        for h in range(ATT_HEADS):
            z = z_ref[h]
            ls = jnp.minimum(z, 0.0) - jnp.log2(1.0 + jnp.exp2(-jnp.abs(z)))
            lk = ls - z
            if mask is not None:
                lk = jnp.where(mask, lk, 0.0)
            hi = lk.astype(BF16)
            ls_ref[h] = ls
            hl_ref[h, :, :t] = hi
            hl_ref[h, :, t:] = (lk - hi.astype(F32)).astype(BF16)
        for h in range(ATT_HEADS):
            tail_ref[h] = _dot(hl_ref[h], later2)
        for h, hs in enumerate(heads):
            tail = tail_ref[h]
            a = jnp.exp2(ls_ref[h] + tail + jnp.tile(c_ref[:, hs], (1, t // HEAD_DIM)))
            lk0 = ls_ref[h][:, :1] - z_ref[h][:, :1]
            if mask is not None:
                a = jnp.where(mask, a, 0.0)
                lk0 = jnp.where(mask[:, :1], lk0, 0.0)
            a_ref[h] = a.astype(BF16)
            c_ref[:, hs] += jnp.broadcast_to(tail[:, :1] + lk0, (tq, HEAD_DIM))
        for h, hs in enumerate(heads):
            acc_ref[:, hs] += _dot(a_ref[h], v_ref[keys, hs])

    def any_weight_left():
        return jnp.max(c_ref[...]) > -SB_UNDERFLOW * LOG2_E

    acc_ref[...] = jnp.zeros_like(acc_ref)
    c_ref[...] = jnp.zeros_like(c_ref)
    for back in range(tiles_per_q):
        offset = (tiles_per_q - 1 - back) * t
        key_tile(i * tiles_per_q + (tiles_per_q - 1 - back), col + offset < row)

    def cond(carry):
        j, go = carry
        return jnp.logical_and(j >= 0, go)

    def body(carry):
        j, _ = carry
        key_tile(j, None)
        return j - 1, any_weight_left()

    lax.while_loop(cond, body, (i * tiles_per_q - 1, any_weight_left()))
    o_ref[...] = acc_ref[...].astype(o_ref.dtype)


def _attention(qkv):
    s = qkv.shape[0]
    tq, t = ATT_TQ, ATT_TK
    nh = ATT_HEADS
    w = nh * HEAD_DIM
    groups = SB_WIDTH // w
    resident = pl.Buffered(1)
    vmem = 2 * s * w * 2 + 2 * (2 * tq * w * 2) + nh * 16 * tq * t + 2 * tq * w * 4 + (8 << 20)
    return pl.pallas_call(
        _attn_kernel,
        grid=(groups, s // tq),
        in_specs=[
            pl.BlockSpec((tq, w), lambda g, i: (i, g)),
            pl.BlockSpec((s, w), lambda g, i: (0, groups + g), pipeline_mode=resident),
            pl.BlockSpec((s, w), lambda g, i: (0, 2 * groups + g), pipeline_mode=resident),
        ],
        out_specs=pl.BlockSpec((tq, w), lambda g, i: (i, g)),
        out_shape=jax.ShapeDtypeStruct((s, SB_WIDTH), BF16),
        scratch_shapes=[
            pltpu.VMEM((nh, tq, t), F32),
            pltpu.VMEM((nh, tq, t), F32),
            pltpu.VMEM((nh, tq, 2 * t), BF16),
            pltpu.VMEM((nh, tq, t), F32),
            pltpu.VMEM((nh, tq, t), BF16),
            pltpu.VMEM((tq, w), F32),
            pltpu.VMEM((tq, w), F32),
        ],
        compiler_params=_params(("arbitrary", "arbitrary"), vmem),
        name="stickbreak_attn",
    )(qkv, qkv, qkv)


def _sgu_kernel(u_ref, v_ref, w_ref, b_ref, o_ref):
    c = CHUNK
    row = lax.broadcasted_iota(jnp.int32, (c, c), 0)
    col = lax.broadcasted_iota(jnp.int32, (c, c), 1)
    causal = col <= row
    for g in range(GROUPS):
        w = jnp.where(causal, w_ref[g], 0.0).astype(BF16)
        b = b_ref[:, g:g + 1]
        cols = slice(g * GROUP_DIM, (g + 1) * GROUP_DIM)
        for cc in range(SGU_ROWS // c):
            rows = slice(cc * c, (cc + 1) * c)
            mixed = _dot(w, v_ref[rows, cols]) + b
            o_ref[rows, cols] = (u_ref[rows, cols].astype(F32) * mixed).astype(o_ref.dtype)


def _sgu(u, vn, sg_w, sg_b_t):
    s = u.shape[0]
    r = SGU_ROWS
    vmem = 2 * 3 * r * SG_WIDTH * 2 + 2 * GROUPS * CHUNK * CHUNK * 4 + (4 << 20)
    return pl.pallas_call(
        _sgu_kernel,
        grid=(s // r,),
        in_specs=[
            pl.BlockSpec((r, SG_WIDTH), lambda i: (i, 0)),
            pl.BlockSpec((r, SG_WIDTH), lambda i: (i, 0)),
            pl.BlockSpec((GROUPS, CHUNK, CHUNK), lambda i: (0, 0, 0)),
            pl.BlockSpec((CHUNK, GROUPS), lambda i: (0, 0)),
        ],
        out_specs=pl.BlockSpec((r, SG_WIDTH), lambda i: (i, 0)),
        out_shape=jax.ShapeDtypeStruct((s, SG_WIDTH), BF16),
        compiler_params=_params(("arbitrary",), vmem),
        name="spatial_gating",
    )(u, vn, sg_w, sg_b_t)


def _merge_kernel(a_ref, b_ref, gate_ref, x_ref, wa_ref, wb_ref, wo_ref, g_ref, beta_ref,
                  o32_ref, o16_ref):
    d = D_MODEL
    ya = _dot(a_ref[...], wa_ref[...])
    yb = _dot(b_ref[...], wb_ref[...])
    merged = gate_ref[:, :d].astype(F32) * ya + gate_ref[:, d:].astype(F32) * yb
    mix = _dot(merged.astype(BF16), wo_ref[...])
    y = _layer_norm(ALPHA * x_ref[...] + mix, g_ref[...], beta_ref[...])
    o32_ref[...] = y
    o16_ref[...] = y.astype(BF16)


def _merge(att, sgu, gates, x32, wa, wb, wo, ln_g, ln_b):
    s, d = x32.shape
    tm = MERGE_TM
    const = lambda i: (0, 0)
    rowblk = lambda i: (i, 0)
    weights = (SB_WIDTH * d + SG_WIDTH * d + d * d) * 2
    vmem = 2 * weights + 2 * tm * (SB_WIDTH * 2 + SG_WIDTH * 2 + 2 * d * 2 + d * 4 + d * 4 + d * 2) \
        + 6 * tm * d * 4
    return pl.pallas_call(
        _merge_kernel,
        grid=(s // tm,),
        in_specs=[
            pl.BlockSpec((tm, SB_WIDTH), rowblk),
            pl.BlockSpec((tm, SG_WIDTH), rowblk),
            pl.BlockSpec((tm, 2 * d), rowblk),
            pl.BlockSpec((tm, d), rowblk),
            pl.BlockSpec((SB_WIDTH, d), const),
            pl.BlockSpec((SG_WIDTH, d), const),
            pl.BlockSpec((d, d), const),
            pl.BlockSpec((1, d), const),
            pl.BlockSpec((1, d), const),
        ],
        out_specs=[pl.BlockSpec((tm, d), rowblk), pl.BlockSpec((tm, d), rowblk)],
        out_shape=[jax.ShapeDtypeStruct((s, d), F32), jax.ShapeDtypeStruct((s, d), BF16)],
        compiler_params=_params(("arbitrary",), vmem),
        name="merge_outproj_ln",
    )(att, sgu, gates, x32, wa, wb, wo, ln_g, ln_b)


def _swiglu_hidden(x, w1_ref, w3_ref):
    h1 = _dot(x, w1_ref[...])
    return (h1 * jax.nn.sigmoid(h1) * _dot(x, w3_ref[...])).astype(BF16)


def _ffn_kernel(x16_ref, x32_ref, w1_ref, w3_ref, w2_ref, w1t_ref, w3t_ref, w2t_ref, g_ref, b_ref,
                o32_ref, o16_ref):
    f = pl.program_id(1)

    @pl.when(f == 0)
    def _():
        o32_ref[...] = jnp.zeros_like(o32_ref)

    o32_ref[...] += _dot(_swiglu_hidden(x16_ref[...], w1_ref, w3_ref), w2_ref[...])

    @pl.when(f == pl.num_programs(1) - 1)
    def _():
        o32_ref[...] += _dot(_swiglu_hidden(x16_ref[...], w1t_ref, w3t_ref), w2t_ref[...])

        def ln_rows(c, carry):
            rows = pl.ds(pl.multiple_of(c * LN_ROWS, LN_ROWS), LN_ROWS)
            y = _layer_norm(ALPHA * x32_ref[rows, :] + o32_ref[rows, :], g_ref[...], b_ref[...])
            o32_ref[rows, :] = y
            o16_ref[rows, :] = y.astype(BF16)
            return carry
        lax.fori_loop(0, o32_ref.shape[0] // LN_ROWS, ln_rows, 0)


def _ffn(x16, x32, w1, w3, w2, ln_g, ln_b):
    s, d = x32.shape
    ff = w1.shape[1]
    tm, tf = FFN_TM, FFN_TF
    n_f = ff // tf
    tail = ff - n_f * tf
    assert 0 < tail < tf and tail % LANES == 0 and (n_f * tf) % tail == 0
    tail_blk = (n_f * tf) // tail
    rowblk = lambda i, f: (i, 0)
    const = lambda i, f: (0, 0)
    vmem = tm * d * 4 + 2 * (tm * d * (2 + 4 + 2) + 3 * d * (tf + tail) * 2) + 5 * tm * tf * 4 \
        + 4 * LN_ROWS * d * 4
    return pl.pallas_call(
        _ffn_kernel,
        grid=(s // tm, n_f),
        in_specs=[
            pl.BlockSpec((tm, d), rowblk),
            pl.BlockSpec((tm, d), rowblk, pipeline_mode=pl.Buffered(1)),
            pl.BlockSpec((d, tf), lambda i, f: (0, f)),
            pl.BlockSpec((d, tf), lambda i, f: (0, f)),
            pl.BlockSpec((tf, d), lambda i, f: (f, 0)),
            pl.BlockSpec((d, tail), lambda i, f: (0, tail_blk)),
            pl.BlockSpec((d, tail), lambda i, f: (0, tail_blk)),
            pl.BlockSpec((tail, d), lambda i, f: (tail_blk, 0)),
            pl.BlockSpec((1, d), const),
            pl.BlockSpec((1, d), const),
        ],
        out_specs=[pl.BlockSpec((tm, d), rowblk), pl.BlockSpec((tm, d), rowblk)],
        out_shape=[jax.ShapeDtypeStruct((s, d), F32), jax.ShapeDtypeStruct((s, d), BF16)],
        compiler_params=_params(("arbitrary", "arbitrary"), vmem),
        name="swiglu_ln",
    )(x16, x32, w1, w3, w2, w1, w3, w2, ln_g, ln_b)


ROUTE_E, ROUTE_W, ROUTE_RANK = 0, 2, 4


def _router_kernel(x_ref, w_ref, route_ref, counts_ref, seen_ref):
    @pl.when(pl.program_id(0) == 0)
    def _():
        seen_ref[...] = jnp.zeros_like(seen_ref)

    x, w = x_ref[...], w_ref[...]
    x_hi, w_hi = x.astype(BF16), w.astype(BF16)
    x_lo, w_lo = (x - x_hi.astype(F32)).astype(BF16), (w - w_hi.astype(F32)).astype(BF16)
    logits = _dot(x_hi, w_hi) + (_dot(x_hi, w_lo) + _dot(x_lo, w_hi))
    tm = logits.shape[0]
    lane = lax.broadcasted_iota(jnp.int32, logits.shape, 1).astype(F32)
    neg = jnp.float32(-jnp.inf)
    l1 = jnp.where(lane < N_EXPERTS, logits, neg)
    m1 = jnp.max(l1, axis=-1, keepdims=True)
    i1 = jnp.min(jnp.where(l1 == m1, lane, float(LANES)), axis=-1, keepdims=True)
    l2 = jnp.where(lane == i1, neg, l1)
    m2 = jnp.max(l2, axis=-1, keepdims=True)
    i2 = jnp.min(jnp.where(l2 == m2, lane, float(LANES)), axis=-1, keepdims=True)
    e2 = jnp.exp(m2 - m1)
    w_top = 1.0 / (1.0 + e2)

    chosen = jnp.where(lane == i1, 1.0, 0.0) + jnp.where(lane == i2, 1.0, 0.0)
    row = lax.broadcasted_iota(jnp.int32, (tm, tm), 0)
    col = lax.broadcasted_iota(jnp.int32, (tm, tm), 1)
    earlier = jnp.where(col < row, 1.0, 0.0).astype(BF16)
    prefix = _dot(earlier, chosen.astype(BF16)) + seen_ref[...]
    r1 = jnp.sum(jnp.where(lane == i1, prefix, 0.0), axis=-1, keepdims=True)
    r2 = jnp.sum(jnp.where(lane == i2, prefix, 0.0), axis=-1, keepdims=True)
    seen_ref[...] += jnp.sum(chosen, axis=0, keepdims=True)
    counts_ref[...] = seen_ref[...]

    fields = ((ROUTE_E, i1), (ROUTE_E + 1, i2), (ROUTE_W, w_top), (ROUTE_W + 1, e2 * w_top),
              (ROUTE_RANK, r1), (ROUTE_RANK + 1, r2))
    route = jnp.zeros_like(logits)
    for at, val in fields:
        route = jnp.where(lane == at, val, route)
    route_ref[...] = route


def _router(x32, w_router_padded):
    s, d = x32.shape
    tm = ROUTER_TM
    vmem = 2 * (tm * d * 4 + d * LANES * 4 + tm * LANES * 4) + 8 * tm * d * 4
    return pl.pallas_call(
        _router_kernel,
        grid=(s // tm,),
        in_specs=[pl.BlockSpec((tm, d), lambda i: (i, 0)), pl.BlockSpec((d, LANES), lambda i: (0, 0))],
        out_specs=[pl.BlockSpec((tm, LANES), lambda i: (i, 0)), pl.BlockSpec((1, LANES), lambda i: (0, 0))],
        out_shape=[jax.ShapeDtypeStruct((s, LANES), F32), jax.ShapeDtypeStruct((1, LANES), F32)],
        scratch_shapes=[pltpu.VMEM((1, LANES), F32)],
        compiler_params=_params(("arbitrary",), vmem),
        name="router_top2",
    )(x32, w_router_padded)


def _row_copy(src, src_row, dst, dst_row, sem):
    return pltpu.make_async_copy(src.at[pl.ds(src_row, 1)], dst.at[pl.ds(dst_row, 1)], sem)


def _dispatch_kernel(pos_ref, x_ref, xs_zero_hbm, xs_hbm, sem):
    del xs_zero_hbm
    tm = DISPATCH_TM

    def start(r, carry):
        for k in range(2):
            _row_copy(x_ref, r, xs_hbm, pos_ref[0, 0, k * tm + r], sem).start(priority=k)
        return carry

    lax.fori_loop(0, tm, start, 0, unroll=ROW_DMA_UNROLL)
    for k in range(2):
        pltpu.make_async_copy(x_ref, xs_hbm.at[pl.ds(0, tm)], sem).wait()


def _dispatch(pos_tiles, x32, n_rows):
    s, d = x32.shape
    tm = DISPATCH_TM
    return pl.pallas_call(
        _dispatch_kernel,
        grid=(s // tm,),
        in_specs=[
            pl.BlockSpec((1, 1, 2 * tm), lambda i: (i, 0, 0), memory_space=pltpu.SMEM),
            pl.BlockSpec((tm, d), lambda i: (i, 0)),
            pl.BlockSpec(memory_space=pl.ANY),
        ],
        out_specs=pl.BlockSpec(memory_space=pl.ANY),
        out_shape=jax.ShapeDtypeStruct((n_rows, d), F32),
        scratch_shapes=[pltpu.SemaphoreType.DMA(())],
        input_output_aliases={2: 0},
        compiler_params=_params(("arbitrary",), 2 * tm * d * 4 + (4 << 20)),
        name="moe_dispatch",
    )(pos_tiles, x32, jnp.zeros((n_rows, d), F32))


def _experts_kernel(tile_expert_ref, tile_rows_ref, n_used_ref, xs_ref, w1_ref, w3_ref, w2_ref, ys_ref,
                    x16_ref):
    del tile_expert_ref, n_used_ref
    f = pl.program_id(1)
    n_rows = tile_rows_ref[pl.program_id(0)]

    @pl.when(f == 0)
    def _():
        ys_ref[...] = jnp.zeros_like(ys_ref)

    def swiglu(rows):
        @pl.when(f == 0)
        def _():
            x16_ref[rows] = xs_ref[rows].astype(BF16)

        x = x16_ref[rows]
        h1 = _dot(x, w1_ref[...].astype(BF16))
        h = (h1 * jax.nn.sigmoid(h1) * _dot(x, w3_ref[...].astype(BF16))).astype(BF16)
        ys_ref[rows] += _dot(h, w2_ref[...].astype(BF16))

    for g in range(1, MOE_GRANULES + 1):
        @pl.when((n_rows > (g - 1) * MOE_TM) & (n_rows <= g * MOE_TM))
        def _():
            swiglu(slice(0, g * MOE_TM))


def _experts(tile_expert, tile_rows, n_used, xs, w1, w3, w2):
    n_rows, d = xs.shape
    ff = w1.shape[2]
    tm, tf = MOE_TM * MOE_GRANULES, MOE_TF
    n_f = ff // tf

    def rows(i, f, te, tr, nu):
        return (jnp.minimum(i, nu[0] - 1), 0)

    def chunk(i, f, nu):
        return jnp.where(i < nu[0], f, n_f - 1)

    vmem = tm * d * 4 + 2 * (tm * d * 4 + 3 * d * tf * 4) + tm * d * 2 + 3 * d * tf * 2 + 6 * tm * tf * 4
    return pl.pallas_call(
        _experts_kernel,
        grid_spec=pltpu.PrefetchScalarGridSpec(
            num_scalar_prefetch=3,
            grid=(n_rows // tm, n_f),
            in_specs=[
                pl.BlockSpec((tm, d), rows, pipeline_mode=pl.Buffered(1)),
                pl.BlockSpec((None, d, tf), lambda i, f, te, tr, nu: (te[i], 0, chunk(i, f, nu))),
                pl.BlockSpec((None, d, tf), lambda i, f, te, tr, nu: (te[i], 0, chunk(i, f, nu))),
                pl.BlockSpec((None, tf, d), lambda i, f, te, tr, nu: (te[i], chunk(i, f, nu), 0)),
            ],
            out_specs=pl.BlockSpec((tm, d), lambda i, f, te, tr, nu: (i, 0)),
            scratch_shapes=[pltpu.VMEM((tm, d), BF16)],
        ),
        out_shape=jax.ShapeDtypeStruct((n_rows, d), F32),
        compiler_params=_params(("arbitrary", "arbitrary"), vmem),
        name="moe_experts",
    )(tile_expert, tile_rows, n_used, xs, w1, w3, w2)


def _combine_kernel(pos_ref, pos_next_ref, route_ref, x_ref, ys_hbm, g_ref, b_ref, o_ref, buf, sems):
    tm = COMBINE_TM
    i = pl.program_id(0)
    slot = i % 2

    def start_tile(p_ref, slot):
        def start(r, carry):
            for k in range(2):
                _row_copy(ys_hbm, p_ref[0, 0, k * tm + r], buf.at[slot, k], r, sems.at[slot]).start(priority=k)
            return carry
        lax.fori_loop(0, tm, start, 0, unroll=ROW_DMA_UNROLL)

    @pl.when(i == 0)
    def _():
        start_tile(pos_ref, 0)

    @pl.when(i + 1 < pl.num_programs(0))
    def _():
        start_tile(pos_next_ref, 1 - slot)

    for k in range(2):
        pltpu.make_async_copy(ys_hbm.at[pl.ds(0, tm)], buf.at[slot, k], sems.at[slot]).wait()
    w0 = route_ref[:, ROUTE_W:ROUTE_W + 1]
    w1 = route_ref[:, ROUTE_W + 1:ROUTE_W + 2]
    y = w0 * buf[slot, 0] + w1 * buf[slot, 1]
    o_ref[...] = _layer_norm(ALPHA * x_ref[...] + y, g_ref[...], b_ref[...])


def _combine(pos_tiles, route, x32, ys, ln_g, ln_b):
    s, d = x32.shape
    tm = COMBINE_TM
    n = s // tm
    rowblk = lambda i: (i, 0)
    const = lambda i: (0, 0)
    vmem = 2 * 2 * tm * d * 4 + 2 * (2 * tm * d * 4 + tm * LANES * 4) + 6 * tm * d * 4
    return pl.pallas_call(
        _combine_kernel,
        grid=(n,),
        in_specs=[
            pl.BlockSpec((1, 1, 2 * tm), lambda i: (i, 0, 0), memory_space=pltpu.SMEM),
            pl.BlockSpec((1, 1, 2 * tm), lambda i: (jnp.minimum(i + 1, n - 1), 0, 0),
                         memory_space=pltpu.SMEM),
            pl.BlockSpec((tm, LANES), rowblk),
            pl.BlockSpec((tm, d), rowblk),
            pl.BlockSpec(memory_space=pl.ANY),
            pl.BlockSpec((1, d), const),
            pl.BlockSpec((1, d), const),
        ],
        out_specs=pl.BlockSpec((tm, d), rowblk),
        out_shape=jax.ShapeDtypeStruct((s, d), F32),
        scratch_shapes=[pltpu.VMEM((2, 2, tm, d), F32), pltpu.SemaphoreType.DMA((2,))],
        compiler_params=_params(("arbitrary",), vmem),
        name="moe_combine_ln",
    )(pos_tiles, pos_tiles, route, x32, ys, ln_g, ln_b)


def _pos_tiles(pos, tm):
    s = pos.shape[0]
    return pos.reshape(s // tm, tm, 2).transpose(0, 2, 1).reshape(s // tm, 1, 2 * tm)


def _moe(x32, w_router, w1, w3, w2, ln_g, ln_b):
    s, d = x32.shape
    tm = MOE_TM * MOE_GRANULES
    n_tiles = (2 * s) // tm + N_EXPERTS
    route, counts = _router(x32, jnp.pad(w_router, ((0, 0), (0, LANES - N_EXPERTS))))

    counts = counts[0, :N_EXPERTS].astype(jnp.int32)
    tiles_per_expert = (counts + tm - 1) // tm
    tile_end = jnp.cumsum(tiles_per_expert)
    tile_start = tile_end - tiles_per_expert
    n_used = tile_end[-1:]
    tile_ids = jnp.arange(n_tiles, dtype=jnp.int32)
    tile_expert = jnp.sum(jnp.minimum(tile_ids, n_used - 1)[:, None] >= tile_end[None, :],
                          axis=1).astype(jnp.int32)
    tile_rows = jnp.clip(counts[tile_expert] - (tile_ids - tile_start[tile_expert]) * tm, 0, tm)
    tile_rows = jnp.where(tile_ids < n_used, tile_rows, 0).astype(jnp.int32)
    experts = route[:, ROUTE_E:ROUTE_E + 2].astype(jnp.int32)
    pos = (tile_start * tm)[experts] + route[:, ROUTE_RANK:ROUTE_RANK + 2].astype(jnp.int32)

    xs = _dispatch(_pos_tiles(pos, DISPATCH_TM), x32, n_tiles * tm)
    ys = _experts(tile_expert, tile_rows, n_used, xs, w1, w3, w2)
    return _combine(_pos_tiles(pos, COMBINE_TM), route, x32, ys, ln_g, ln_b)


def kernel(x, w_in, b_gate, sg_w, sg_b, sg_ln_g, sg_ln_b, w_branch_a, w_branch_b, w_out,
           ln1_g, ln1_b, ffn_w1, ffn_w3, ffn_w2, moe_router, moe_w1, moe_w3, moe_w2,
           ln2_g, ln2_b):
    b, s, d = x.shape
    assert (b, s, d) == (1, SEQ, D_MODEL)
    x32 = x.reshape(s, d)
    x16 = x32.astype(BF16)
    for layer in range(DEPTH):
        qkv = _inproj(x16, w_in, layer, 0, 3 * SB_WIDTH, "qkv")
        u = _inproj(x16, w_in, layer, OFF_U, SG_WIDTH, "gelu")
        vn = _inproj(x16, w_in, layer, OFF_VG, SG_WIDTH, "gelu_ln",
                     (sg_ln_g[layer].reshape(1, -1), sg_ln_b[layer].reshape(1, -1)))
        gates = _inproj(x16, w_in, layer, OFF_GATE, 2 * d, "gate", (b_gate[layer].reshape(1, -1),))
        att = _attention(qkv)
        sgu = _sgu(u, vn, sg_w[layer], sg_b[layer].T)
        x32, x16 = _merge(att, sgu, gates, x32,
                          w_branch_a[layer].astype(BF16), w_branch_b[layer].astype(BF16),
                          w_out[layer].astype(BF16),
                          ln1_g[layer].reshape(1, d), ln1_b[layer].reshape(1, d))
        i = layer // 2
        g2, b2 = ln2_g[layer].reshape(1, d), ln2_b[layer].reshape(1, d)
        if layer % 2 == 0:
            x32, x16 = _ffn(x16, x32, ffn_w1[i].astype(BF16), ffn_w3[i].astype(BF16),
                            ffn_w2[i].astype(BF16), g2, b2)
        else:
            x32 = _moe(x32, moe_router[i], moe_w1[i], moe_w3[i], moe_w2[i], g2, b2)
            x16 = x32.astype(BF16) if layer + 1 < DEPTH else None
    return x32.reshape(b, s, d)
```

```python
import functools

import jax
import jax.numpy as jnp
from jax import lax
from jax.experimental import pallas as pl
from jax.experimental.pallas import tpu as pltpu

F32 = jnp.float32
BF16 = jnp.bfloat16

D_MODEL = 2048
SEQ = 8192
DEPTH = 2
HEADS = 8
HEAD_DIM = 128
SB_WIDTH = HEADS * HEAD_DIM
GROUPS = 8
GROUP_DIM = 128
SG_WIDTH = GROUPS * GROUP_DIM
CHUNK = 128
OFF_U = 3 * SB_WIDTH
OFF_VG = OFF_U + SG_WIDTH
OFF_GATE = OFF_VG + SG_WIDTH
D_FF_DENSE = 5504
N_EXPERTS = 8
D_FF_EXPERT = 7168
ALPHA = (2.0 * DEPTH) ** 0.25
LN_EPS = 1e-5
SB_UNDERFLOW = 110.0
LOG2_E = 1.4426950408889634

LANES = 128
V7X_VMEM_BYTES = 64 * 1024 * 1024

PROJ_TM = 1024
PROJ_TN = 1024
ATT_TQ = 256
ATT_TK = 256
ATT_HEADS = 8
SGU_ROWS = 512
MERGE_TM = 256
FFN_TM = 1024
FFN_TF = 256
LN_ROWS = 256
ROUTER_TM = 512
DISPATCH_TM = 512
MOE_TM = 256
MOE_GRANULES = 5
MOE_TF = 256
COMBINE_TM = 256
ROW_DMA_UNROLL = 8


def _params(semantics, vmem_bytes):
    assert vmem_bytes < V7X_VMEM_BYTES
    return pltpu.CompilerParams(dimension_semantics=semantics, vmem_limit_bytes=vmem_bytes)


def _dot(a, b):
    return jnp.dot(a, b, preferred_element_type=F32)


def _layer_norm(y, g, b):
    mu = jnp.mean(y, axis=-1, keepdims=True)
    d = y - mu
    var = jnp.mean(d * d, axis=-1, keepdims=True)
    return d * lax.rsqrt(var + LN_EPS) * g + b


def _gelu_tanh(x):
    return 0.5 * x * (1.0 + jnp.tanh(0.7978845608028654 * (x + 0.044715 * (x * x * x))))


def _inproj_kernel(x_ref, w_ref, *rest, mode):
    *rest, w16_ref = rest

    @pl.when(pl.program_id(1) == 0)
    def _():
        w16_ref[...] = w_ref[...].astype(BF16)

    acc = _dot(x_ref[...], w16_ref[...])
    if mode == "qkv":
        (o_ref,) = rest
        scale = jnp.where(pl.program_id(0) == 0, HEAD_DIM ** -0.5 * LOG2_E, 1.0).astype(F32)
        o_ref[...] = (acc * scale).astype(o_ref.dtype)
    elif mode == "gelu":
        (o_ref,) = rest
        o_ref[...] = _gelu_tanh(acc).astype(o_ref.dtype)
    elif mode == "gelu_ln":
        g_ref, b_ref, o_ref = rest
        act = _gelu_tanh(acc)
        for grp in range(acc.shape[1] // GROUP_DIM):
            cols = slice(grp * GROUP_DIM, (grp + 1) * GROUP_DIM)
            o_ref[:, cols] = _layer_norm(act[:, cols], g_ref[:, cols], b_ref[:, cols]).astype(o_ref.dtype)
    elif mode == "gate":
        b_ref, o_ref = rest
        o_ref[...] = jax.nn.sigmoid(acc + b_ref[...]).astype(o_ref.dtype)
    else:
        raise ValueError(mode)


def _inproj(x16, w_in, layer, col_off, width, mode, extra=()):
    s, d = x16.shape
    tm, tn = PROJ_TM, PROJ_TN
    n_blk = width // tn
    off_blk = col_off // tn
    in_specs = [
        pl.BlockSpec((tm, d), lambda n, m: (m, 0)),
        pl.BlockSpec((None, d, tn), lambda n, m: (layer, 0, off_blk + n)),
    ]
    for _ in extra:
        in_specs.append(pl.BlockSpec((1, tn), lambda n, m: (0, n)))
    vmem = 2 * (tm * d * 2 + d * tn * 4 + tm * tn * 2) + d * tn * 2 + 6 * tm * tn * 4
    return pl.pallas_call(
        functools.partial(_inproj_kernel, mode=mode),
        grid=(n_blk, s // tm),
        in_specs=in_specs,
        out_specs=pl.BlockSpec((tm, tn), lambda n, m: (m, n)),
        out_shape=jax.ShapeDtypeStruct((s, width), BF16),
        scratch_shapes=[pltpu.VMEM((d, tn), BF16)],
        compiler_params=_params(("arbitrary", "arbitrary"), vmem),
        name="inproj_" + mode,
    )(x16, w_in, *extra)


def _attn_kernel(q_ref, k_ref, v_ref, o_ref, z_ref, ls_ref, hl_ref, tail_ref, a_ref, acc_ref, c_ref):
    tq, t = ATT_TQ, ATT_TK
    tiles_per_q = tq // t
    i = pl.program_id(1)
    krow = lax.broadcasted_iota(jnp.int32, (t, t), 0)
    kcol = lax.broadcasted_iota(jnp.int32, (t, t), 1)
    later = jnp.where(krow > kcol, 1.0, 0.0).astype(BF16)
    later2 = jnp.concatenate([later, later], axis=0)
    row = lax.broadcasted_iota(jnp.int32, (tq, t), 0)
    col = lax.broadcasted_iota(jnp.int32, (tq, t), 1)
    heads = [slice(h * HEAD_DIM, (h + 1) * HEAD_DIM) for h in range(ATT_HEADS)]

    def key_tile(j, mask):
        keys = pl.ds(pl.multiple_of(j * t, t), t)
        for h, hs in enumerate(heads):
            z_ref[h] = lax.dot_general(q_ref[:, hs], k_ref[keys, hs], (((1,), (1,)), ((), ())),
                                       preferred_element_type=F32)
        for h in range(ATT_HEADS):
            z = z_ref[h]
            ls = jnp.minimum(z, 0.0) - jnp.log2(1.0 + jnp.exp2(-jnp.abs(z)))
            lk = ls - z
            if mask is not None:
                lk = jnp.where(mask, lk, 0.0)
            hi = lk.astype(BF16)
            ls_ref[h] = ls
            hl_ref[h, :, :t] = hi
            hl_ref[h, :, t:] = (lk - hi.astype(F32)).astype(BF16)
        for h in range(ATT_HEADS):
            tail_ref[h] = _dot(hl_ref[h], later2)
        for h, hs in enumerate(heads):
            tail = tail_ref[h]
            a = jnp.exp2(ls_ref[h] + tail + jnp.tile(c_ref[:, hs], (1, t // HEAD_DIM)))
            lk0 = ls_ref[h][:, :1] - z_ref[h][:, :1]
            if mask is not None:
                a = jnp.where(mask, a, 0.0)
                lk0 = jnp.where(mask[:, :1], lk0, 0.0)
            a_ref[h] = a.astype(BF16)
            c_ref[:, hs] += jnp.broadcast_to(tail[:, :1] + lk0, (tq, HEAD_DIM))
        for h, hs in enumerate(heads):
            acc_ref[:, hs] += _dot(a_ref[h], v_ref[keys, hs])

    def any_weight_left():
        return jnp.max(c_ref[...]) > -SB_UNDERFLOW * LOG2_E

    acc_ref[...] = jnp.zeros_like(acc_ref)
    c_ref[...] = jnp.zeros_like(c_ref)
    for back in range(tiles_per_q):
        offset = (tiles_per_q - 1 - back) * t
        key_tile(i * tiles_per_q + (tiles_per_q - 1 - back), col + offset < row)

    def cond(carry):
        j, go = carry
        return jnp.logical_and(j >= 0, go)

    def body(carry):
        j, _ = carry
        key_tile(j, None)
        return j - 1, any_weight_left()

    lax.while_loop(cond, body, (i * tiles_per_q - 1, any_weight_left()))
    o_ref[...] = acc_ref[...].astype(o_ref.dtype)


def _attention(qkv):
    s = qkv.shape[0]
    tq, t = ATT_TQ, ATT_TK
    nh = ATT_HEADS
    w = nh * HEAD_DIM
    groups = SB_WIDTH // w
    resident = pl.Buffered(1)
    vmem = 2 * s * w * 2 + 2 * (2 * tq * w * 2) + nh * 16 * tq * t + 2 * tq * w * 4 + (8 << 20)
    return pl.pallas_call(
        _attn_kernel,
        grid=(groups, s // tq),
        in_specs=[
            pl.BlockSpec((tq, w), lambda g, i: (i, g)),
            pl.BlockSpec((s, w), lambda g, i: (0, groups + g), pipeline_mode=resident),
            pl.BlockSpec((s, w), lambda g, i: (0, 2 * groups + g), pipeline_mode=resident),
        ],
        out_specs=pl.BlockSpec((tq, w), lambda g, i: (i, g)),
        out_shape=jax.ShapeDtypeStruct((s, SB_WIDTH), BF16),
        scratch_shapes=[
            pltpu.VMEM((nh, tq, t), F32),
            pltpu.VMEM((nh, tq, t), F32),
            pltpu.VMEM((nh, tq, 2 * t), BF16),
            pltpu.VMEM((nh, tq, t), F32),
            pltpu.VMEM((nh, tq, t), BF16),
            pltpu.VMEM((tq, w), F32),
            pltpu.VMEM((tq, w), F32),
        ],
        compiler_params=_params(("arbitrary", "arbitrary"), vmem),
        name="stickbreak_attn",
    )(qkv, qkv, qkv)


def _sgu_kernel(u_ref, v_ref, w_ref, b_ref, o_ref):
    c = CHUNK
    row = lax.broadcasted_iota(jnp.int32, (c, c), 0)
    col = lax.broadcasted_iota(jnp.int32, (c, c), 1)
    causal = col <= row
    for g in range(GROUPS):
        w = jnp.where(causal, w_ref[g], 0.0).astype(BF16)
        b = b_ref[:, g:g + 1]
        cols = slice(g * GROUP_DIM, (g + 1) * GROUP_DIM)
        for cc in range(SGU_ROWS // c):
            rows = slice(cc * c, (cc + 1) * c)
            mixed = _dot(w, v_ref[rows, cols]) + b
            o_ref[rows, cols] = (u_ref[rows, cols].astype(F32) * mixed).astype(o_ref.dtype)


def _sgu(u, vn, sg_w, sg_b_t):
    s = u.shape[0]
    r = SGU_ROWS
    vmem = 2 * 3 * r * SG_WIDTH * 2 + 2 * GROUPS * CHUNK * CHUNK * 4 + (4 << 20)
    return pl.pallas_call(
        _sgu_kernel,
        grid=(s // r,),
        in_specs=[
            pl.BlockSpec((r, SG_WIDTH), lambda i: (i, 0)),
            pl.BlockSpec((r, SG_WIDTH), lambda i: (i, 0)),
            pl.BlockSpec((GROUPS, CHUNK, CHUNK), lambda i: (0, 0, 0)),
            pl.BlockSpec((CHUNK, GROUPS), lambda i: (0, 0)),
        ],
        out_specs=pl.BlockSpec((r, SG_WIDTH), lambda i: (i, 0)),
        out_shape=jax.ShapeDtypeStruct((s, SG_WIDTH), BF16),
        compiler_params=_params(("arbitrary",), vmem),
        name="spatial_gating",
    )(u, vn, sg_w, sg_b_t)


def _merge_kernel(a_ref, b_ref, gate_ref, x_ref, wa_ref, wb_ref, wo_ref, g_ref, beta_ref,
                  o32_ref, o16_ref):
    d = D_MODEL
    ya = _dot(a_ref[...], wa_ref[...])
    yb = _dot(b_ref[...], wb_ref[...])
    merged = gate_ref[:, :d].astype(F32) * ya + gate_ref[:, d:].astype(F32) * yb
    mix = _dot(merged.astype(BF16), wo_ref[...])
    y = _layer_norm(ALPHA * x_ref[...] + mix, g_ref[...], beta_ref[...])
    o32_ref[...] = y
    o16_ref[...] = y.astype(BF16)


def _merge(att, sgu, gates, x32, wa, wb, wo, ln_g, ln_b):
    s, d = x32.shape
    tm = MERGE_TM
    const = lambda i: (0, 0)
    rowblk = lambda i: (i, 0)
    weights = (SB_WIDTH * d + SG_WIDTH * d + d * d) * 2
    vmem = 2 * weights + 2 * tm * (SB_WIDTH * 2 + SG_WIDTH * 2 + 2 * d * 2 + d * 4 + d * 4 + d * 2) \
        + 6 * tm * d * 4
    return pl.pallas_call(
        _merge_kernel,
        grid=(s // tm,),
        in_specs=[
            pl.BlockSpec((tm, SB_WIDTH), rowblk),
            pl.BlockSpec((tm, SG_WIDTH), rowblk),
            pl.BlockSpec((tm, 2 * d), rowblk),
            pl.BlockSpec((tm, d), rowblk),
            pl.BlockSpec((SB_WIDTH, d), const),
            pl.BlockSpec((SG_WIDTH, d), const),
            pl.BlockSpec((d, d), const),
            pl.BlockSpec((1, d), const),
            pl.BlockSpec((1, d), const),
        ],
        out_specs=[pl.BlockSpec((tm, d), rowblk), pl.BlockSpec((tm, d), rowblk)],
        out_shape=[jax.ShapeDtypeStruct((s, d), F32), jax.ShapeDtypeStruct((s, d), BF16)],
        compiler_params=_params(("arbitrary",), vmem),
        name="merge_outproj_ln",
    )(att, sgu, gates, x32, wa, wb, wo, ln_g, ln_b)


def _swiglu_hidden(x, w1_ref, w3_ref):
    h1 = _dot(x, w1_ref[...])
    return (h1 * jax.nn.sigmoid(h1) * _dot(x, w3_ref[...])).astype(BF16)


def _ffn_kernel(x16_ref, x32_ref, w1_ref, w3_ref, w2_ref, w1t_ref, w3t_ref, w2t_ref, g_ref, b_ref,
                o32_ref, o16_ref):
    f = pl.program_id(1)

    @pl.when(f == 0)
    def _():
        o32_ref[...] = jnp.zeros_like(o32_ref)

    o32_ref[...] += _dot(_swiglu_hidden(x16_ref[...], w1_ref, w3_ref), w2_ref[...])

    @pl.when(f == pl.num_programs(1) - 1)
    def _():
        o32_ref[...] += _dot(_swiglu_hidden(x16_ref[...], w1t_ref, w3t_ref), w2t_ref[...])

        def ln_rows(c, carry):
            rows = pl.ds(pl.multiple_of(c * LN_ROWS, LN_ROWS), LN_ROWS)
            y = _layer_norm(ALPHA * x32_ref[rows, :] + o32_ref[rows, :], g_ref[...], b_ref[...])
            o32_ref[rows, :] = y
            o16_ref[rows, :] = y.astype(BF16)
            return carry
        lax.fori_loop(0, o32_ref.shape[0] // LN_ROWS, ln_rows, 0)


def _ffn(x16, x32, w1, w3, w2, ln_g, ln_b):
    s, d = x32.shape
    ff = w1.shape[1]
    tm, tf = FFN_TM, FFN_TF
    n_f = ff // tf
    tail = ff - n_f * tf
    assert 0 < tail < tf and tail % LANES == 0 and (n_f * tf) % tail == 0
    tail_blk = (n_f * tf) // tail
    rowblk = lambda i, f: (i, 0)
    const = lambda i, f: (0, 0)
    vmem = tm * d * 4 + 2 * (tm * d * (2 + 4 + 2) + 3 * d * (tf + tail) * 2) + 5 * tm * tf * 4 \
        + 4 * LN_ROWS * d * 4
    return pl.pallas_call(
        _ffn_kernel,
        grid=(s // tm, n_f),
        in_specs=[
            pl.BlockSpec((tm, d), rowblk),
            pl.BlockSpec((tm, d), rowblk, pipeline_mode=pl.Buffered(1)),
            pl.BlockSpec((d, tf), lambda i, f: (0, f)),
            pl.BlockSpec((d, tf), lambda i, f: (0, f)),
            pl.BlockSpec((tf, d), lambda i, f: (f, 0)),
            pl.BlockSpec((d, tail), lambda i, f: (0, tail_blk)),
            pl.BlockSpec((d, tail), lambda i, f: (0, tail_blk)),
            pl.BlockSpec((tail, d), lambda i, f: (tail_blk, 0)),
            pl.BlockSpec((1, d), const),
            pl.BlockSpec((1, d), const),
        ],
        out_specs=[pl.BlockSpec((tm, d), rowblk), pl.BlockSpec((tm, d), rowblk)],
        out_shape=[jax.ShapeDtypeStruct((s, d), F32), jax.ShapeDtypeStruct((s, d), BF16)],
        compiler_params=_params(("arbitrary", "arbitrary"), vmem),
        name="swiglu_ln",
    )(x16, x32, w1, w3, w2, w1, w3, w2, ln_g, ln_b)


ROUTE_E, ROUTE_W, ROUTE_RANK = 0, 2, 4


def _router_kernel(x_ref, w_ref, route_ref, counts_ref, seen_ref):
    @pl.when(pl.program_id(0) == 0)
    def _():
        seen_ref[...] = jnp.zeros_like(seen_ref)

    x, w = x_ref[...], w_ref[...]
    x_hi, w_hi = x.astype(BF16), w.astype(BF16)
    x_lo, w_lo = (x - x_hi.astype(F32)).astype(BF16), (w - w_hi.astype(F32)).astype(BF16)
    logits = _dot(x_hi, w_hi) + (_dot(x_hi, w_lo) + _dot(x_lo, w_hi))
    tm = logits.shape[0]
    lane = lax.broadcasted_iota(jnp.int32, logits.shape, 1).astype(F32)
    neg = jnp.float32(-jnp.inf)
    l1 = jnp.where(lane < N_EXPERTS, logits, neg)
    m1 = jnp.max(l1, axis=-1, keepdims=True)
    i1 = jnp.min(jnp.where(l1 == m1, lane, float(LANES)), axis=-1, keepdims=True)
    l2 = jnp.where(lane == i1, neg, l1)
    m2 = jnp.max(l2, axis=-1, keepdims=True)
    i2 = jnp.min(jnp.where(l2 == m2, lane, float(LANES)), axis=-1, keepdims=True)
    e2 = jnp.exp(m2 - m1)
    w_top = 1.0 / (1.0 + e2)

    chosen = jnp.where(lane == i1, 1.0, 0.0) + jnp.where(lane == i2, 1.0, 0.0)
    row = lax.broadcasted_iota(jnp.int32, (tm, tm), 0)
    col = lax.broadcasted_iota(jnp.int32, (tm, tm), 1)
    earlier = jnp.where(col < row, 1.0, 0.0).astype(BF16)
    prefix = _dot(earlier, chosen.astype(BF16)) + seen_ref[...]
    r1 = jnp.sum(jnp.where(lane == i1, prefix, 0.0), axis=-1, keepdims=True)
    r2 = jnp.sum(jnp.where(lane == i2, prefix, 0.0), axis=-1, keepdims=True)
    seen_ref[...] += jnp.sum(chosen, axis=0, keepdims=True)
    counts_ref[...] = seen_ref[...]

    fields = ((ROUTE_E, i1), (ROUTE_E + 1, i2), (ROUTE_W, w_top), (ROUTE_W + 1, e2 * w_top),
              (ROUTE_RANK, r1), (ROUTE_RANK + 1, r2))
    route = jnp.zeros_like(logits)
    for at, val in fields:
        route = jnp.where(lane == at, val, route)
    route_ref[...] = route


def _router(x32, w_router_padded):
    s, d = x32.shape
    tm = ROUTER_TM
    vmem = 2 * (tm * d * 4 + d * LANES * 4 + tm * LANES * 4) + 8 * tm * d * 4
    return pl.pallas_call(
        _router_kernel,
        grid=(s // tm,),
        in_specs=[pl.BlockSpec((tm, d), lambda i: (i, 0)), pl.BlockSpec((d, LANES), lambda i: (0, 0))],
        out_specs=[pl.BlockSpec((tm, LANES), lambda i: (i, 0)), pl.BlockSpec((1, LANES), lambda i: (0, 0))],
        out_shape=[jax.ShapeDtypeStruct((s, LANES), F32), jax.ShapeDtypeStruct((1, LANES), F32)],
        scratch_shapes=[pltpu.VMEM((1, LANES), F32)],
        compiler_params=_params(("arbitrary",), vmem),
        name="router_top2",
    )(x32, w_router_padded)


def _row_copy(src, src_row, dst, dst_row, sem):
    return pltpu.make_async_copy(src.at[pl.ds(src_row, 1)], dst.at[pl.ds(dst_row, 1)], sem)


def _dispatch_kernel(pos_ref, x_ref, xs_zero_hbm, xs_hbm, sem):
    del xs_zero_hbm
    tm = DISPATCH_TM

    def start(r, carry):
        for k in range(2):
            _row_copy(x_ref, r, xs_hbm, pos_ref[0, 0, k * tm + r], sem).start(priority=k)
        return carry

    lax.fori_loop(0, tm, start, 0, unroll=ROW_DMA_UNROLL)
    for k in range(2):
        pltpu.make_async_copy(x_ref, xs_hbm.at[pl.ds(0, tm)], sem).wait()


def _dispatch(pos_tiles, x32, n_rows):
    s, d = x32.shape
    tm = DISPATCH_TM
    return pl.pallas_call(
        _dispatch_kernel,
        grid=(s // tm,),
        in_specs=[
            pl.BlockSpec((1, 1, 2 * tm), lambda i: (i, 0, 0), memory_space=pltpu.SMEM),
            pl.BlockSpec((tm, d), lambda i: (i, 0)),
            pl.BlockSpec(memory_space=pl.ANY),
        ],
        out_specs=pl.BlockSpec(memory_space=pl.ANY),
        out_shape=jax.ShapeDtypeStruct((n_rows, d), F32),
        scratch_shapes=[pltpu.SemaphoreType.DMA(())],
        input_output_aliases={2: 0},
        compiler_params=_params(("arbitrary",), 2 * tm * d * 4 + (4 << 20)),
        name="moe_dispatch",
    )(pos_tiles, x32, jnp.zeros((n_rows, d), F32))


def _experts_kernel(tile_expert_ref, tile_rows_ref, n_used_ref, xs_hbm, w1_ref, w3_ref, w2_ref, ys_ref,
                    x16_ref, stage_ref, stage_sem):
    del tile_expert_ref
    i = pl.program_id(0)
    f = pl.program_id(1)
    n_rows = tile_rows_ref[i]
    tm = stage_ref.shape[0]

    def tile_copy(t):
        return pltpu.make_async_copy(xs_hbm.at[pl.ds(pl.multiple_of(t * tm, tm), tm)], stage_ref, stage_sem)

    @pl.when(f == 0)
    def _():
        ys_ref[...] = jnp.zeros_like(ys_ref)

    @pl.when((f == 0) & (i == 0))
    def _():
        tile_copy(0).start()

    @pl.when((f == 0) & (n_rows > 0))
    def _():
        tile_copy(i).wait()

    @pl.when((f == 1) & (i + 1 < n_used_ref[0]))
    def _():
        tile_copy(i + 1).start()

    def swiglu(rows):
        @pl.when(f == 0)
        def _():
            x16_ref[rows] = stage_ref[rows].astype(BF16)

        x = x16_ref[rows]
        h1 = _dot(x, w1_ref[...].astype(BF16))
        h = (h1 * jax.nn.sigmoid(h1) * _dot(x, w3_ref[...].astype(BF16))).astype(BF16)
        ys_ref[rows] += _dot(h, w2_ref[...].astype(BF16))

    for g in range(1, MOE_GRANULES + 1):
        @pl.when((n_rows > (g - 1) * MOE_TM) & (n_rows <= g * MOE_TM))
        def _():
            swiglu(slice(0, g * MOE_TM))


def _experts(tile_expert, tile_rows, n_used, xs, w1, w3, w2):
    n_rows, d = xs.shape
    ff = w1.shape[2]
    tm, tf = MOE_TM * MOE_GRANULES, MOE_TF
    n_f = ff // tf

    def chunk(i, f, nu):
        return jnp.where(i < nu[0], f, n_f - 1)

    assert n_f >= 2
    vmem = tm * d * 4 + 2 * (tm * d * 4 + 3 * d * tf * 4) + tm * d * 2 + 3 * d * tf * 2 + 6 * tm * tf * 4
    return pl.pallas_call(
        _experts_kernel,
        grid_spec=pltpu.PrefetchScalarGridSpec(
            num_scalar_prefetch=3,
            grid=(n_rows // tm, n_f),
            in_specs=[
                pl.BlockSpec(memory_space=pl.ANY),
                pl.BlockSpec((None, d, tf), lambda i, f, te, tr, nu: (te[i], 0, chunk(i, f, nu))),
                pl.BlockSpec((None, d, tf), lambda i, f, te, tr, nu: (te[i], 0, chunk(i, f, nu))),
                pl.BlockSpec((None, tf, d), lambda i, f, te, tr, nu: (te[i], chunk(i, f, nu), 0)),
            ],
            out_specs=pl.BlockSpec((tm, d), lambda i, f, te, tr, nu: (i, 0)),
            scratch_shapes=[pltpu.VMEM((tm, d), BF16), pltpu.VMEM((tm, d), F32), pltpu.SemaphoreType.DMA(())],
        ),
        out_shape=jax.ShapeDtypeStruct((n_rows, d), F32),
        compiler_params=_params(("arbitrary", "arbitrary"), vmem),
        name="moe_experts",
    )(tile_expert, tile_rows, n_used, xs, w1, w3, w2)


def _combine_kernel(pos_ref, pos_next_ref, route_ref, x_ref, ys_hbm, g_ref, b_ref, o_ref, buf, sems):
    tm = COMBINE_TM
    i = pl.program_id(0)
    slot = i % 2

    def start_tile(p_ref, slot):
        def start(r, carry):
            for k in range(2):
                _row_copy(ys_hbm, p_ref[0, 0, k * tm + r], buf.at[slot, k], r, sems.at[slot]).start(priority=k)
            return carry
        lax.fori_loop(0, tm, start, 0, unroll=ROW_DMA_UNROLL)

    @pl.when(i == 0)
    def _():
        start_tile(pos_ref, 0)

    @pl.when(i + 1 < pl.num_programs(0))
    def _():
        start_tile(pos_next_ref, 1 - slot)

    for k in range(2):
        pltpu.make_async_copy(ys_hbm.at[pl.ds(0, tm)], buf.at[slot, k], sems.at[slot]).wait()
    w0 = route_ref[:, ROUTE_W:ROUTE_W + 1]
    w1 = route_ref[:, ROUTE_W + 1:ROUTE_W + 2]
    y = w0 * buf[slot, 0] + w1 * buf[slot, 1]
    o_ref[...] = _layer_norm(ALPHA * x_ref[...] + y, g_ref[...], b_ref[...])


def _combine(pos_tiles, route, x32, ys, ln_g, ln_b):
    s, d = x32.shape
    tm = COMBINE_TM
    n = s // tm
    rowblk = lambda i: (i, 0)
    const = lambda i: (0, 0)
    vmem = 2 * 2 * tm * d * 4 + 2 * (2 * tm * d * 4 + tm * LANES * 4) + 6 * tm * d * 4
    return pl.pallas_call(
        _combine_kernel,
        grid=(n,),
        in_specs=[
            pl.BlockSpec((1, 1, 2 * tm), lambda i: (i, 0, 0), memory_space=pltpu.SMEM),
            pl.BlockSpec((1, 1, 2 * tm), lambda i: (jnp.minimum(i + 1, n - 1), 0, 0),
                         memory_space=pltpu.SMEM),
            pl.BlockSpec((tm, LANES), rowblk),
            pl.BlockSpec((tm, d), rowblk),
            pl.BlockSpec(memory_space=pl.ANY),
            pl.BlockSpec((1, d), const),
            pl.BlockSpec((1, d), const),
        ],
        out_specs=pl.BlockSpec((tm, d), rowblk),
        out_shape=jax.ShapeDtypeStruct((s, d), F32),
        scratch_shapes=[pltpu.VMEM((2, 2, tm, d), F32), pltpu.SemaphoreType.DMA((2,))],
        compiler_params=_params(("arbitrary",), vmem),
        name="moe_combine_ln",
    )(pos_tiles, pos_tiles, route, x32, ys, ln_g, ln_b)


def _pos_tiles(pos, tm):
    s = pos.shape[0]
    return pos.reshape(s // tm, tm, 2).transpose(0, 2, 1).reshape(s // tm, 1, 2 * tm)


def _moe(x32, w_router, w1, w3, w2, ln_g, ln_b):
    s, d = x32.shape
    tm = MOE_TM * MOE_GRANULES
    n_tiles = (2 * s) // tm + N_EXPERTS
    route, counts = _router(x32, jnp.pad(w_router, ((0, 0), (0, LANES - N_EXPERTS))))

    counts = counts[0, :N_EXPERTS].astype(jnp.int32)
    tiles_per_expert = (counts + tm - 1) // tm
    tile_end = jnp.cumsum(tiles_per_expert)
    tile_start = tile_end - tiles_per_expert
    n_used = tile_end[-1:]
    tile_ids = jnp.arange(n_tiles, dtype=jnp.int32)
    tile_expert = jnp.sum(jnp.minimum(tile_ids, n_used - 1)[:, None] >= tile_end[None, :],
                          axis=1).astype(jnp.int32)
    tile_rows = jnp.clip(counts[tile_expert] - (tile_ids - tile_start[tile_expert]) * tm, 0, tm)
    tile_rows = jnp.where(tile_ids < n_used, tile_rows, 0).astype(jnp.int32)
    experts = route[:, ROUTE_E:ROUTE_E + 2].astype(jnp.int32)
    pos = (tile_start * tm)[experts] + route[:, ROUTE_RANK:ROUTE_RANK + 2].astype(jnp.int32)

    xs = _dispatch(_pos_tiles(pos, DISPATCH_TM), x32, n_tiles * tm)
    ys = _experts(tile_expert, tile_rows, n_used, xs, w1, w3, w2)
    return _combine(_pos_tiles(pos, COMBINE_TM), route, x32, ys, ln_g, ln_b)


def kernel(x, w_in, b_gate, sg_w, sg_b, sg_ln_g, sg_ln_b, w_branch_a, w_branch_b, w_out,
           ln1_g, ln1_b, ffn_w1, ffn_w3, ffn_w2, moe_router, moe_w1, moe_w3, moe_w2,
           ln2_g, ln2_b):
    b, s, d = x.shape
    assert (b, s, d) == (1, SEQ, D_MODEL)
    x32 = x.reshape(s, d)
    x16 = x32.astype(BF16)
    for layer in range(DEPTH):
        qkv = _inproj(x16, w_in, layer, 0, 3 * SB_WIDTH, "qkv")
        u = _inproj(x16, w_in, layer, OFF_U, SG_WIDTH, "gelu")
        vn = _inproj(x16, w_in, layer, OFF_VG, SG_WIDTH, "gelu_ln",
                     (sg_ln_g[layer].reshape(1, -1), sg_ln_b[layer].reshape(1, -1)))
        gates = _inproj(x16, w_in, layer, OFF_GATE, 2 * d, "gate", (b_gate[layer].reshape(1, -1),))
        att = _attention(qkv)
        sgu = _sgu(u, vn, sg_w[layer], sg_b[layer].T)
        x32, x16 = _merge(att, sgu, gates, x32,
                          w_branch_a[layer].astype(BF16), w_branch_b[layer].astype(BF16),
                          w_out[layer].astype(BF16),
                          ln1_g[layer].reshape(1, d), ln1_b[layer].reshape(1, d))
        i = layer // 2
        g2, b2 = ln2_g[layer].reshape(1, d), ln2_b[layer].reshape(1, d)
        if layer % 2 == 0:
            x32, x16 = _ffn(x16, x32, ffn_w1[i].astype(BF16), ffn_w3[i].astype(BF16),
                            ffn_w2[i].astype(BF16), g2, b2)
        else:
            x32 = _moe(x32, moe_router[i], moe_w1[i], moe_w3[i], moe_w2[i], g2, b2)
            x16 = x32.astype(BF16) if layer + 1 < DEPTH else None
    return x32.reshape(b, s, d)
```

```python
import functools

import jax
import jax.numpy as jnp
from jax import lax
from jax.experimental import pallas as pl
from jax.experimental.pallas import tpu as pltpu

F32 = jnp.float32
BF16 = jnp.bfloat16

D_MODEL = 2048
SEQ = 8192
DEPTH = 2
HEADS = 8
HEAD_DIM = 128
SB_WIDTH = HEADS * HEAD_DIM
GROUPS = 8
GROUP_DIM = 128
SG_WIDTH = GROUPS * GROUP_DIM
CHUNK = 128
OFF_U = 3 * SB_WIDTH
OFF_VG = OFF_U + SG_WIDTH
OFF_GATE = OFF_VG + SG_WIDTH
N_EXPERTS = 8
ALPHA = (2.0 * DEPTH) ** 0.25
LN_EPS = 1e-5
SB_UNDERFLOW = 110.0
LOG2_E = 1.4426950408889634

LANES = 128
V7X_VMEM_BYTES = 64 * 1024 * 1024

PROJ_TM = 1024
PROJ_TN = 1024
ATT_TQ = 256
ATT_TK = 256
ATT_HEADS = 8
SGU_ROWS = 512
MERGE_TM = 256
FFN_TM = 1024
FFN_TF = 256
LN_ROWS = 256
ROUTER_TM = 512
DISPATCH_TM = 512
MOE_TM = 256
MOE_GRANULES = 5
MOE_TF = 256
COMBINE_TM = 256
ROW_DMA_UNROLL = 8


def _params(semantics, vmem_bytes):
    assert vmem_bytes < V7X_VMEM_BYTES
    return pltpu.CompilerParams(dimension_semantics=semantics, vmem_limit_bytes=vmem_bytes)


def _dot(a, b):
    return jnp.dot(a, b, preferred_element_type=F32)


def _layer_norm(y, g, b):
    mu = jnp.mean(y, axis=-1, keepdims=True)
    d = y - mu
    var = jnp.mean(d * d, axis=-1, keepdims=True)
    return d * lax.rsqrt(var + LN_EPS) * g + b


def _gelu_tanh(x):
    return 0.5 * x * (1.0 + jnp.tanh(0.7978845608028654 * (x + 0.044715 * (x * x * x))))


def _inproj_kernel(x_ref, w_ref, *rest, mode):
    *rest, w16_ref = rest

    @pl.when(pl.program_id(1) == 0)
    def _():
        w16_ref[...] = w_ref[...].astype(BF16)

    acc = _dot(x_ref[...], w16_ref[...])
    if mode == "qkv":
        (o_ref,) = rest
        scale = jnp.where(pl.program_id(0) == 0, HEAD_DIM ** -0.5 * LOG2_E, 1.0).astype(F32)
        o_ref[...] = (acc * scale).astype(o_ref.dtype)
    elif mode == "gelu":
        (o_ref,) = rest
        o_ref[...] = _gelu_tanh(acc).astype(o_ref.dtype)
    elif mode == "gelu_ln":
        g_ref, b_ref, o_ref = rest
        act = _gelu_tanh(acc)
        for grp in range(acc.shape[1] // GROUP_DIM):
            cols = slice(grp * GROUP_DIM, (grp + 1) * GROUP_DIM)
            o_ref[:, cols] = _layer_norm(act[:, cols], g_ref[:, cols], b_ref[:, cols]).astype(o_ref.dtype)
    elif mode == "gate":
        b_ref, o_ref = rest
        o_ref[...] = jax.nn.sigmoid(acc + b_ref[...]).astype(o_ref.dtype)
    else:
        raise ValueError(mode)


def _inproj(x16, w_in, layer, col_off, width, mode, extra=()):
    s, d = x16.shape
    tm, tn = PROJ_TM, PROJ_TN
    n_blk = width // tn
    off_blk = col_off // tn
    in_specs = [
        pl.BlockSpec((tm, d), lambda n, m: (m, 0)),
        pl.BlockSpec((None, d, tn), lambda n, m: (layer, 0, off_blk + n)),
    ]
    for _ in extra:
        in_specs.append(pl.BlockSpec((1, tn), lambda n, m: (0, n)))
    vmem = 2 * (tm * d * 2 + d * tn * 4 + tm * tn * 2) + d * tn * 2 + 6 * tm * tn * 4
    return pl.pallas_call(
        functools.partial(_inproj_kernel, mode=mode),
        grid=(n_blk, s // tm),
        in_specs=in_specs,
        out_specs=pl.BlockSpec((tm, tn), lambda n, m: (m, n)),
        out_shape=jax.ShapeDtypeStruct((s, width), BF16),
        scratch_shapes=[pltpu.VMEM((d, tn), BF16)],
        compiler_params=_params(("arbitrary", "arbitrary"), vmem),
        name="inproj_" + mode,
    )(x16, w_in, *extra)


def _attn_kernel(q_ref, k_ref, v_ref, o_ref, z_ref, ls_ref, hl_ref, tail_ref, a_ref, acc_ref, c_ref):
    tq, t = ATT_TQ, ATT_TK
    tiles_per_q = tq // t
    i = pl.program_id(1)
    krow = lax.broadcasted_iota(jnp.int32, (t, t), 0)
    kcol = lax.broadcasted_iota(jnp.int32, (t, t), 1)
    later = jnp.where(krow > kcol, 1.0, 0.0).astype(BF16)
    later2 = jnp.concatenate([later, later], axis=0)
    row = lax.broadcasted_iota(jnp.int32, (tq, t), 0)
    col = lax.broadcasted_iota(jnp.int32, (tq, t), 1)
    heads = [slice(h * HEAD_DIM, (h + 1) * HEAD_DIM) for h in range(ATT_HEADS)]

    def key_tile(j, mask):
        keys = pl.ds(pl.multiple_of(j * t, t), t)
        for h, hs in enumerate(heads):
            z_ref[h] = lax.dot_general(q_ref[:, hs], k_ref[keys, hs], (((1,), (1,)), ((), ())),
                                       preferred_element_type=F32)
        for h in range(ATT_HEADS):
            z = z_ref[h]
            ls = jnp.minimum(z, 0.0) - jnp.log2(1.0 + jnp.exp2(-jnp.abs(z)))
            lk = ls - z
            if mask is not None:
                lk = jnp.where(mask, lk, 0.0)
            hi = lk.astype(BF16)
            ls_ref[h] = ls
            hl_ref[h, :, :t] = hi
            hl_ref[h, :, t:] = (lk - hi.astype(F32)).astype(BF16)
        for h in range(ATT_HEADS):
            tail_ref[h] = _dot(hl_ref[h], later2)
        for h, hs in enumerate(heads):
            tail = tail_ref[h]
            a = jnp.exp2(ls_ref[h] + tail + jnp.tile(c_ref[:, hs], (1, t // HEAD_DIM)))
            lk0 = ls_ref[h][:, :1] - z_ref[h][:, :1]
            if mask is not None:
                a = jnp.where(mask, a, 0.0)
                lk0 = jnp.where(mask[:, :1], lk0, 0.0)
            a_ref[h] = a.astype(BF16)
            c_ref[:, hs] += jnp.broadcast_to(tail[:, :1] + lk0, (tq, HEAD_DIM))
        for h, hs in enumerate(heads):
            acc_ref[:, hs] += _dot(a_ref[h], v_ref[keys, hs])

    def any_weight_left():
        return jnp.max(c_ref[...]) > -SB_UNDERFLOW * LOG2_E

    acc_ref[...] = jnp.zeros_like(acc_ref)
    c_ref[...] = jnp.zeros_like(c_ref)
    for back in range(tiles_per_q):
        offset = (tiles_per_q - 1 - back) * t
        key_tile(i * tiles_per_q + (tiles_per_q - 1 - back), col + offset < row)

    def cond(carry):
        j, go = carry
        return jnp.logical_and(j >= 0, go)

    def body(carry):
        j, _ = carry
        key_tile(j, None)
        return j - 1, any_weight_left()

    lax.while_loop(cond, body, (i * tiles_per_q - 1, any_weight_left()))
    o_ref[...] = acc_ref[...].astype(o_ref.dtype)


def _attention(qkv):
    s = qkv.shape[0]
    tq, t = ATT_TQ, ATT_TK
    nh = ATT_HEADS
    w = nh * HEAD_DIM
    groups = SB_WIDTH // w
    resident = pl.Buffered(1)
    vmem = 2 * s * w * 2 + 2 * (2 * tq * w * 2) + nh * 16 * tq * t + 2 * tq * w * 4 + (8 << 20)
    return pl.pallas_call(
        _attn_kernel,
        grid=(groups, s // tq),
        in_specs=[
            pl.BlockSpec((tq, w), lambda g, i: (i, g)),
            pl.BlockSpec((s, w), lambda g, i: (0, groups + g), pipeline_mode=resident),
            pl.BlockSpec((s, w), lambda g, i: (0, 2 * groups + g), pipeline_mode=resident),
        ],
        out_specs=pl.BlockSpec((tq, w), lambda g, i: (i, g)),
        out_shape=jax.ShapeDtypeStruct((s, SB_WIDTH), BF16),
        scratch_shapes=[
            pltpu.VMEM((nh, tq, t), F32),
            pltpu.VMEM((nh, tq, t), F32),
            pltpu.VMEM((nh, tq, 2 * t), BF16),
            pltpu.VMEM((nh, tq, t), F32),
            pltpu.VMEM((nh, tq, t), BF16),
            pltpu.VMEM((tq, w), F32),
            pltpu.VMEM((tq, w), F32),
        ],
        compiler_params=_params(("arbitrary", "arbitrary"), vmem),
        name="stickbreak_attn",
    )(qkv, qkv, qkv)


def _sgu_kernel(u_ref, v_ref, w_ref, b_ref, o_ref):
    c = CHUNK
    row = lax.broadcasted_iota(jnp.int32, (c, c), 0)
    col = lax.broadcasted_iota(jnp.int32, (c, c), 1)
    causal = col <= row
    for g in range(GROUPS):
        w = jnp.where(causal, w_ref[g], 0.0).astype(BF16)
        b = b_ref[:, g:g + 1]
        cols = slice(g * GROUP_DIM, (g + 1) * GROUP_DIM)
        for cc in range(SGU_ROWS // c):
            rows = slice(cc * c, (cc + 1) * c)
            mixed = _dot(w, v_ref[rows, cols]) + b
            o_ref[rows, cols] = (u_ref[rows, cols].astype(F32) * mixed).astype(o_ref.dtype)


def _sgu(u, vn, sg_w, sg_b_t):
    s = u.shape[0]
    r = SGU_ROWS
    vmem = 2 * 3 * r * SG_WIDTH * 2 + 2 * GROUPS * CHUNK * CHUNK * 4 + (4 << 20)
    return pl.pallas_call(
        _sgu_kernel,
        grid=(s // r,),
        in_specs=[
            pl.BlockSpec((r, SG_WIDTH), lambda i: (i, 0)),
            pl.BlockSpec((r, SG_WIDTH), lambda i: (i, 0)),
            pl.BlockSpec((GROUPS, CHUNK, CHUNK), lambda i: (0, 0, 0)),
            pl.BlockSpec((CHUNK, GROUPS), lambda i: (0, 0)),
        ],
        out_specs=pl.BlockSpec((r, SG_WIDTH), lambda i: (i, 0)),
        out_shape=jax.ShapeDtypeStruct((s, SG_WIDTH), BF16),
        compiler_params=_params(("arbitrary",), vmem),
        name="spatial_gating",
    )(u, vn, sg_w, sg_b_t)


def _merge_kernel(a_ref, b_ref, gate_ref, x_ref, wa_ref, wb_ref, wo_ref, g_ref, beta_ref,
                  o32_ref, o16_ref):
    d = D_MODEL
    ya = _dot(a_ref[...], wa_ref[...])
    yb = _dot(b_ref[...], wb_ref[...])
    merged = gate_ref[:, :d].astype(F32) * ya + gate_ref[:, d:].astype(F32) * yb
    mix = _dot(merged.astype(BF16), wo_ref[...])
    y = _layer_norm(ALPHA * x_ref[...] + mix, g_ref[...], beta_ref[...])
    o32_ref[...] = y
    o16_ref[...] = y.astype(BF16)


def _merge(att, sgu, gates, x32, wa, wb, wo, ln_g, ln_b):
    s, d = x32.shape
    tm = MERGE_TM
    const = lambda i: (0, 0)
    rowblk = lambda i: (i, 0)
    weights = (SB_WIDTH * d + SG_WIDTH * d + d * d) * 2
    vmem = 2 * weights + 2 * tm * (SB_WIDTH * 2 + SG_WIDTH * 2 + 2 * d * 2 + d * 4 + d * 4 + d * 2) \
        + 6 * tm * d * 4
    return pl.pallas_call(
        _merge_kernel,
        grid=(s // tm,),
        in_specs=[
            pl.BlockSpec((tm, SB_WIDTH), rowblk),
            pl.BlockSpec((tm, SG_WIDTH), rowblk),
            pl.BlockSpec((tm, 2 * d), rowblk),
            pl.BlockSpec((tm, d), rowblk),
            pl.BlockSpec((SB_WIDTH, d), const),
            pl.BlockSpec((SG_WIDTH, d), const),
            pl.BlockSpec((d, d), const),
            pl.BlockSpec((1, d), const),
            pl.BlockSpec((1, d), const),
        ],
        out_specs=[pl.BlockSpec((tm, d), rowblk), pl.BlockSpec((tm, d), rowblk)],
        out_shape=[jax.ShapeDtypeStruct((s, d), F32), jax.ShapeDtypeStruct((s, d), BF16)],
        compiler_params=_params(("arbitrary",), vmem),
        name="merge_outproj_ln",
    )(att, sgu, gates, x32, wa, wb, wo, ln_g, ln_b)


def _swiglu_hidden(x, w1_ref, w3_ref):
    h1 = _dot(x, w1_ref[...])
    return (h1 * jax.nn.sigmoid(h1) * _dot(x, w3_ref[...])).astype(BF16)


def _ffn_kernel(x16_ref, x32_hbm, w1_ref, w3_ref, w2_ref, w1t_ref, w3t_ref, w2t_ref, g_ref, b_ref,
                o32_ref, o16_ref, res_ref, res_sem):
    f = pl.program_id(1)
    tm = res_ref.shape[0]
    residual = pltpu.make_async_copy(
        x32_hbm.at[pl.ds(pl.multiple_of(pl.program_id(0) * tm, tm), tm)], res_ref, res_sem)

    @pl.when(f == 0)
    def _():
        residual.start()
        o32_ref[...] = jnp.zeros_like(o32_ref)

    o32_ref[...] += _dot(_swiglu_hidden(x16_ref[...], w1_ref, w3_ref), w2_ref[...])

    @pl.when(f == pl.num_programs(1) - 1)
    def _():
        o32_ref[...] += _dot(_swiglu_hidden(x16_ref[...], w1t_ref, w3t_ref), w2t_ref[...])
        residual.wait()

        def ln_rows(c, carry):
            rows = pl.ds(pl.multiple_of(c * LN_ROWS, LN_ROWS), LN_ROWS)
            y = _layer_norm(ALPHA * res_ref[rows, :] + o32_ref[rows, :], g_ref[...], b_ref[...])
            o32_ref[rows, :] = y
            o16_ref[rows, :] = y.astype(BF16)
            return carry
        lax.fori_loop(0, o32_ref.shape[0] // LN_ROWS, ln_rows, 0)


def _ffn(x16, x32, w1, w3, w2, ln_g, ln_b):
    s, d = x32.shape
    ff = w1.shape[1]
    tm, tf = FFN_TM, FFN_TF
    n_f = ff // tf
    tail = ff - n_f * tf
    assert 0 < tail < tf and tail % LANES == 0 and (n_f * tf) % tail == 0
    tail_blk = (n_f * tf) // tail
    rowblk = lambda i, f: (i, 0)
    const = lambda i, f: (0, 0)
    vmem = tm * d * 4 + 2 * (tm * d * (2 + 4 + 2) + 3 * d * (tf + tail) * 2) + 5 * tm * tf * 4 \
        + 4 * LN_ROWS * d * 4
    return pl.pallas_call(
        _ffn_kernel,
        grid=(s // tm, n_f),
        in_specs=[
            pl.BlockSpec((tm, d), rowblk),
            pl.BlockSpec(memory_space=pl.ANY),
            pl.BlockSpec((d, tf), lambda i, f: (0, f)),
            pl.BlockSpec((d, tf), lambda i, f: (0, f)),
            pl.BlockSpec((tf, d), lambda i, f: (f, 0)),
            pl.BlockSpec((d, tail), lambda i, f: (0, tail_blk)),
            pl.BlockSpec((d, tail), lambda i, f: (0, tail_blk)),
            pl.BlockSpec((tail, d), lambda i, f: (tail_blk, 0)),
            pl.BlockSpec((1, d), const),
            pl.BlockSpec((1, d), const),
        ],
        out_specs=[pl.BlockSpec((tm, d), rowblk), pl.BlockSpec((tm, d), rowblk)],
        out_shape=[jax.ShapeDtypeStruct((s, d), F32), jax.ShapeDtypeStruct((s, d), BF16)],
        scratch_shapes=[pltpu.VMEM((tm, d), F32), pltpu.SemaphoreType.DMA(())],
        compiler_params=_params(("arbitrary", "arbitrary"), vmem),
        name="swiglu_ln",
    )(x16, x32, w1, w3, w2, w1, w3, w2, ln_g, ln_b)


ROUTE_E, ROUTE_W, ROUTE_RANK = 0, 2, 4


def _router_kernel(x_ref, w_ref, route_ref, counts_ref, seen_ref):
    @pl.when(pl.program_id(0) == 0)
    def _():
        seen_ref[...] = jnp.zeros_like(seen_ref)

    x, w = x_ref[...], w_ref[...]
    x_hi, w_hi = x.astype(BF16), w.astype(BF16)
    x_lo, w_lo = (x - x_hi.astype(F32)).astype(BF16), (w - w_hi.astype(F32)).astype(BF16)
    logits = _dot(x_hi, w_hi) + (_dot(x_hi, w_lo) + _dot(x_lo, w_hi))
    tm = logits.shape[0]
    lane = lax.broadcasted_iota(jnp.int32, logits.shape, 1).astype(F32)
    neg = jnp.float32(-jnp.inf)
    l1 = jnp.where(lane < N_EXPERTS, logits, neg)
    m1 = jnp.max(l1, axis=-1, keepdims=True)
    i1 = jnp.min(jnp.where(l1 == m1, lane, float(LANES)), axis=-1, keepdims=True)
    l2 = jnp.where(lane == i1, neg, l1)
    m2 = jnp.max(l2, axis=-1, keepdims=True)
    i2 = jnp.min(jnp.where(l2 == m2, lane, float(LANES)), axis=-1, keepdims=True)
    e2 = jnp.exp(m2 - m1)
    w_top = 1.0 / (1.0 + e2)

    chosen = jnp.where(lane == i1, 1.0, 0.0) + jnp.where(lane == i2, 1.0, 0.0)
    row = lax.broadcasted_iota(jnp.int32, (tm, tm), 0)
    col = lax.broadcasted_iota(jnp.int32, (tm, tm), 1)
    earlier = jnp.where(col < row, 1.0, 0.0).astype(BF16)
    prefix = _dot(earlier, chosen.astype(BF16)) + seen_ref[...]
    r1 = jnp.sum(jnp.where(lane == i1, prefix, 0.0), axis=-1, keepdims=True)
    r2 = jnp.sum(jnp.where(lane == i2, prefix, 0.0), axis=-1, keepdims=True)
    seen_ref[...] += jnp.sum(chosen, axis=0, keepdims=True)
    counts_ref[...] = seen_ref[...]

    fields = ((ROUTE_E, i1), (ROUTE_E + 1, i2), (ROUTE_W, w_top), (ROUTE_W + 1, e2 * w_top),
              (ROUTE_RANK, r1), (ROUTE_RANK + 1, r2))
    route = jnp.zeros_like(logits)
    for at, val in fields:
        route = jnp.where(lane == at, val, route)
    route_ref[...] = route


def _router(x32, w_router_padded):
    s, d = x32.shape
    tm = ROUTER_TM
    vmem = 2 * (tm * d * 4 + d * LANES * 4 + tm * LANES * 4) + 8 * tm * d * 4
    return pl.pallas_call(
        _router_kernel,
        grid=(s // tm,),
        in_specs=[pl.BlockSpec((tm, d), lambda i: (i, 0)), pl.BlockSpec((d, LANES), lambda i: (0, 0))],
        out_specs=[pl.BlockSpec((tm, LANES), lambda i: (i, 0)), pl.BlockSpec((1, LANES), lambda i: (0, 0))],
        out_shape=[jax.ShapeDtypeStruct((s, LANES), F32), jax.ShapeDtypeStruct((1, LANES), F32)],
        scratch_shapes=[pltpu.VMEM((1, LANES), F32)],
        compiler_params=_params(("arbitrary",), vmem),
        name="router_top2",
    )(x32, w_router_padded)


def _row_copy(src, src_row, dst, dst_row, sem):
    return pltpu.make_async_copy(src.at[pl.ds(src_row, 1)], dst.at[pl.ds(dst_row, 1)], sem)


def _dispatch_kernel(pos_ref, x_ref, xs_zero_hbm, xs_hbm, sem):
    del xs_zero_hbm
    tm = DISPATCH_TM

    def start(r, carry):
        for k in range(2):
            _row_copy(x_ref, r, xs_hbm, pos_ref[0, 0, k * tm + r], sem).start(priority=k)
        return carry

    lax.fori_loop(0, tm, start, 0, unroll=ROW_DMA_UNROLL)
    for k in range(2):
        pltpu.make_async_copy(x_ref, xs_hbm.at[pl.ds(0, tm)], sem).wait()


def _dispatch(pos_tiles, x32, n_rows):
    s, d = x32.shape
    tm = DISPATCH_TM
    return pl.pallas_call(
        _dispatch_kernel,
        grid=(s // tm,),
        in_specs=[
            pl.BlockSpec((1, 1, 2 * tm), lambda i: (i, 0, 0), memory_space=pltpu.SMEM),
            pl.BlockSpec((tm, d), lambda i: (i, 0)),
            pl.BlockSpec(memory_space=pl.ANY),
        ],
        out_specs=pl.BlockSpec(memory_space=pl.ANY),
        out_shape=jax.ShapeDtypeStruct((n_rows, d), F32),
        scratch_shapes=[pltpu.SemaphoreType.DMA(())],
        input_output_aliases={2: 0},
        compiler_params=_params(("arbitrary",), 2 * tm * d * 4 + (4 << 20)),
        name="moe_dispatch",
    )(pos_tiles, x32, jnp.zeros((n_rows, d), F32))


def _experts_kernel(tile_expert_ref, tile_rows_ref, n_used_ref, xs_hbm, w1_ref, w3_ref, w2_ref, ys_ref,
                    x16_ref, stage_ref, stage_sem):
    del tile_expert_ref
    i = pl.program_id(0)
    f = pl.program_id(1)
    n_rows = tile_rows_ref[i]
    tm = stage_ref.shape[0]

    def tile_copy(t):
        return pltpu.make_async_copy(xs_hbm.at[pl.ds(pl.multiple_of(t * tm, tm), tm)], stage_ref, stage_sem)

    @pl.when(f == 0)
    def _():
        ys_ref[...] = jnp.zeros_like(ys_ref)

    @pl.when((f == 0) & (i == 0))
    def _():
        tile_copy(0).start()

    @pl.when((f == 0) & (n_rows > 0))
    def _():
        tile_copy(i).wait()

    @pl.when((f == 1) & (i + 1 < n_used_ref[0]))
    def _():
        tile_copy(i + 1).start()

    def swiglu(rows):
        @pl.when(f == 0)
        def _():
            x16_ref[rows] = stage_ref[rows].astype(BF16)

        x = x16_ref[rows]
        h1 = _dot(x, w1_ref[...].astype(BF16))
        h = (h1 * jax.nn.sigmoid(h1) * _dot(x, w3_ref[...].astype(BF16))).astype(BF16)
        ys_ref[rows] += _dot(h, w2_ref[...].astype(BF16))

    for g in range(1, MOE_GRANULES + 1):
        @pl.when((n_rows > (g - 1) * MOE_TM) & (n_rows <= g * MOE_TM))
        def _():
            swiglu(slice(0, g * MOE_TM))


def _experts(tile_expert, tile_rows, n_used, xs, w1, w3, w2):
    n_rows, d = xs.shape
    ff = w1.shape[2]
    tm, tf = MOE_TM * MOE_GRANULES, MOE_TF
    n_f = ff // tf

    def chunk(i, f, nu):
        return jnp.where(i < nu[0], f, n_f - 1)

    assert n_f >= 2
    vmem = tm * d * 4 + 2 * (tm * d * 4 + 3 * d * tf * 4) + tm * d * 2 + 3 * d * tf * 2 + 6 * tm * tf * 4
    return pl.pallas_call(
        _experts_kernel,
        grid_spec=pltpu.PrefetchScalarGridSpec(
            num_scalar_prefetch=3,
            grid=(n_rows // tm, n_f),
            in_specs=[
                pl.BlockSpec(memory_space=pl.ANY),
                pl.BlockSpec((None, d, tf), lambda i, f, te, tr, nu: (te[i], 0, chunk(i, f, nu))),
                pl.BlockSpec((None, d, tf), lambda i, f, te, tr, nu: (te[i], 0, chunk(i, f, nu))),
                pl.BlockSpec((None, tf, d), lambda i, f, te, tr, nu: (te[i], chunk(i, f, nu), 0)),
            ],
            out_specs=pl.BlockSpec((tm, d), lambda i, f, te, tr, nu: (i, 0)),
            scratch_shapes=[pltpu.VMEM((tm, d), BF16), pltpu.VMEM((tm, d), F32), pltpu.SemaphoreType.DMA(())],
        ),
        out_shape=jax.ShapeDtypeStruct((n_rows, d), F32),
        compiler_params=_params(("arbitrary", "arbitrary"), vmem),
        name="moe_experts",
    )(tile_expert, tile_rows, n_used, xs, w1, w3, w2)


def _combine_kernel(pos_ref, pos_next_ref, route_ref, x_ref, ys_hbm, g_ref, b_ref, o_ref, buf, sems):
    tm = COMBINE_TM
    i = pl.program_id(0)
    slot = i % 2

    def start_tile(p_ref, slot):
        def start(r, carry):
            for k in range(2):
                _row_copy(ys_hbm, p_ref[0, 0, k * tm + r], buf.at[slot, k], r, sems.at[slot]).start(priority=k)
            return carry
        lax.fori_loop(0, tm, start, 0, unroll=ROW_DMA_UNROLL)

    @pl.when(i == 0)
    def _():
        start_tile(pos_ref, 0)

    @pl.when(i + 1 < pl.num_programs(0))
    def _():
        start_tile(pos_next_ref, 1 - slot)

    for k in range(2):
        pltpu.make_async_copy(ys_hbm.at[pl.ds(0, tm)], buf.at[slot, k], sems.at[slot]).wait()
    w0 = route_ref[:, ROUTE_W:ROUTE_W + 1]
    w1 = route_ref[:, ROUTE_W + 1:ROUTE_W + 2]
    y = w0 * buf[slot, 0] + w1 * buf[slot, 1]
    o_ref[...] = _layer_norm(ALPHA * x_ref[...] + y, g_ref[...], b_ref[...])


def _combine(pos_tiles, route, x32, ys, ln_g, ln_b):
    s, d = x32.shape
    tm = COMBINE_TM
    n = s // tm
    rowblk = lambda i: (i, 0)
    const = lambda i: (0, 0)
    vmem = 2 * 2 * tm * d * 4 + 2 * (2 * tm * d * 4 + tm * LANES * 4) + 6 * tm * d * 4
    return pl.pallas_call(
        _combine_kernel,
        grid=(n,),
        in_specs=[
            pl.BlockSpec((1, 1, 2 * tm), lambda i: (i, 0, 0), memory_space=pltpu.SMEM),
            pl.BlockSpec((1, 1, 2 * tm), lambda i: (jnp.minimum(i + 1, n - 1), 0, 0),
                         memory_space=pltpu.SMEM),
            pl.BlockSpec((tm, LANES), rowblk),
            pl.BlockSpec((tm, d), rowblk),
            pl.BlockSpec(memory_space=pl.ANY),
            pl.BlockSpec((1, d), const),
            pl.BlockSpec((1, d), const),
        ],
        out_specs=pl.BlockSpec((tm, d), rowblk),
        out_shape=jax.ShapeDtypeStruct((s, d), F32),
        scratch_shapes=[pltpu.VMEM((2, 2, tm, d), F32), pltpu.SemaphoreType.DMA((2,))],
        compiler_params=_params(("arbitrary",), vmem),
        name="moe_combine_ln",
    )(pos_tiles, pos_tiles, route, x32, ys, ln_g, ln_b)


def _pos_tiles(pos, tm):
    s = pos.shape[0]
    return pos.reshape(s // tm, tm, 2).transpose(0, 2, 1).reshape(s // tm, 1, 2 * tm)


def _moe(x32, w_router, w1, w3, w2, ln_g, ln_b):
    s, d = x32.shape
    tm = MOE_TM * MOE_GRANULES
    n_tiles = (2 * s) // tm + N_EXPERTS
    route, counts = _router(x32, jnp.pad(w_router, ((0, 0), (0, LANES - N_EXPERTS))))

    counts = counts[0, :N_EXPERTS].astype(jnp.int32)
    tiles_per_expert = (counts + tm - 1) // tm
    tile_end = jnp.cumsum(tiles_per_expert)
    tile_start = tile_end - tiles_per_expert
    n_used = tile_end[-1:]
    tile_ids = jnp.arange(n_tiles, dtype=jnp.int32)
    tile_expert = jnp.sum(jnp.minimum(tile_ids, n_used - 1)[:, None] >= tile_end[None, :],
                          axis=1).astype(jnp.int32)
    tile_rows = jnp.clip(counts[tile_expert] - (tile_ids - tile_start[tile_expert]) * tm, 0, tm)
    tile_rows = jnp.where(tile_ids < n_used, tile_rows, 0).astype(jnp.int32)
    experts = route[:, ROUTE_E:ROUTE_E + 2].astype(jnp.int32)
    pos = (tile_start * tm)[experts] + route[:, ROUTE_RANK:ROUTE_RANK + 2].astype(jnp.int32)

    xs = _dispatch(_pos_tiles(pos, DISPATCH_TM), x32, n_tiles * tm)
    ys = _experts(tile_expert, tile_rows, n_used, xs, w1, w3, w2)
    return _combine(_pos_tiles(pos, COMBINE_TM), route, x32, ys, ln_g, ln_b)


def kernel(x, w_in, b_gate, sg_w, sg_b, sg_ln_g, sg_ln_b, w_branch_a, w_branch_b, w_out,
           ln1_g, ln1_b, ffn_w1, ffn_w3, ffn_w2, moe_router, moe_w1, moe_w3, moe_w2,
           ln2_g, ln2_b):
    b, s, d = x.shape
    assert (b, s, d) == (1, SEQ, D_MODEL)
    x32 = x.reshape(s, d)
    x16 = x32.astype(BF16)
    for layer in range(DEPTH):
        qkv = _inproj(x16, w_in, layer, 0, 3 * SB_WIDTH, "qkv")
        u = _inproj(x16, w_in, layer, OFF_U, SG_WIDTH, "gelu")
        vn = _inproj(x16, w_in, layer, OFF_VG, SG_WIDTH, "gelu_ln",
                     (sg_ln_g[layer].reshape(1, -1), sg_ln_b[layer].reshape(1, -1)))
        gates = _inproj(x16, w_in, layer, OFF_GATE, 2 * d, "gate", (b_gate[layer].reshape(1, -1),))
        att = _attention(qkv)
        sgu = _sgu(u, vn, sg_w[layer], sg_b[layer].T)
        x32, x16 = _merge(att, sgu, gates, x32,
                          w_branch_a[layer].astype(BF16), w_branch_b[layer].astype(BF16),
                          w_out[layer].astype(BF16),
                          ln1_g[layer].reshape(1, d), ln1_b[layer].reshape(1, d))
        i = layer // 2
        g2, b2 = ln2_g[layer].reshape(1, d), ln2_b[layer].reshape(1, d)
        if layer % 2 == 0:
            x32, x16 = _ffn(x16, x32, ffn_w1[i].astype(BF16), ffn_w3[i].astype(BF16),
                            ffn_w2[i].astype(BF16), g2, b2)
        else:
            x32 = _moe(x32, moe_router[i], moe_w1[i], moe_w3[i], moe_w2[i], g2, b2)
            x16 = x32.astype(BF16) if layer + 1 < DEPTH else None
    return x32.reshape(b, s, d)
```

```python
import functools

import jax
import jax.numpy as jnp
from jax import lax
from jax.experimental import pallas as pl
from jax.experimental.pallas import tpu as pltpu

F32 = jnp.float32
BF16 = jnp.bfloat16

D_MODEL = 2048
SEQ = 8192
DEPTH = 2
HEADS = 8
HEAD_DIM = 128
SB_WIDTH = HEADS * HEAD_DIM
GROUPS = 8
GROUP_DIM = 128
SG_WIDTH = GROUPS * GROUP_DIM
CHUNK = 128
OFF_U = 3 * SB_WIDTH
OFF_VG = OFF_U + SG_WIDTH
OFF_GATE = OFF_VG + SG_WIDTH
N_EXPERTS = 8
ALPHA = (2.0 * DEPTH) ** 0.25
LN_EPS = 1e-5
SB_UNDERFLOW = 110.0
LOG2_E = 1.4426950408889634

LANES = 128
V7X_VMEM_BYTES = 64 * 1024 * 1024

PROJ_TM = 1024
PROJ_TN = 1024
ATT_TQ = 256
ATT_TK = 256
ATT_HEADS = 8
SGU_ROWS = 512
MERGE_TM = 256
FFN_TM = 1024
FFN_TF = 256
LN_ROWS = 256
ROUTER_TM = 512
DISPATCH_TM = 512
MOE_TM = 256
MOE_GRANULES = 5
MOE_TF = 256
COMBINE_TM = 512
ROW_DMA_UNROLL = 16


def _params(semantics, vmem_bytes):
    assert vmem_bytes < V7X_VMEM_BYTES
    return pltpu.CompilerParams(dimension_semantics=semantics, vmem_limit_bytes=vmem_bytes)


def _dot(a, b):
    return jnp.dot(a, b, preferred_element_type=F32)


def _layer_norm(y, g, b):
    mu = jnp.mean(y, axis=-1, keepdims=True)
    d = y - mu
    var = jnp.mean(d * d, axis=-1, keepdims=True)
    return d * lax.rsqrt(var + LN_EPS) * g + b


def _gelu_tanh(x):
    return 0.5 * x * (1.0 + jnp.tanh(0.7978845608028654 * (x + 0.044715 * (x * x * x))))


def _inproj_kernel(x_ref, w_ref, *rest, mode):
    *rest, w16_ref = rest

    @pl.when(pl.program_id(1) == 0)
    def _():
        w16_ref[...] = w_ref[...].astype(BF16)

    acc = _dot(x_ref[...], w16_ref[...])
    if mode == "qkv":
        (o_ref,) = rest
        scale = jnp.where(pl.program_id(0) == 0, HEAD_DIM ** -0.5 * LOG2_E, 1.0).astype(F32)
        o_ref[...] = (acc * scale).astype(o_ref.dtype)
    elif mode == "gelu":
        (o_ref,) = rest
        o_ref[...] = _gelu_tanh(acc).astype(o_ref.dtype)
    elif mode == "gelu_ln":
        g_ref, b_ref, o_ref = rest
        act = _gelu_tanh(acc)
        for grp in range(acc.shape[1] // GROUP_DIM):
            cols = slice(grp * GROUP_DIM, (grp + 1) * GROUP_DIM)
            o_ref[:, cols] = _layer_norm(act[:, cols], g_ref[:, cols], b_ref[:, cols]).astype(o_ref.dtype)
    elif mode == "gate":
        b_ref, o_ref = rest
        o_ref[...] = jax.nn.sigmoid(acc + b_ref[...]).astype(o_ref.dtype)
    else:
        raise ValueError(mode)


def _inproj(x16, w_in, layer, col_off, width, mode, extra=()):
    s, d = x16.shape
    tm, tn = PROJ_TM, PROJ_TN
    n_blk = width // tn
    off_blk = col_off // tn
    in_specs = [
        pl.BlockSpec((tm, d), lambda n, m: (m, 0)),
        pl.BlockSpec((None, d, tn), lambda n, m: (layer, 0, off_blk + n)),
    ]
    for _ in extra:
        in_specs.append(pl.BlockSpec((1, tn), lambda n, m: (0, n)))
    vmem = 2 * (tm * d * 2 + d * tn * 4 + tm * tn * 2) + d * tn * 2 + 6 * tm * tn * 4
    return pl.pallas_call(
        functools.partial(_inproj_kernel, mode=mode),
        grid=(n_blk, s // tm),
        in_specs=in_specs,
        out_specs=pl.BlockSpec((tm, tn), lambda n, m: (m, n)),
        out_shape=jax.ShapeDtypeStruct((s, width), BF16),
        scratch_shapes=[pltpu.VMEM((d, tn), BF16)],
        compiler_params=_params(("arbitrary", "arbitrary"), vmem),
        name="inproj_" + mode,
    )(x16, w_in, *extra)


def _attn_kernel(q_ref, k_ref, v_ref, o_ref, z_ref, ls_ref, hl_ref, tail_ref, a_ref, acc_ref, c_ref):
    tq, t = ATT_TQ, ATT_TK
    tiles_per_q = tq // t
    i = pl.program_id(1)
    krow = lax.broadcasted_iota(jnp.int32, (t, t), 0)
    kcol = lax.broadcasted_iota(jnp.int32, (t, t), 1)
    later = jnp.where(krow > kcol, 1.0, 0.0).astype(BF16)
    later2 = jnp.concatenate([later, later], axis=0)
    row = lax.broadcasted_iota(jnp.int32, (tq, t), 0)
    col = lax.broadcasted_iota(jnp.int32, (tq, t), 1)
    heads = [slice(h * HEAD_DIM, (h + 1) * HEAD_DIM) for h in range(ATT_HEADS)]

    def key_tile(j, mask):
        keys = pl.ds(pl.multiple_of(j * t, t), t)
        for h, hs in enumerate(heads):
            z_ref[h] = lax.dot_general(q_ref[:, hs], k_ref[keys, hs], (((1,), (1,)), ((), ())),
                                       preferred_element_type=F32)
        for h in range(ATT_HEADS):
            z = z_ref[h]
            ls = jnp.minimum(z, 0.0) - jnp.log2(1.0 + jnp.exp2(-jnp.abs(z)))
            lk = ls - z
            if mask is not None:
                lk = jnp.where(mask, lk, 0.0)
            hi = lk.astype(BF16)
            ls_ref[h] = ls
            hl_ref[h, :, :t] = hi
            hl_ref[h, :, t:] = (lk - hi.astype(F32)).astype(BF16)
        for h in range(ATT_HEADS):
            tail_ref[h] = _dot(hl_ref[h], later2)
        for h, hs in enumerate(heads):
            tail = tail_ref[h]
            a = jnp.exp2(ls_ref[h] + tail + jnp.tile(c_ref[:, hs], (1, t // HEAD_DIM)))
            lk0 = ls_ref[h][:, :1] - z_ref[h][:, :1]
            if mask is not None:
                a = jnp.where(mask, a, 0.0)
                lk0 = jnp.where(mask[:, :1], lk0, 0.0)
            a_ref[h] = a.astype(BF16)
            c_ref[:, hs] += jnp.broadcast_to(tail[:, :1] + lk0, (tq, HEAD_DIM))
        for h, hs in enumerate(heads):
            acc_ref[:, hs] += _dot(a_ref[h], v_ref[keys, hs])

    def any_weight_left():
        return jnp.max(c_ref[...]) > -SB_UNDERFLOW * LOG2_E

    acc_ref[...] = jnp.zeros_like(acc_ref)
    c_ref[...] = jnp.zeros_like(c_ref)
    for back in range(tiles_per_q):
        offset = (tiles_per_q - 1 - back) * t
        key_tile(i * tiles_per_q + (tiles_per_q - 1 - back), col + offset < row)

    def cond(carry):
        j, go = carry
        return jnp.logical_and(j >= 0, go)

    def body(carry):
        j, _ = carry
        key_tile(j, None)
        return j - 1, any_weight_left()

    lax.while_loop(cond, body, (i * tiles_per_q - 1, any_weight_left()))
    o_ref[...] = acc_ref[...].astype(o_ref.dtype)


def _attention(qkv):
    s = qkv.shape[0]
    tq, t = ATT_TQ, ATT_TK
    nh = ATT_HEADS
    w = nh * HEAD_DIM
    groups = SB_WIDTH // w
    resident = pl.Buffered(1)
    vmem = 2 * s * w * 2 + 2 * (2 * tq * w * 2) + nh * 16 * tq * t + 2 * tq * w * 4 + (8 << 20)
    return pl.pallas_call(
        _attn_kernel,
        grid=(groups, s // tq),
        in_specs=[
            pl.BlockSpec((tq, w), lambda g, i: (i, g)),
            pl.BlockSpec((s, w), lambda g, i: (0, groups + g), pipeline_mode=resident),
            pl.BlockSpec((s, w), lambda g, i: (0, 2 * groups + g), pipeline_mode=resident),
        ],
        out_specs=pl.BlockSpec((tq, w), lambda g, i: (i, g)),
        out_shape=jax.ShapeDtypeStruct((s, SB_WIDTH), BF16),
        scratch_shapes=[
            pltpu.VMEM((nh, tq, t), F32),
            pltpu.VMEM((nh, tq, t), F32),
            pltpu.VMEM((nh, tq, 2 * t), BF16),
            pltpu.VMEM((nh, tq, t), F32),
            pltpu.VMEM((nh, tq, t), BF16),
            pltpu.VMEM((tq, w), F32),
            pltpu.VMEM((tq, w), F32),
        ],
        compiler_params=_params(("arbitrary", "arbitrary"), vmem),
        name="stickbreak_attn",
    )(qkv, qkv, qkv)


def _sgu_kernel(u_ref, v_ref, w_ref, b_ref, o_ref):
    c = CHUNK
    row = lax.broadcasted_iota(jnp.int32, (c, c), 0)
    col = lax.broadcasted_iota(jnp.int32, (c, c), 1)
    causal = col <= row
    for g in range(GROUPS):
        w = jnp.where(causal, w_ref[g], 0.0).astype(BF16)
        b = b_ref[:, g:g + 1]
        cols = slice(g * GROUP_DIM, (g + 1) * GROUP_DIM)
        for cc in range(SGU_ROWS // c):
            rows = slice(cc * c, (cc + 1) * c)
            mixed = _dot(w, v_ref[rows, cols]) + b
            o_ref[rows, cols] = (u_ref[rows, cols].astype(F32) * mixed).astype(o_ref.dtype)


def _sgu(u, vn, sg_w, sg_b_t):
    s = u.shape[0]
    r = SGU_ROWS
    vmem = 2 * 3 * r * SG_WIDTH * 2 + 2 * GROUPS * CHUNK * CHUNK * 4 + (4 << 20)
    return pl.pallas_call(
        _sgu_kernel,
        grid=(s // r,),
        in_specs=[
            pl.BlockSpec((r, SG_WIDTH), lambda i: (i, 0)),
            pl.BlockSpec((r, SG_WIDTH), lambda i: (i, 0)),
            pl.BlockSpec((GROUPS, CHUNK, CHUNK), lambda i: (0, 0, 0)),
            pl.BlockSpec((CHUNK, GROUPS), lambda i: (0, 0)),
        ],
        out_specs=pl.BlockSpec((r, SG_WIDTH), lambda i: (i, 0)),
        out_shape=jax.ShapeDtypeStruct((s, SG_WIDTH), BF16),
        compiler_params=_params(("arbitrary",), vmem),
        name="spatial_gating",
    )(u, vn, sg_w, sg_b_t)


def _merge_kernel(a_ref, b_ref, gate_ref, x_ref, wa_ref, wb_ref, wo_ref, g_ref, beta_ref,
                  o32_ref, o16_ref):
    d = D_MODEL
    ya = _dot(a_ref[...], wa_ref[...])
    yb = _dot(b_ref[...], wb_ref[...])
    merged = gate_ref[:, :d].astype(F32) * ya + gate_ref[:, d:].astype(F32) * yb
    mix = _dot(merged.astype(BF16), wo_ref[...])
    y = _layer_norm(ALPHA * x_ref[...] + mix, g_ref[...], beta_ref[...])
    o32_ref[...] = y
    o16_ref[...] = y.astype(BF16)


def _merge(att, sgu, gates, x32, wa, wb, wo, ln_g, ln_b):
    s, d = x32.shape
    tm = MERGE_TM
    const = lambda i: (0, 0)
    rowblk = lambda i: (i, 0)
    weights = (SB_WIDTH * d + SG_WIDTH * d + d * d) * 2
    vmem = 2 * weights + 2 * tm * (SB_WIDTH * 2 + SG_WIDTH * 2 + 2 * d * 2 + d * 4 + d * 4 + d * 2) \
        + 6 * tm * d * 4
    return pl.pallas_call(
        _merge_kernel,
        grid=(s // tm,),
        in_specs=[
            pl.BlockSpec((tm, SB_WIDTH), rowblk),
            pl.BlockSpec((tm, SG_WIDTH), rowblk),
            pl.BlockSpec((tm, 2 * d), rowblk),
            pl.BlockSpec((tm, d), rowblk),
            pl.BlockSpec((SB_WIDTH, d), const),
            pl.BlockSpec((SG_WIDTH, d), const),
            pl.BlockSpec((d, d), const),
            pl.BlockSpec((1, d), const),
            pl.BlockSpec((1, d), const),
        ],
        out_specs=[pl.BlockSpec((tm, d), rowblk), pl.BlockSpec((tm, d), rowblk)],
        out_shape=[jax.ShapeDtypeStruct((s, d), F32), jax.ShapeDtypeStruct((s, d), BF16)],
        compiler_params=_params(("arbitrary",), vmem),
        name="merge_outproj_ln",
    )(att, sgu, gates, x32, wa, wb, wo, ln_g, ln_b)


def _swiglu_hidden(x, w1_ref, w3_ref):
    h1 = _dot(x, w1_ref[...])
    return (h1 * jax.nn.sigmoid(h1) * _dot(x, w3_ref[...])).astype(BF16)


def _ffn_kernel(x16_ref, x32_hbm, w1_ref, w3_ref, w2_ref, w1t_ref, w3t_ref, w2t_ref, g_ref, b_ref,
                o32_ref, o16_ref, res_ref, res_sem):
    f = pl.program_id(1)
    tm = res_ref.shape[0]
    residual = pltpu.make_async_copy(
        x32_hbm.at[pl.ds(pl.multiple_of(pl.program_id(0) * tm, tm), tm)], res_ref, res_sem)

    @pl.when(f == 0)
    def _():
        residual.start()
        o32_ref[...] = jnp.zeros_like(o32_ref)

    o32_ref[...] += _dot(_swiglu_hidden(x16_ref[...], w1_ref, w3_ref), w2_ref[...])

    @pl.when(f == pl.num_programs(1) - 1)
    def _():
        o32_ref[...] += _dot(_swiglu_hidden(x16_ref[...], w1t_ref, w3t_ref), w2t_ref[...])
        residual.wait()

        def ln_rows(c, carry):
            rows = pl.ds(pl.multiple_of(c * LN_ROWS, LN_ROWS), LN_ROWS)
            y = _layer_norm(ALPHA * res_ref[rows, :] + o32_ref[rows, :], g_ref[...], b_ref[...])
            o32_ref[rows, :] = y
            o16_ref[rows, :] = y.astype(BF16)
            return carry
        lax.fori_loop(0, o32_ref.shape[0] // LN_ROWS, ln_rows, 0)


def _ffn(x16, x32, w1, w3, w2, ln_g, ln_b):
    s, d = x32.shape
    ff = w1.shape[1]
    tm, tf = FFN_TM, FFN_TF
    n_f = ff // tf
    tail = ff - n_f * tf
    assert 0 < tail < tf and tail % LANES == 0 and (n_f * tf) % tail == 0
    tail_blk = (n_f * tf) // tail
    rowblk = lambda i, f: (i, 0)
    const = lambda i, f: (0, 0)
    vmem = tm * d * 4 + 2 * (tm * d * (2 + 4 + 2) + 3 * d * (tf + tail) * 2) + 5 * tm * tf * 4 \
        + 4 * LN_ROWS * d * 4
    return pl.pallas_call(
        _ffn_kernel,
        grid=(s // tm, n_f),
        in_specs=[
            pl.BlockSpec((tm, d), rowblk),
            pl.BlockSpec(memory_space=pl.ANY),
            pl.BlockSpec((d, tf), lambda i, f: (0, f)),
            pl.BlockSpec((d, tf), lambda i, f: (0, f)),
            pl.BlockSpec((tf, d), lambda i, f: (f, 0)),
            pl.BlockSpec((d, tail), lambda i, f: (0, tail_blk)),
            pl.BlockSpec((d, tail), lambda i, f: (0, tail_blk)),
            pl.BlockSpec((tail, d), lambda i, f: (tail_blk, 0)),
            pl.BlockSpec((1, d), const),
            pl.BlockSpec((1, d), const),
        ],
        out_specs=[pl.BlockSpec((tm, d), rowblk), pl.BlockSpec((tm, d), rowblk)],
        out_shape=[jax.ShapeDtypeStruct((s, d), F32), jax.ShapeDtypeStruct((s, d), BF16)],
        scratch_shapes=[pltpu.VMEM((tm, d), F32), pltpu.SemaphoreType.DMA(())],
        compiler_params=_params(("arbitrary", "arbitrary"), vmem),
        name="swiglu_ln",
    )(x16, x32, w1, w3, w2, w1, w3, w2, ln_g, ln_b)


ROUTE_E, ROUTE_W, ROUTE_RANK = 0, 2, 4


def _router_kernel(x_ref, w_ref, route_ref, counts_ref, seen_ref):
    @pl.when(pl.program_id(0) == 0)
    def _():
        seen_ref[...] = jnp.zeros_like(seen_ref)

    x, w = x_ref[...], w_ref[...]
    x_hi, w_hi = x.astype(BF16), w.astype(BF16)
    x_lo, w_lo = (x - x_hi.astype(F32)).astype(BF16), (w - w_hi.astype(F32)).astype(BF16)
    logits = _dot(x_hi, w_hi) + (_dot(x_hi, w_lo) + _dot(x_lo, w_hi))
    tm = logits.shape[0]
    lane = lax.broadcasted_iota(jnp.int32, logits.shape, 1).astype(F32)
    neg = jnp.float32(-jnp.inf)
    l1 = jnp.where(lane < N_EXPERTS, logits, neg)
    m1 = jnp.max(l1, axis=-1, keepdims=True)
    i1 = jnp.min(jnp.where(l1 == m1, lane, float(LANES)), axis=-1, keepdims=True)
    l2 = jnp.where(lane == i1, neg, l1)
    m2 = jnp.max(l2, axis=-1, keepdims=True)
    i2 = jnp.min(jnp.where(l2 == m2, lane, float(LANES)), axis=-1, keepdims=True)
    e2 = jnp.exp(m2 - m1)
    w_top = 1.0 / (1.0 + e2)

    chosen = jnp.where(lane == i1, 1.0, 0.0) + jnp.where(lane == i2, 1.0, 0.0)
    row = lax.broadcasted_iota(jnp.int32, (tm, tm), 0)
    col = lax.broadcasted_iota(jnp.int32, (tm, tm), 1)
    earlier = jnp.where(col < row, 1.0, 0.0).astype(BF16)
    prefix = _dot(earlier, chosen.astype(BF16)) + seen_ref[...]
    r1 = jnp.sum(jnp.where(lane == i1, prefix, 0.0), axis=-1, keepdims=True)
    r2 = jnp.sum(jnp.where(lane == i2, prefix, 0.0), axis=-1, keepdims=True)
    seen_ref[...] += jnp.sum(chosen, axis=0, keepdims=True)
    counts_ref[...] = seen_ref[...]

    fields = ((ROUTE_E, i1), (ROUTE_E + 1, i2), (ROUTE_W, w_top), (ROUTE_W + 1, e2 * w_top),
              (ROUTE_RANK, r1), (ROUTE_RANK + 1, r2))
    route = jnp.zeros_like(logits)
    for at, val in fields:
        route = jnp.where(lane == at, val, route)
    route_ref[...] = route


def _router(x32, w_router_padded):
    s, d = x32.shape
    tm = ROUTER_TM
    vmem = 2 * (tm * d * 4 + d * LANES * 4 + tm * LANES * 4) + 8 * tm * d * 4
    return pl.pallas_call(
        _router_kernel,
        grid=(s // tm,),
        in_specs=[pl.BlockSpec((tm, d), lambda i: (i, 0)), pl.BlockSpec((d, LANES), lambda i: (0, 0))],
        out_specs=[pl.BlockSpec((tm, LANES), lambda i: (i, 0)), pl.BlockSpec((1, LANES), lambda i: (0, 0))],
        out_shape=[jax.ShapeDtypeStruct((s, LANES), F32), jax.ShapeDtypeStruct((1, LANES), F32)],
        scratch_shapes=[pltpu.VMEM((1, LANES), F32)],
        compiler_params=_params(("arbitrary",), vmem),
        name="router_top2",
    )(x32, w_router_padded)


def _row_copy(src, src_row, dst, dst_row, sem):
    return pltpu.make_async_copy(src.at[pl.ds(src_row, 1)], dst.at[pl.ds(dst_row, 1)], sem)


def _dispatch_kernel(pos_ref, x_ref, xs_zero_hbm, xs_hbm, sem):
    del xs_zero_hbm
    tm = DISPATCH_TM

    def start(r, carry):
        for k in range(2):
            _row_copy(x_ref, r, xs_hbm, pos_ref[0, 0, k * tm + r], sem).start(priority=k)
        return carry

    lax.fori_loop(0, tm, start, 0, unroll=ROW_DMA_UNROLL)
    for k in range(2):
        pltpu.make_async_copy(x_ref, xs_hbm.at[pl.ds(0, tm)], sem).wait()


def _dispatch(pos_tiles, x32, n_rows):
    s, d = x32.shape
    tm = DISPATCH_TM
    return pl.pallas_call(
        _dispatch_kernel,
        grid=(s // tm,),
        in_specs=[
            pl.BlockSpec((1, 1, 2 * tm), lambda i: (i, 0, 0), memory_space=pltpu.SMEM),
            pl.BlockSpec((tm, d), lambda i: (i, 0)),
            pl.BlockSpec(memory_space=pl.ANY),
        ],
        out_specs=pl.BlockSpec(memory_space=pl.ANY),
        out_shape=jax.ShapeDtypeStruct((n_rows, d), F32),
        scratch_shapes=[pltpu.SemaphoreType.DMA(())],
        input_output_aliases={2: 0},
        compiler_params=_params(("arbitrary",), 2 * tm * d * 4 + (4 << 20)),
        name="moe_dispatch",
    )(pos_tiles, x32, jnp.zeros((n_rows, d), F32))


def _experts_kernel(tile_expert_ref, tile_rows_ref, n_used_ref, xs_hbm, w1_ref, w3_ref, w2_ref, ys_ref,
                    x16_ref, stage_ref, stage_sem):
    del tile_expert_ref
    i = pl.program_id(0)
    f = pl.program_id(1)
    n_rows = tile_rows_ref[i]
    tm = stage_ref.shape[0]

    def tile_copy(t):
        return pltpu.make_async_copy(xs_hbm.at[pl.ds(pl.multiple_of(t * tm, tm), tm)], stage_ref, stage_sem)

    @pl.when(f == 0)
    def _():
        ys_ref[...] = jnp.zeros_like(ys_ref)

    @pl.when((f == 0) & (i == 0))
    def _():
        tile_copy(0).start()

    @pl.when((f == 0) & (n_rows > 0))
    def _():
        tile_copy(i).wait()

    @pl.when((f == 1) & (i + 1 < n_used_ref[0]))
    def _():
        tile_copy(i + 1).start()

    def swiglu(rows):
        @pl.when(f == 0)
        def _():
            x16_ref[rows] = stage_ref[rows].astype(BF16)

        x = x16_ref[rows]
        h1 = _dot(x, w1_ref[...].astype(BF16))
        h = (h1 * jax.nn.sigmoid(h1) * _dot(x, w3_ref[...].astype(BF16))).astype(BF16)
        ys_ref[rows] += _dot(h, w2_ref[...].astype(BF16))

    for g in range(1, MOE_GRANULES + 1):
        @pl.when((n_rows > (g - 1) * MOE_TM) & (n_rows <= g * MOE_TM))
        def _():
            swiglu(slice(0, g * MOE_TM))


def _experts(tile_expert, tile_rows, n_used, xs, w1, w3, w2):
    n_rows, d = xs.shape
    ff = w1.shape[2]
    tm, tf = MOE_TM * MOE_GRANULES, MOE_TF
    n_f = ff // tf

    def chunk(i, f, nu):
        return jnp.where(i < nu[0], f, n_f - 1)

    assert n_f >= 2
    vmem = tm * d * 4 + 2 * (tm * d * 4 + 3 * d * tf * 4) + tm * d * 2 + 3 * d * tf * 2 + 6 * tm * tf * 4
    return pl.pallas_call(
        _experts_kernel,
        grid_spec=pltpu.PrefetchScalarGridSpec(
            num_scalar_prefetch=3,
            grid=(n_rows // tm, n_f),
            in_specs=[
                pl.BlockSpec(memory_space=pl.ANY),
                pl.BlockSpec((None, d, tf), lambda i, f, te, tr, nu: (te[i], 0, chunk(i, f, nu))),
                pl.BlockSpec((None, d, tf), lambda i, f, te, tr, nu: (te[i], 0, chunk(i, f, nu))),
                pl.BlockSpec((None, tf, d), lambda i, f, te, tr, nu: (te[i], chunk(i, f, nu), 0)),
            ],
            out_specs=pl.BlockSpec((tm, d), lambda i, f, te, tr, nu: (i, 0)),
            scratch_shapes=[pltpu.VMEM((tm, d), BF16), pltpu.VMEM((tm, d), F32), pltpu.SemaphoreType.DMA(())],
        ),
        out_shape=jax.ShapeDtypeStruct((n_rows, d), F32),
        compiler_params=_params(("arbitrary", "arbitrary"), vmem),
        name="moe_experts",
    )(tile_expert, tile_rows, n_used, xs, w1, w3, w2)


def _combine_kernel(pos_ref, pos_next_ref, route_ref, x_ref, ys_hbm, g_ref, b_ref, o_ref, buf, sems):
    tm = COMBINE_TM
    i = pl.program_id(0)
    slot = i % 2

    def start_tile(p_ref, slot):
        def start(r, carry):
            for k in range(2):
                _row_copy(ys_hbm, p_ref[0, 0, k * tm + r], buf.at[slot, k], r, sems.at[slot]).start(priority=k)
            return carry
        lax.fori_loop(0, tm, start, 0, unroll=ROW_DMA_UNROLL)

    @pl.when(i == 0)
    def _():
        start_tile(pos_ref, 0)

    @pl.when(i + 1 < pl.num_programs(0))
    def _():
        start_tile(pos_next_ref, 1 - slot)

    for k in range(2):
        pltpu.make_async_copy(ys_hbm.at[pl.ds(0, tm)], buf.at[slot, k], sems.at[slot]).wait()
    w0 = route_ref[:, ROUTE_W:ROUTE_W + 1]
    w1 = route_ref[:, ROUTE_W + 1:ROUTE_W + 2]
    y = w0 * buf[slot, 0] + w1 * buf[slot, 1]
    o_ref[...] = _layer_norm(ALPHA * x_ref[...] + y, g_ref[...], b_ref[...])


def _combine(pos_tiles, route, x32, ys, ln_g, ln_b):
    s, d = x32.shape
    tm = COMBINE_TM
    n = s // tm
    rowblk = lambda i: (i, 0)
    const = lambda i: (0, 0)
    vmem = 2 * 2 * tm * d * 4 + 2 * (2 * tm * d * 4 + tm * LANES * 4) + 6 * tm * d * 4
    return pl.pallas_call(
        _combine_kernel,
        grid=(n,),
        in_specs=[
            pl.BlockSpec((1, 1, 2 * tm), lambda i: (i, 0, 0), memory_space=pltpu.SMEM),
            pl.BlockSpec((1, 1, 2 * tm), lambda i: (jnp.minimum(i + 1, n - 1), 0, 0),
                         memory_space=pltpu.SMEM),
            pl.BlockSpec((tm, LANES), rowblk),
            pl.BlockSpec((tm, d), rowblk),
            pl.BlockSpec(memory_space=pl.ANY),
            pl.BlockSpec((1, d), const),
            pl.BlockSpec((1, d), const),
        ],
        out_specs=pl.BlockSpec((tm, d), rowblk),
        out_shape=jax.ShapeDtypeStruct((s, d), F32),
        scratch_shapes=[pltpu.VMEM((2, 2, tm, d), F32), pltpu.SemaphoreType.DMA((2,))],
        compiler_params=_params(("arbitrary",), vmem),
        name="moe_combine_ln",
    )(pos_tiles, pos_tiles, route, x32, ys, ln_g, ln_b)


def _pos_tiles(pos, tm):
    s = pos.shape[0]
    return pos.reshape(s // tm, tm, 2).transpose(0, 2, 1).reshape(s // tm, 1, 2 * tm)


def _moe(x32, w_router, w1, w3, w2, ln_g, ln_b):
    s, d = x32.shape
    tm = MOE_TM * MOE_GRANULES
    n_tiles = (2 * s) // tm + N_EXPERTS
    route, counts = _router(x32, jnp.pad(w_router, ((0, 0), (0, LANES - N_EXPERTS))))

    counts = counts[0, :N_EXPERTS].astype(jnp.int32)
    tiles_per_expert = (counts + tm - 1) // tm
    tile_end = jnp.cumsum(tiles_per_expert)
    tile_start = tile_end - tiles_per_expert
    n_used = tile_end[-1:]
    tile_ids = jnp.arange(n_tiles, dtype=jnp.int32)
    tile_expert = jnp.sum(jnp.minimum(tile_ids, n_used - 1)[:, None] >= tile_end[None, :],
                          axis=1).astype(jnp.int32)
    tile_rows = jnp.clip(counts[tile_expert] - (tile_ids - tile_start[tile_expert]) * tm, 0, tm)
    tile_rows = jnp.where(tile_ids < n_used, tile_rows, 0).astype(jnp.int32)
    experts = route[:, ROUTE_E:ROUTE_E + 2].astype(jnp.int32)
    pos = (tile_start * tm)[experts] + route[:, ROUTE_RANK:ROUTE_RANK + 2].astype(jnp.int32)

    xs = _dispatch(_pos_tiles(pos, DISPATCH_TM), x32, n_tiles * tm)
    ys = _experts(tile_expert, tile_rows, n_used, xs, w1, w3, w2)
    return _combine(_pos_tiles(pos, COMBINE_TM), route, x32, ys, ln_g, ln_b)


def kernel(x, w_in, b_gate, sg_w, sg_b, sg_ln_g, sg_ln_b, w_branch_a, w_branch_b, w_out,
           ln1_g, ln1_b, ffn_w1, ffn_w3, ffn_w2, moe_router, moe_w1, moe_w3, moe_w2,
           ln2_g, ln2_b):
    b, s, d = x.shape
    assert (b, s, d) == (1, SEQ, D_MODEL)
    x32 = x.reshape(s, d)
    x16 = x32.astype(BF16)
    for layer in range(DEPTH):
        qkv = _inproj(x16, w_in, layer, 0, 3 * SB_WIDTH, "qkv")
        u = _inproj(x16, w_in, layer, OFF_U, SG_WIDTH, "gelu")
        vn = _inproj(x16, w_in, layer, OFF_VG, SG_WIDTH, "gelu_ln",
                     (sg_ln_g[layer].reshape(1, -1), sg_ln_b[layer].reshape(1, -1)))
        gates = _inproj(x16, w_in, layer, OFF_GATE, 2 * d, "gate", (b_gate[layer].reshape(1, -1),))
        att = _attention(qkv)
        sgu = _sgu(u, vn, sg_w[layer], sg_b[layer].T)
        x32, x16 = _merge(att, sgu, gates, x32,
                          w_branch_a[layer].astype(BF16), w_branch_b[layer].astype(BF16),
                          w_out[layer].astype(BF16),
                          ln1_g[layer].reshape(1, d), ln1_b[layer].reshape(1, d))
        i = layer // 2
        g2, b2 = ln2_g[layer].reshape(1, d), ln2_b[layer].reshape(1, d)
        if layer % 2 == 0:
            x32, x16 = _ffn(x16, x32, ffn_w1[i].astype(BF16), ffn_w3[i].astype(BF16),
                            ffn_w2[i].astype(BF16), g2, b2)
        else:
            x32 = _moe(x32, moe_router[i], moe_w1[i], moe_w3[i], moe_w2[i], g2, b2)
            x16 = x32.astype(BF16) if layer + 1 < DEPTH else None
    return x32.reshape(b, s, d)
```
